```python
import math
import jax
import jax.numpy as jnp
from jax import lax
import numpy as np

D_MODEL = 1024
BATCH = 2
SEQ = 16384
DEPTH = 2

GRID_W = 64
CTX_LEN = 256
QBLK = 128
EPS = 1e-6
NEG_INF = -1e30
ROPE_DIM = 64
ROPE_BASE = 10000.0
N_MOD = 9
FFN_DIM = 2816
FFN_RES = 0.5
N_BRANCH = 3

DIFF_D = 64
DIFF_HEADS = D_MODEL // (2 * DIFF_D)
DIFF_V = 2 * DIFF_D
WIN_HEAD_DIM = 64
WIN_Q_HEADS = D_MODEL // WIN_HEAD_DIM
WIN_KV_HEADS = WIN_Q_HEADS // 4
WIN_GROUP = WIN_Q_HEADS // WIN_KV_HEADS
WINDOW = 128
WIN_SCALE = WIN_HEAD_DIM ** -0.5
MLA_HEADS = D_MODEL // 128
MLA_Q_RANK = 384
MLA_KV_RANK = 256
MLA_NOPE = 128
MLA_ROPE = ROPE_DIM
MLA_V = 128
MLA_SCALE = (MLA_NOPE + MLA_ROPE) ** -0.5

MIX_SPLITS = (
    DIFF_HEADS * 2 * DIFF_D,
    DIFF_HEADS * 2 * DIFF_D,
    DIFF_HEADS * DIFF_V,
    WIN_Q_HEADS * WIN_HEAD_DIM,
    WIN_KV_HEADS * WIN_HEAD_DIM,
    WIN_KV_HEADS * WIN_HEAD_DIM,
    MLA_Q_RANK,
    MLA_KV_RANK,
    MLA_ROPE,
    N_BRANCH * D_MODEL,
)
MIX_IN = sum(MIX_SPLITS)
MIX_CUTS = tuple(int(v) for v in np.cumsum(MIX_SPLITS)[:-1])

kernel_name = 'hybrid_diffusion_parallel_branch_block'


def rms_norm(x, g):
    xf = x.astype(jnp.float32)
    y = xf * lax.rsqrt(jnp.mean(xf * xf, axis=-1, keepdims=True) + EPS)
    return (y * g.astype(jnp.float32)).astype(x.dtype)


def modulate(x, shift, scale):
    return x * (1 + scale) + shift


def swiglu(x, w_in, w_out):
    a, b = jnp.split(x @ w_in, 2, axis=-1)
    return (jax.nn.silu(a) * b) @ w_out


def ffn_sublayer(t, shift, scale, gate, g_pre, g_post, w_in, w_out):
    y = swiglu(modulate(rms_norm(t, g_pre), shift, scale), w_in, w_out)
    return t + FFN_RES * gate * rms_norm(y, g_post)


def axial_rope_tables(n):
    rows = n // GRID_W
    row = jnp.repeat(jnp.arange(rows), GRID_W).astype(jnp.float32)
    col = jnp.tile(jnp.arange(GRID_W), rows).astype(jnp.float32)
    quarter = ROPE_DIM // 4
    inv_freq = 1.0 / (ROPE_BASE ** (jnp.arange(quarter, dtype=jnp.float32) / quarter))
    ang_r = row[:, None] * inv_freq
    ang_c = col[:, None] * inv_freq
    return (jnp.cos(ang_r), jnp.sin(ang_r), jnp.cos(ang_c), jnp.sin(ang_c))


def apply_rope2d(x, tabs):
    cr, sr, cc, sc = (t.reshape((1, t.shape[0]) + (1,) * (x.ndim - 3) + (t.shape[1],)) for t in tabs)
    xf = x.astype(jnp.float32)
    r1, r2, c1, c2 = jnp.split(xf, 4, axis=-1)
    out = jnp.concatenate([r1 * cr - r2 * sr, r2 * cr + r1 * sr,
                           c1 * cc - c2 * sc, c2 * cc + c1 * sc], axis=-1)
    return out.astype(x.dtype)


def maybe_rope(t, tabs):
    return t if tabs is None else apply_rope2d(t, tabs)


def project(hm, w_in):
    return jnp.split(hm @ w_in, MIX_CUTS, axis=-1)


def mixer_kv(parts, kv_norm_g, w_ukv, tabs):
    B, N = parts[0].shape[:2]
    a_k = maybe_rope(parts[1].reshape(B, N, DIFF_HEADS, 2, DIFF_D), tabs)
    a_v = parts[2].reshape(B, N, DIFF_HEADS, DIFF_V)
    b_k = maybe_rope(parts[4].reshape(B, N, WIN_KV_HEADS, WIN_HEAD_DIM), tabs)
    b_v = parts[5].reshape(B, N, WIN_KV_HEADS, WIN_HEAD_DIM)
    kv = (rms_norm(parts[7], kv_norm_g) @ w_ukv).reshape(B, N, MLA_HEADS, MLA_NOPE + MLA_V)
    c_kr = maybe_rope(parts[8], tabs)
    return (a_k, a_v, b_k, b_v, kv[..., :MLA_NOPE], c_kr, kv[..., MLA_NOPE:])


def mixer_q(parts, q_norm_g, w_uq, tabs):
    B, N = parts[0].shape[:2]
    a_q = maybe_rope(parts[0].reshape(B, N, DIFF_HEADS, 2, DIFF_D), tabs)
    b_q = maybe_rope(parts[3].reshape(B, N, WIN_KV_HEADS, WIN_GROUP, WIN_HEAD_DIM), tabs)
    q = (rms_norm(parts[6], q_norm_g) @ w_uq).reshape(B, N, MLA_HEADS, MLA_NOPE + MLA_ROPE)
    return (a_q, b_q, q[..., :MLA_NOPE], maybe_rope(q[..., MLA_NOPE:], tabs))


def diff_attend(q, k, v, lam):
    s = jnp.einsum('bqhnd,bkhnd->bhnqk', q, k, preferred_element_type=jnp.float32) / math.sqrt(DIFF_D)
    p = jax.nn.softmax(s, axis=-1)
    a = p[:, :, 0] - lam * p[:, :, 1]
    return jnp.einsum('bhqk,bkhe->bqhe', a.astype(v.dtype), v)


def sink_attend(q, segments, sink):
    B, Q = q.shape[:2]
    scores = []
    for k, v, mask in segments:
        s = jnp.einsum('bqhgd,bkhd->bhgqk', q, k, preferred_element_type=jnp.float32) * WIN_SCALE
        scores.append(s if mask is None else jnp.where(mask, s, NEG_INF))
    sink_col = jnp.broadcast_to(sink.astype(jnp.float32).reshape(1, WIN_KV_HEADS, WIN_GROUP, 1, 1),
                                (B, WIN_KV_HEADS, WIN_GROUP, Q, 1))
    p = jax.nn.softmax(jnp.concatenate(scores + [sink_col], axis=-1), axis=-1)
    outs = []
    start = 0
    for k, v, _ in segments:
        n = k.shape[1]
        outs.append(jnp.einsum('bhgqk,bkhd->bqhgd', p[..., start:start + n].astype(v.dtype), v))
        start += n
    return sum(outs[1:], outs[0])


def mla_attend(q_nope, q_rope, k_nope, k_rope, v):
    s = (jnp.einsum('bqhd,bkhd->bhqk', q_nope, k_nope, preferred_element_type=jnp.float32)
         + jnp.einsum('bqhr,bkr->bhqk', q_rope, k_rope, preferred_element_type=jnp.float32)) * MLA_SCALE
    p = jax.nn.softmax(s, axis=-1)
    return jnp.einsum('bhqk,bkhd->bqhd', p.astype(v.dtype), v)


def sweep_query_blocks(attend, qs, extra=()):
    S = qs[0].shape[1]
    nblk = S // QBLK
    blocks = tuple(jnp.moveaxis(q.reshape((q.shape[0], nblk, QBLK) + q.shape[2:]), 1, 0) for q in qs)
    out = lax.map(lambda xs: attend(*xs), blocks + tuple(extra))
    out = jnp.moveaxis(out, 0, 1)
    return out.reshape((out.shape[0], S) + out.shape[3:])


def band_blocks(t):
    B, S = t.shape[:2]
    nblk = S // QBLK
    tp = jnp.pad(t, [(0, 0), (QBLK, QBLK)] + [(0, 0)] * (t.ndim - 2))
    tp = tp.reshape((B, nblk + 2, QBLK) + t.shape[2:])
    band = jnp.concatenate([tp[:, :-2], tp[:, 1:-1], tp[:, 2:]], axis=2)
    return jnp.moveaxis(band, 1, 0)


def mix_latent(q, kv_lat, kv_ctx, lam, sink, band_mask, key_valid):
    a_q, b_q, c_qn, c_qr = q
    a_k, a_v, b_k, b_v, c_kn, c_kr, c_v = kv_lat
    ca_k, ca_v, cb_k, cb_v, cc_kn, cc_kr, cc_v = kv_ctx
    cat = lambda u, w: jnp.concatenate([u, w], axis=1)
    ka, va = cat(a_k, ca_k), cat(a_v, ca_v)
    ya = sweep_query_blocks(lambda qb: diff_attend(qb, ka, va, lam), (a_q,))
    yb = sweep_query_blocks(
        lambda qb, kb, vb, kvd: sink_attend(qb, ((kb, vb, band_mask & kvd[None, :]), (cb_k, cb_v, None)), sink),
        (b_q,), (band_blocks(b_k), band_blocks(b_v), key_valid))
    kn, kr, vc = cat(c_kn, cc_kn), cat(c_kr, cc_kr), cat(c_v, cc_v)
    yc = sweep_query_blocks(lambda qn, qr: mla_attend(qn, qr, kn, kr, vc), (c_qn, c_qr))
    return ya, yb, yc


def mix_context(q, kv_ctx, lam, sink):
    a_q, b_q, c_qn, c_qr = q
    a_k, a_v, b_k, b_v, c_kn, c_kr, c_v = kv_ctx
    ya = diff_attend(a_q, a_k, a_v, lam)
    yb = sink_attend(b_q, ((b_k, b_v, None),), sink)
    yc = mla_attend(c_qn, c_qr, c_kn, c_kr, c_v)
    return ya, yb, yc


def merge_branches(ya, yb, yc, gates, subln_g, lambda_init, branch_w, w_out):
    B, N = ya.shape[:2]
    ya = (rms_norm(ya, subln_g) * (1.0 - lambda_init)).reshape(B, N, DIFF_HEADS * DIFF_V)
    yb = yb.reshape(B, N, WIN_Q_HEADS * WIN_HEAD_DIM)
    yc = yc.reshape(B, N, MLA_HEADS * MLA_V)
    ga, gb, gc = jnp.split(jax.nn.sigmoid(gates), N_BRANCH, axis=-1)
    merged = ga * (ya @ branch_w[0]) + gb * (yb @ branch_w[1]) + gc * (yc @ branch_w[2])
    return merged @ w_out


def setup_inputs(seed: int = 0) -> dict:
    key = jax.random.key(seed)
    ks = jax.random.split(key, 19)
    nrm = lambda k, shape, scale: scale * jax.random.normal(k, shape, jnp.float32)
    return {
        'x': nrm(ks[0], (BATCH, SEQ, D_MODEL), 1.0),
        'c': nrm(ks[1], (BATCH, D_MODEL), 1.0),
        'ctx': nrm(ks[2], (BATCH, CTX_LEN, D_MODEL), 1.0),
        'c_ctx': nrm(ks[3], (D_MODEL,), 1.0),
        'ada_w': nrm(ks[4], (DEPTH, D_MODEL, N_MOD * D_MODEL), 0.3 * D_MODEL ** -0.5),
        'ada_b': nrm(ks[5], (DEPTH, N_MOD * D_MODEL), 0.02),
        'norm_g': 1.0 + nrm(ks[6], (DEPTH, 6, D_MODEL), 0.02),
        'ffn_w_in': nrm(ks[7], (DEPTH, 2, D_MODEL, 2 * FFN_DIM), D_MODEL ** -0.5),
        'ffn_w_out': nrm(ks[8], (DEPTH, 2, FFN_DIM, D_MODEL), FFN_DIM ** -0.5),
        'mix_w_in': nrm(ks[9], (DEPTH, D_MODEL, MIX_IN), D_MODEL ** -0.5),
        'diff_lambda': nrm(ks[10], (DEPTH, 4, DIFF_D), 0.1),
        'diff_subln_g': 1.0 + nrm(ks[11], (DEPTH, DIFF_V), 0.02),
        'win_sink': nrm(ks[12], (DEPTH, WIN_Q_HEADS), 0.5),
        'mla_q_norm_g': 1.0 + nrm(ks[13], (DEPTH, MLA_Q_RANK), 0.02),
        'mla_kv_norm_g': 1.0 + nrm(ks[14], (DEPTH, MLA_KV_RANK), 0.02),
        'mla_w_uq': nrm(ks[15], (DEPTH, MLA_Q_RANK, MLA_HEADS * (MLA_NOPE + MLA_ROPE)), MLA_Q_RANK ** -0.5),
        'mla_w_ukv': nrm(ks[16], (DEPTH, MLA_KV_RANK, MLA_HEADS * (MLA_NOPE + MLA_V)), MLA_KV_RANK ** -0.5),
        'branch_w': nrm(ks[17], (DEPTH, N_BRANCH, D_MODEL, D_MODEL), D_MODEL ** -0.5),
        'mix_w_out': nrm(ks[18], (DEPTH, D_MODEL, D_MODEL), D_MODEL ** -0.5),
    }


def reference(x, c, ctx, c_ctx, ada_w, ada_b, norm_g, ffn_w_in, ffn_w_out, mix_w_in,
              diff_lambda, diff_subln_g, win_sink, mla_q_norm_g, mla_kv_norm_g,
              mla_w_uq, mla_w_ukv, branch_w, mix_w_out):
    B, S, _ = x.shape
    tabs = axial_rope_tables(S)
    nblk = S // QBLK
    qq = jnp.arange(QBLK)[:, None]
    kk = jnp.arange(3 * QBLK)[None, :]
    band_mask = jnp.abs(kk - QBLK - qq) <= WINDOW
    kpos = jnp.arange(nblk)[:, None] * QBLK - QBLK + kk
    key_valid = (kpos >= 0) & (kpos < S)
    silu_c = jax.nn.silu(c)[:, None, :]
    silu_cc = jax.nn.silu(c_ctx)
    h = ctx
    for l in range(DEPTH):
        last = l == DEPTH - 1
        lambda_init = 0.8 - 0.6 * math.exp(-0.3 * l)
        mx = jnp.split(silu_c @ ada_w[l] + ada_b[l], N_MOD, axis=-1)
        mc = jnp.split(silu_cc @ ada_w[l] + ada_b[l], N_MOD, axis=-1)
        g = norm_g[l]
        dl = diff_lambda[l].astype(jnp.float32)
        lam = jnp.exp(jnp.sum(dl[0] * dl[1])) - jnp.exp(jnp.sum(dl[2] * dl[3])) + lambda_init
        x = ffn_sublayer(x, mx[0], mx[1], mx[2], g[0], g[1], ffn_w_in[l, 0], ffn_w_out[l, 0])
        h = ffn_sublayer(h, mc[0], mc[1], mc[2], g[0], g[1], ffn_w_in[l, 0], ffn_w_out[l, 0])
        px = project(modulate(rms_norm(x, g[2]), mx[3], mx[4]), mix_w_in[l])
        pc = project(modulate(rms_norm(h, g[2]), mc[3], mc[4]), mix_w_in[l])
        kv_x = mixer_kv(px, mla_kv_norm_g[l], mla_w_ukv[l], tabs)
        kv_c = mixer_kv(pc, mla_kv_norm_g[l], mla_w_ukv[l], None)
        q_x = mixer_q(px, mla_q_norm_g[l], mla_w_uq[l], tabs)
        ya, yb, yc = mix_latent(q_x, kv_x, kv_c, lam, win_sink[l], band_mask, key_valid)
        y = merge_branches(ya, yb, yc, px[9], diff_subln_g[l], lambda_init, branch_w[l], mix_w_out[l])
        x = x + mx[5] * rms_norm(y, g[3])
        if not last:
            q_c = mixer_q(pc, mla_q_norm_g[l], mla_w_uq[l], None)
            ca, cb, cc = mix_context(q_c, kv_c, lam, win_sink[l])
            yh = merge_branches(ca, cb, cc, pc[9], diff_subln_g[l], lambda_init, branch_w[l], mix_w_out[l])
            h = h + mc[5] * rms_norm(yh, g[3])
            h = ffn_sublayer(h, mc[6], mc[7], mc[8], g[4], g[5], ffn_w_in[l, 1], ffn_w_out[l, 1])
        x = ffn_sublayer(x, mx[6], mx[7], mx[8], g[4], g[5], ffn_w_in[l, 1], ffn_w_out[l, 1])
    return x
```

```python
import functools
import math

import jax
import jax.numpy as jnp
from jax import lax
from jax.experimental import pallas as pl
from jax.experimental.pallas import tpu as pltpu

F32 = jnp.float32
BF16 = jnp.bfloat16

GRID_W = 64
QBLK = 128
EPS = 1e-6
NEG_INF = -1e30
ROPE_DIM = 64
ROPE_BASE = 10000.0
N_MOD = 9
FFN_RES = 0.5
DIFF_HEADS = 8
DIFF_D = 64
WIN_Q_HEADS = 16
WIN_KV_HEADS = 4
WIN_GROUP = 4
WIN_HEAD_DIM = 64
WIN_SCALE = WIN_HEAD_DIM ** -0.5
MLA_HEADS = 8
MLA_Q_RANK = 384
MLA_KV_RANK = 256
MLA_NOPE = 128
MLA_ROPE = 64
MLA_V = 128
MLA_SCALE = (MLA_NOPE + MLA_ROPE) ** -0.5

LANES = 128
VMEM_LIMIT = 56 * 1024 * 1024


def _cparams(sem):
    return pltpu.CompilerParams(dimension_semantics=sem, vmem_limit_bytes=VMEM_LIMIT)


def _resident(shape):
    nd = len(shape)
    return pl.BlockSpec(shape, lambda *_: (0,) * nd, pipeline_mode=pl.Buffered(1))


def _rms(x, g):
    return x * lax.rsqrt(jnp.mean(x * x, axis=-1, keepdims=True) + EPS) * g


def _norm_mod(x, g, shift, scale):
    return _rms(x, g) * (1.0 + scale) + shift


def _dot(a, b):
    return jnp.dot(a, b, preferred_element_type=F32)


def _dot_nt(a, b):
    return lax.dot_general(a, b, (((1,), (1,)), ((), ())), preferred_element_type=F32)


def _mod_spec(arr):
    d = arr.shape[-1]
    if arr.shape[0] == 1:
        return pl.BlockSpec((1, 1, d), lambda b, *_: (0, 0, 0))
    return pl.BlockSpec((1, 1, d), lambda b, *_: (b, 0, 0))


def _ada_kernel(c_ref, w_ref, b_ref, o_ref):
    c = c_ref[...]
    a = c * jax.nn.sigmoid(c)
    o_ref[0] = jnp.dot(a, w_ref[0], preferred_element_type=F32,
                       precision=lax.Precision.HIGHEST) + b_ref[0]


def _ada_mods(cvecs, ada_w, ada_b):
    depth, d, nd = ada_w.shape
    rows = cvecs.shape[0]
    tn = 1152 if nd % 1152 == 0 else nd
    return pl.pallas_call(
        _ada_kernel,
        grid=(depth, nd // tn),
        in_specs=[pl.BlockSpec((rows, d), lambda l, j: (0, 0)),
                  pl.BlockSpec((1, d, tn), lambda l, j: (l, 0, j)),
                  pl.BlockSpec((1, 1, tn), lambda l, j: (l, 0, j))],
        out_specs=pl.BlockSpec((1, rows, tn), lambda l, j: (l, 0, j)),
        out_shape=jax.ShapeDtypeStruct((depth, rows, nd), F32),
        compiler_params=_cparams(("parallel", "parallel")),
        name="ada_mods",
    )(cvecs, ada_w, ada_b.reshape(depth, 1, nd))


def _ffn_kernel(x_ref, sh_ref, sc_ref, gt_ref, gpre_ref, gpost_ref, win_ref, wout_ref, o_ref,
                *, ffn_dim, chunk):
    x = x_ref[0]
    xm = _norm_mod(x, gpre_ref[...], sh_ref[0], sc_ref[0]).astype(BF16)
    acc = jnp.zeros(x.shape, F32)
    for c in range(ffn_dim // chunk):
        a = _dot(xm, win_ref[:, c * chunk:(c + 1) * chunk])
        b = _dot(xm, win_ref[:, ffn_dim + c * chunk:ffn_dim + (c + 1) * chunk])
        h = (a * jax.nn.sigmoid(a) * b).astype(BF16)
        acc = acc + _dot(h, wout_ref[c * chunk:(c + 1) * chunk, :])
    o_ref[0] = x + FFN_RES * gt_ref[0] * _rms(acc, gpost_ref[...])


def _ffn(x, shift, scale, gate, g_pre, g_post, w_in, w_out):
    bsz, n, d = x.shape
    ffn_dim = w_out.shape[0]
    tm = min(512, n)
    tok = pl.BlockSpec((1, tm, d), lambda b, i: (b, i, 0))
    gain = pl.BlockSpec((1, d), lambda b, i: (0, 0))
    return pl.pallas_call(
        functools.partial(_ffn_kernel, ffn_dim=ffn_dim, chunk=256),
        grid=(bsz, n // tm),
        in_specs=[tok, _mod_spec(shift), _mod_spec(scale), _mod_spec(gate), gain, gain,
                  _resident(w_in.shape), _resident(w_out.shape)],
        out_specs=tok,
        out_shape=jax.ShapeDtypeStruct(x.shape, F32),
        compiler_params=_cparams(("parallel", "parallel")),
        name="ffn",
    )(x, shift, scale, gate, g_pre, g_post, w_in, w_out)


def _rope_masks(tm):
    lane = lax.broadcasted_iota(jnp.int32, (tm, LANES), 1)
    even = ((lane // (ROPE_DIM // 4)) & 1) == 0
    lo = lane < (LANES // 2)
    return even, lo


def _rope(blk, cos, sin, even):
    q = ROPE_DIM // 4
    partner = jnp.where(even, pltpu.roll(blk, LANES - q, 1), pltpu.roll(blk, q, 1))
    return blk * cos + partner * sin


def _proj_diff_kernel(x_ref, sh_ref, sc_ref, g_ref, cos_ref, sin_ref, w_ref,
                      q_ref, k_ref, v_ref):
    hm = _norm_mod(x_ref[0], g_ref[...], sh_ref[0], sc_ref[0]).astype(BF16)
    tm = hm.shape[0]
    cos = cos_ref[...]
    sin = sin_ref[...]
    even, lo = _rope_masks(tm)
    hw = DIFF_HEADS * 2 * DIFF_D
    q = _dot(hm, w_ref[:, 0:hw])
    k = _dot(hm, w_ref[:, hw:2 * hw])
    scale = 1.0 / math.sqrt(DIFF_D)
    for h in range(DIFF_HEADS):
        qb = _rope(q[:, h * LANES:(h + 1) * LANES], cos, sin, even) * scale
        q_ref[0, :, (2 * h) * LANES:(2 * h + 1) * LANES] = jnp.where(lo, qb, 0.0).astype(BF16)
        q_ref[0, :, (2 * h + 1) * LANES:(2 * h + 2) * LANES] = jnp.where(lo, 0.0, qb).astype(BF16)
        kb = _rope(k[:, h * LANES:(h + 1) * LANES], cos, sin, even)
        k_ref[0, :, h * LANES:(h + 1) * LANES] = kb.astype(BF16)
    v_ref[0] = _dot(hm, w_ref[:, 2 * hw:3 * hw]).astype(BF16)


def _proj_diff(x, shift, scale, g, cos, sin, w):
    bsz, n, d = x.shape
    tm = min(512, n)
    hw = DIFF_HEADS * 2 * DIFF_D
    tok = lambda c: pl.BlockSpec((1, tm, c), lambda b, i: (b, i, 0))
    tab = pl.BlockSpec((tm, LANES), lambda b, i: (i, 0))
    return pl.pallas_call(
        _proj_diff_kernel,
        grid=(bsz, n // tm),
        in_specs=[tok(d), _mod_spec(shift), _mod_spec(scale),
                  pl.BlockSpec((1, d), lambda b, i: (0, 0)), tab, tab, _resident(w.shape)],
        out_specs=[tok(2 * hw), tok(hw), tok(hw)],
        out_shape=[jax.ShapeDtypeStruct((bsz, n, 2 * hw), BF16),
                   jax.ShapeDtypeStruct((bsz, n, hw), BF16),
                   jax.ShapeDtypeStruct((bsz, n, hw), BF16)],
        compiler_params=_cparams(("parallel", "parallel")),
        name="proj_diff",
    )(x, shift, scale, g, cos, sin, w)


_WQ0, _WK0, _WV0 = 0, 1024, 1280
_CQ0, _CKV0, _KR0, _PB_COLS = 1536, 1920, 2176, 2304


def _proj_wm_kernel(x_ref, sh_ref, sc_ref, g_ref, cos_ref, sin_ref, w_ref, wuq_ref, wukv_ref,
                    qn_ref, kvn_ref, wq_ref, wk_ref, wv_ref, mq_ref, mk_ref, mv_ref):
    hm = _norm_mod(x_ref[0], g_ref[...], sh_ref[0], sc_ref[0]).astype(BF16)
    tm = hm.shape[0]
    cos = cos_ref[...]
    sin = sin_ref[...]
    even, lo = _rope_masks(tm)
    half = LANES // 2
    p = _dot(hm, w_ref[...])

    def split_heads(blk, out_ref, j):
        out_ref[0, :, (2 * j) * LANES:(2 * j + 1) * LANES] = jnp.where(lo, blk, 0.0).astype(BF16)
        out_ref[0, :, (2 * j + 1) * LANES:(2 * j + 2) * LANES] = (
            jnp.where(lo, pltpu.roll(blk, half, 1), 0.0).astype(BF16))

    for j in range(WIN_Q_HEADS // 2):
        blk = _rope(p[:, _WQ0 + j * LANES:_WQ0 + (j + 1) * LANES], cos, sin, even) * WIN_SCALE
        split_heads(blk, wq_ref, j)
    for j in range(WIN_KV_HEADS // 2):
        split_heads(_rope(p[:, _WK0 + j * LANES:_WK0 + (j + 1) * LANES], cos, sin, even), wk_ref, j)
        split_heads(p[:, _WV0 + j * LANES:_WV0 + (j + 1) * LANES], wv_ref, j)

    cq = _rms(p[:, _CQ0:_CQ0 + MLA_Q_RANK], qn_ref[...]).astype(BF16)
    q2 = _dot(cq, wuq_ref[...])
    ckv = _rms(p[:, _CKV0:_CKV0 + MLA_KV_RANK], kvn_ref[...]).astype(BF16)
    kv = _dot(ckv, wukv_ref[...])
    kr = _rope(p[:, _KR0:_KR0 + LANES], cos, sin, even).astype(BF16)
    for h in range(MLA_HEADS):
        c0 = 2 * h * LANES
        mq_ref[0, :, c0:c0 + LANES] = (q2[:, c0:c0 + LANES] * MLA_SCALE).astype(BF16)
        qr = _rope(q2[:, c0 + LANES:c0 + 2 * LANES], cos, sin, even) * MLA_SCALE
        mq_ref[0, :, c0 + LANES:c0 + 2 * LANES] = qr.astype(BF16)
        mk_ref[0, :, c0:c0 + LANES] = kv[:, c0:c0 + LANES].astype(BF16)
        mk_ref[0, :, c0 + LANES:c0 + 2 * LANES] = kr
        mv_ref[0, :, h * LANES:(h + 1) * LANES] = kv[:, c0 + LANES:c0 + 2 * LANES].astype(BF16)


def _proj_wm(x, shift, scale, g, cos, sin, w, wuq, wukv, qn, kvn):
    bsz, n, d = x.shape
    tm = min(512, n)
    tok = lambda c: pl.BlockSpec((1, tm, c), lambda b, i: (b, i, 0))
    tab = pl.BlockSpec((tm, LANES), lambda b, i: (i, 0))
    row = lambda c: pl.BlockSpec((1, c), lambda b, i: (0, 0))
    widths = (WIN_Q_HEADS * LANES, WIN_KV_HEADS * LANES, WIN_KV_HEADS * LANES,
              MLA_HEADS * 2 * LANES, MLA_HEADS * 2 * LANES, MLA_HEADS * LANES)
    return pl.pallas_call(
        _proj_wm_kernel,
        grid=(bsz, n // tm),
        in_specs=[tok(d), _mod_spec(shift), _mod_spec(scale), row(d), tab, tab,
                  _resident(w.shape), _resident(wuq.shape), _resident(wukv.shape),
                  row(MLA_Q_RANK), row(MLA_KV_RANK)],
        out_specs=[tok(c) for c in widths],
        out_shape=[jax.ShapeDtypeStruct((bsz, n, c), BF16) for c in widths],
        compiler_params=_cparams(("parallel", "parallel")),
        name="proj_wm",
    )(x, shift, scale, g, cos, sin, w, wuq, wukv, qn, kvn)


def _flash_loop(qt, k_ref, vt_ref, m_sc, l_sc, acc_sc, nc, tk):
    m_sc[...] = jnp.full(m_sc.shape, NEG_INF, F32)
    l_sc[...] = jnp.zeros(l_sc.shape, F32)
    acc_sc[...] = jnp.zeros(acc_sc.shape, F32)

    def body(j, carry):
        start = pl.multiple_of(j * tk, tk)
        k = k_ref[0, 0, pl.ds(start, tk), :]
        s = _dot(k, qt)
        m_prev = m_sc[...]
        m_new = jnp.maximum(m_prev, jnp.max(s, axis=0, keepdims=True))
        alpha = jnp.exp(m_prev - m_new)
        p = jnp.exp(s - m_new)
        l_sc[...] = alpha * l_sc[...] + jnp.sum(p, axis=0, keepdims=True)
        acc_sc[...] = alpha * acc_sc[...] + _dot(vt_ref[0, 0, j], p.astype(BF16))
        m_sc[...] = m_new
        return carry

    lax.fori_loop(0, nc, body, 0)


def _diff_flash_kernel(qt_ref, k_ref, vt_ref, dl_ref, g_ref, o_ref, m_sc, l_sc, acc_sc,
                       *, nc, tk, lambda_init):
    _flash_loop(qt_ref[0, 0, 0], k_ref, vt_ref, m_sc, l_sc, acc_sc, nc, tk)
    tq = o_ref.shape[1]
    o = acc_sc[...] / l_sc[...]
    dl = dl_ref[...]
    lam = (jnp.exp(jnp.sum(dl[0:1] * dl[1:2], axis=1, keepdims=True))
           - jnp.exp(jnp.sum(dl[2:3] * dl[3:4], axis=1, keepdims=True)) + lambda_init)
    y = (o[:, :tq] - lam * o[:, tq:]).T
    o_ref[0] = (_rms(y, g_ref[...]) * (1.0 - lambda_init)).astype(BF16)


def _mla_flash_kernel(qt_ref, k_ref, vt_ref, o_ref, m_sc, l_sc, acc_sc, *, nc, tk):
    _flash_loop(qt_ref[0, 0, 0], k_ref, vt_ref, m_sc, l_sc, acc_sc, nc, tk)
    o_ref[0] = (acc_sc[...] / l_sc[...]).T.astype(BF16)


def _pick_chunk(nk):
    for cand in (1280, 1024, 768, 512, 256, 128):
        if nk % cand == 0:
            return cand
    raise ValueError(f"unsupported key count {nk}")


def _flash(mode, q, k, v, tq, extra=(), lambda_init=0.0):
    bsz, s, _ = q.shape
    nk = k.shape[1]
    heads = v.shape[2] // LANES
    dk = k.shape[2] // heads
    maps = q.shape[2] // (heads * dk)
    tq = min(tq, s)
    nq = s // tq
    w = maps * tq
    tk = _pick_chunk(nk)
    nc = nk // tk
    qt = q.reshape(bsz, nq, tq, heads, maps, dk).transpose(0, 3, 1, 5, 4, 2).reshape(bsz, heads, nq, dk, w)
    kh = k.reshape(bsz, nk, heads, dk).transpose(0, 2, 1, 3)
    vt = v.reshape(bsz, nc, tk, heads, LANES).transpose(0, 3, 1, 4, 2)
    in_specs = [pl.BlockSpec((1, 1, 1, dk, w), lambda b, h, i: (b, h, i, 0, 0)),
                pl.BlockSpec((1, 1, nk, dk), lambda b, h, i: (b, h, 0, 0)),
                pl.BlockSpec((1, 1, nc, LANES, tk), lambda b, h, i: (b, h, 0, 0, 0))]
    if mode == "diff":
        kern = functools.partial(_diff_flash_kernel, nc=nc, tk=tk, lambda_init=lambda_init)
        in_specs += [pl.BlockSpec(extra[0].shape, lambda b, h, i: (0, 0)),
                     pl.BlockSpec(extra[1].shape, lambda b, h, i: (0, 0))]
    else:
        kern = functools.partial(_mla_flash_kernel, nc=nc, tk=tk)
    return pl.pallas_call(
        kern,
        grid=(bsz, heads, nq),
        in_specs=in_specs,
        out_specs=pl.BlockSpec((1, tq, LANES), lambda b, h, i: (b, i, h)),
        out_shape=jax.ShapeDtypeStruct((bsz, s, heads * LANES), BF16),
        scratch_shapes=[pltpu.VMEM((1, w), F32), pltpu.VMEM((1, w), F32),
                        pltpu.VMEM((LANES, w), F32)],
        compiler_params=_cparams(("parallel", "parallel", "arbitrary")),
        name=f"{mode}_flash",
    )(qt, kh, vt, *extra)


def _win_kernel(*refs, band, nblk):
    if band:
        (q_ref, kp_ref, ko_ref, kn_ref, vp_ref, vo_ref, vn_ref, kc_ref, vc_ref,
         sink_ref, o_ref) = refs
    else:
        q_ref, kc_ref, vc_ref, sink_ref, o_ref = refs
    t = q_ref.shape[1]
    nctx = kc_ref.shape[1]
    rows = WIN_GROUP * t
    if band:
        i = pl.program_id(1)
        ncols = 3 * t + nctx
        qq = lax.broadcasted_iota(jnp.int32, (rows, ncols), 0) % t
        cc = lax.broadcasted_iota(jnp.int32, (rows, ncols), 1)
        off_prev = jnp.where(i > 0, 0, t)
        off_next = jnp.where(i < nblk - 1, 0, t)
        is_prev = cc < t
        is_next = jnp.logical_and(cc >= 2 * t, cc < 3 * t)
        bad_prev = jnp.logical_and(is_prev, cc < qq + off_prev)
        bad_next = jnp.logical_and(is_next, cc - 2 * t > qq - off_next)
        mask = jnp.logical_not(jnp.logical_or(bad_prev, bad_next))
    grp = lax.broadcasted_iota(jnp.int32, (rows, 1), 0) // t
    for kvh in range(WIN_KV_HEADS):
        ksl = slice(kvh * LANES, (kvh + 1) * LANES)
        if band:
            kcat = jnp.concatenate([kp_ref[0, :, ksl], ko_ref[0, :, ksl], kn_ref[0, :, ksl],
                                    kc_ref[0, :, ksl]], axis=0)
            vcat = jnp.concatenate([vp_ref[0, :, ksl], vo_ref[0, :, ksl], vn_ref[0, :, ksl],
                                    vc_ref[0, :, ksl]], axis=0)
        else:
            kcat = kc_ref[0, :, ksl]
            vcat = vc_ref[0, :, ksl]
        h0 = kvh * WIN_GROUP
        qs = jnp.concatenate([q_ref[0, :, (h0 + g) * LANES:(h0 + g + 1) * LANES]
                              for g in range(WIN_GROUP)], axis=0)
        sink = jnp.zeros((rows, 1), F32)
        for g in range(WIN_GROUP):
            sink = jnp.where(grp == g, sink_ref[h0 + g:h0 + g + 1, 0:1], sink)
        s = _dot_nt(qs, kcat)
        if band:
            s = jnp.where(mask, s, NEG_INF)
        m = jnp.maximum(jnp.max(s, axis=1, keepdims=True), sink)
        p = jnp.exp(s - m)
        l = jnp.sum(p, axis=1, keepdims=True) + jnp.exp(sink - m)
        o = _dot(p.astype(BF16), vcat) / l
        for pair in range(WIN_GROUP // 2):
            a = o[(2 * pair) * t:(2 * pair + 1) * t]
            b = o[(2 * pair + 1) * t:(2 * pair + 2) * t]
            blk = kvh * (WIN_GROUP // 2) + pair
            o_ref[0, :, blk * LANES:(blk + 1) * LANES] = (a + pltpu.roll(b, LANES // 2, 1)).astype(BF16)


def _win_attn(q, k, v, kc, vc, sink, band):
    bsz, s, qw = q.shape
    kw = kc.shape[2]
    nctx = kc.shape[1]
    out_w = WIN_Q_HEADS * WIN_HEAD_DIM
    if band:
        t = QBLK
        nblk = s // t
        kspec = lambda f: pl.BlockSpec((1, t, kw), f)
        prev = lambda b, i: (b, jnp.maximum(i - 1, 0), 0)
        own = lambda b, i: (b, i, 0)
        nxt = lambda b, i: (b, jnp.minimum(i + 1, nblk - 1), 0)
        ctx = pl.BlockSpec((1, nctx, kw), lambda b, i: (b, 0, 0))
        in_specs = [pl.BlockSpec((1, t, qw), own), kspec(prev), kspec(own), kspec(nxt),
                    kspec(prev), kspec(own), kspec(nxt), ctx, ctx]
        args = (q, k, k, k, v, v, v, kc, vc)
    else:
        t = s
        nblk = 1
        ctx = pl.BlockSpec((1, nctx, kw), lambda b, i: (b, 0, 0))
        in_specs = [pl.BlockSpec((1, t, qw), lambda b, i: (b, i, 0)), ctx, ctx]
        args = (q, kc, vc)
    in_specs.append(pl.BlockSpec(sink.shape, lambda b, i: (0, 0)))
    return pl.pallas_call(
        functools.partial(_win_kernel, band=band, nblk=nblk),
        grid=(bsz, nblk),
        in_specs=in_specs,
        out_specs=pl.BlockSpec((1, t, out_w), lambda b, i: (b, i, 0)),
        out_shape=jax.ShapeDtypeStruct((bsz, s, out_w), BF16),
        compiler_params=_cparams(("parallel", "parallel")),
        name="win_attn" if band else "win_attn_ctx",
    )(*args, sink)


def _merge_kernel(x_ref, ya_ref, yb_ref, yc_ref, sh_ref, sc_ref, gt_ref, g2_ref, g3_ref,
                  wg_ref, bw_ref, wo_ref, o_ref):
    x = x_ref[0]
    d = x.shape[1]
    hm = _norm_mod(x, g2_ref[...], sh_ref[0], sc_ref[0]).astype(BF16)
    merged = jnp.zeros(x.shape, F32)
    for i, y_ref in enumerate((ya_ref, yb_ref, yc_ref)):
        gate = jax.nn.sigmoid(_dot(hm, wg_ref[:, i * d:(i + 1) * d]))
        merged = merged + gate * _dot(y_ref[0], bw_ref[i])
    y = _dot(merged.astype(BF16), wo_ref[...])
    o_ref[0] = x + gt_ref[0] * _rms(y, g3_ref[...])


def _merge(x, ya, yb, yc, shift, scale, gate, g2, g3, wg, bw, wo):
    bsz, n, d = x.shape
    tm = min(512, n)
    tok = pl.BlockSpec((1, tm, d), lambda b, i: (b, i, 0))
    gain = pl.BlockSpec((1, d), lambda b, i: (0, 0))
    return pl.pallas_call(
        _merge_kernel,
        grid=(bsz, n // tm),
        in_specs=[tok, tok, tok, tok, _mod_spec(shift), _mod_spec(scale), _mod_spec(gate),
                  gain, gain, _resident(wg.shape), _resident(bw.shape), _resident(wo.shape)],
        out_specs=tok,
        out_shape=jax.ShapeDtypeStruct(x.shape, F32),
        compiler_params=_cparams(("parallel", "parallel")),
        name="merge",
    )(x, ya, yb, yc, shift, scale, gate, g2, g3, wg, bw, wo)


def _rope_tables(n):
    quarter = ROPE_DIM // 4
    pos = jnp.arange(n)
    row = (pos // GRID_W).astype(F32)
    col = (pos % GRID_W).astype(F32)
    inv_freq = 1.0 / (ROPE_BASE ** (jnp.arange(quarter, dtype=F32) / quarter))
    ang_r = row[:, None] * inv_freq
    ang_c = col[:, None] * inv_freq
    cos = jnp.concatenate([jnp.cos(ang_r)] * 2 + [jnp.cos(ang_c)] * 2, axis=1)
    sin = jnp.concatenate([-jnp.sin(ang_r), jnp.sin(ang_r), -jnp.sin(ang_c), jnp.sin(ang_c)], axis=1)
    return jnp.tile(cos, (1, 2)), jnp.tile(sin, (1, 2))


def _layer_weights(l, ffn_w_in, ffn_w_out, mix_w_in, mla_w_uq, mla_w_ukv, branch_w, mix_w_out):
    d = mix_w_in.shape[1]
    w = mix_w_in[l]
    hw = DIFF_HEADS * 2 * DIFF_D
    wq_n = WIN_Q_HEADS * WIN_HEAD_DIM
    wk_n = WIN_KV_HEADS * WIN_HEAD_DIM
    c = 3 * hw
    w_diff = w[:, :c]
    rest_n = wq_n + 2 * wk_n + MLA_Q_RANK + MLA_KV_RANK + MLA_ROPE
    w_wm = jnp.concatenate([w[:, c:c + rest_n], jnp.zeros((d, _PB_COLS - rest_n), w.dtype)], axis=1)
    w_gate = w[:, c + rest_n:]
    wuq = mla_w_uq[l].reshape(MLA_Q_RANK, MLA_HEADS, MLA_NOPE + MLA_ROPE)
    wuq = jnp.pad(wuq, ((0, 0), (0, 0), (0, 2 * LANES - MLA_NOPE - MLA_ROPE)))
    wuq = wuq.reshape(MLA_Q_RANK, MLA_HEADS * 2 * LANES)
    return dict(
        ffn_in=[ffn_w_in[l, i].astype(BF16) for i in range(2)],
        ffn_out=[ffn_w_out[l, i].astype(BF16) for i in range(2)],
        w_diff=w_diff.astype(BF16), w_wm=w_wm.astype(BF16), w_gate=w_gate.astype(BF16),
        wuq=wuq.astype(BF16), wukv=mla_w_ukv[l].astype(BF16),
        bw=branch_w[l].astype(BF16), wo=mix_w_out[l].astype(BF16))


def kernel(x, c, ctx, c_ctx, ada_w, ada_b, norm_g, ffn_w_in, ffn_w_out, mix_w_in, diff_lambda,
           diff_subln_g, win_sink, mla_q_norm_g, mla_kv_norm_g, mla_w_uq, mla_w_ukv, branch_w,
           mix_w_out):
    bsz, s, d = x.shape
    nctx = ctx.shape[1]
    depth = ada_w.shape[0]

    cvecs = jnp.zeros((8, d), F32).at[:bsz].set(c).at[bsz].set(c_ctx)
    mods = _ada_mods(cvecs, ada_w, ada_b).reshape(depth, 8, N_MOD, d)

    cos_x, sin_x = _rope_tables(s)
    cos_c = jnp.ones((nctx, LANES), F32)
    sin_c = jnp.zeros((nctx, LANES), F32)

    h = ctx
    for l in range(depth):
        last = l == depth - 1
        lambda_init = 0.8 - 0.6 * math.exp(-0.3 * l)
        wts = _layer_weights(l, ffn_w_in, ffn_w_out, mix_w_in, mla_w_uq, mla_w_ukv, branch_w,
                             mix_w_out)
        mx = [mods[l, :bsz, k][:, None, :] for k in range(N_MOD)]
        mc = [mods[l, bsz:bsz + 1, k][:, None, :] for k in range(N_MOD)]
        g = [norm_g[l, k][None, :] for k in range(6)]
        qn = mla_q_norm_g[l][None, :]
        kvn = mla_kv_norm_g[l][None, :]
        dl = diff_lambda[l].astype(F32)
        subln = diff_subln_g[l][None, :]
        sink = jnp.broadcast_to(win_sink[l].astype(F32)[:, None], (WIN_Q_HEADS, LANES))

        x = _ffn(x, mx[0], mx[1], mx[2], g[0], g[1], wts["ffn_in"][0], wts["ffn_out"][0])
        h = _ffn(h, mc[0], mc[1], mc[2], g[0], g[1], wts["ffn_in"][0], wts["ffn_out"][0])

        dq_x, dk_x, dv_x = _proj_diff(x, mx[3], mx[4], g[2], cos_x, sin_x, wts["w_diff"])
        dq_c, dk_c, dv_c = _proj_diff(h, mc[3], mc[4], g[2], cos_c, sin_c, wts["w_diff"])
        wq_x, wk_x, wv_x, mq_x, mk_x, mv_x = _proj_wm(
            x, mx[3], mx[4], g[2], cos_x, sin_x, wts["w_wm"], wts["wuq"], wts["wukv"], qn, kvn)
        wq_c, wk_c, wv_c, mq_c, mk_c, mv_c = _proj_wm(
            h, mc[3], mc[4], g[2], cos_c, sin_c, wts["w_wm"], wts["wuq"], wts["wukv"], qn, kvn)

        cat = lambda u, w: jnp.concatenate([u, w], axis=1)
        ya = _flash("diff", dq_x, cat(dk_x, dk_c), cat(dv_x, dv_c), 256, (dl, subln), lambda_init)
        yb = _win_attn(wq_x, wk_x, wv_x, wk_c, wv_c, sink, band=True)
        yc = _flash("mla", mq_x, cat(mk_x, mk_c), cat(mv_x, mv_c), 512)
        x_new = _merge(x, ya, yb, yc, mx[3], mx[4], mx[5], g[2], g[3],
                       wts["w_gate"], wts["bw"], wts["wo"])
        if not last:
            ca = _flash("diff", dq_c, dk_c, dv_c, 256, (dl, subln), lambda_init)
            cb = _win_attn(wq_c, None, None, wk_c, wv_c, sink, band=False)
            cc = _flash("mla", mq_c, mk_c, mv_c, 512)
            h = _merge(h, ca, cb, cc, mc[3], mc[4], mc[5], g[2], g[3],
                       wts["w_gate"], wts["bw"], wts["wo"])
            h = _ffn(h, mc[6], mc[7], mc[8], g[4], g[5], wts["ffn_in"][1], wts["ffn_out"][1])
        x = _ffn(x_new, mx[6], mx[7], mx[8], g[4], g[5], wts["ffn_in"][1], wts["ffn_out"][1])
    return x
```

```python
import functools
import math

import jax
import jax.numpy as jnp
from jax import lax
from jax.experimental import pallas as pl
from jax.experimental.pallas import tpu as pltpu

F32 = jnp.float32
BF16 = jnp.bfloat16

GRID_W = 64
QBLK = 128
EPS = 1e-6
NEG_INF = -1e30
ROPE_DIM = 64
ROPE_BASE = 10000.0
N_MOD = 9
FFN_RES = 0.5
DIFF_HEADS = 8
DIFF_D = 64
WIN_Q_HEADS = 16
WIN_KV_HEADS = 4
WIN_GROUP = 4
WIN_HEAD_DIM = 64
WIN_SCALE = WIN_HEAD_DIM ** -0.5
MLA_HEADS = 8
MLA_Q_RANK = 384
MLA_KV_RANK = 256
MLA_NOPE = 128
MLA_ROPE = 64
MLA_V = 128
MLA_SCALE = (MLA_NOPE + MLA_ROPE) ** -0.5
LOG2E = math.log2(math.e)

LANES = 128
VMEM_LIMIT = 56 * 1024 * 1024


def _cparams(sem):
    return pltpu.CompilerParams(dimension_semantics=sem, vmem_limit_bytes=VMEM_LIMIT)


def _resident(shape):
    nd = len(shape)
    return pl.BlockSpec(shape, lambda *_: (0,) * nd, pipeline_mode=pl.Buffered(1))


def _rms(x, g):
    return x * lax.rsqrt(jnp.mean(x * x, axis=-1, keepdims=True) + EPS) * g


def _norm_mod(x, g, shift, scale):
    return _rms(x, g) * (1.0 + scale) + shift


def _dot(a, b):
    return jnp.dot(a, b, preferred_element_type=F32)


def _dot_nt(a, b):
    return lax.dot_general(a, b, (((1,), (1,)), ((), ())), preferred_element_type=F32)


def _mod_spec(arr):
    d = arr.shape[-1]
    if arr.shape[0] == 1:
        return pl.BlockSpec((1, 1, d), lambda b, *_: (0, 0, 0))
    return pl.BlockSpec((1, 1, d), lambda b, *_: (b, 0, 0))


def _ada_kernel(c_ref, w_ref, b_ref, o_ref):
    c = c_ref[...]
    a = c * jax.nn.sigmoid(c)
    o_ref[0] = jnp.dot(a, w_ref[0], preferred_element_type=F32,
                       precision=lax.Precision.HIGHEST) + b_ref[0]


def _ada_mods(cvecs, ada_w, ada_b):
    depth, d, nd = ada_w.shape
    rows = cvecs.shape[0]
    tn = 1152 if nd % 1152 == 0 else nd
    return pl.pallas_call(
        _ada_kernel,
        grid=(depth, nd // tn),
        in_specs=[pl.BlockSpec((rows, d), lambda l, j: (0, 0)),
                  pl.BlockSpec((1, d, tn), lambda l, j: (l, 0, j)),
                  pl.BlockSpec((1, 1, tn), lambda l, j: (l, 0, j))],
        out_specs=pl.BlockSpec((1, rows, tn), lambda l, j: (l, 0, j)),
        out_shape=jax.ShapeDtypeStruct((depth, rows, nd), F32),
        compiler_params=_cparams(("parallel", "parallel")),
        name="ada_mods",
    )(cvecs, ada_w, ada_b.reshape(depth, 1, nd))


def _ffn_kernel(x_ref, sh_ref, sc_ref, gt_ref, gpre_ref, gpost_ref, win_ref, wout_ref, o_ref,
                *, ffn_dim, chunk):
    x = x_ref[0]
    xm = _norm_mod(x, gpre_ref[...], sh_ref[0], sc_ref[0]).astype(BF16)
    acc = jnp.zeros(x.shape, F32)
    for c in range(ffn_dim // chunk):
        a = _dot(xm, win_ref[:, c * chunk:(c + 1) * chunk])
        b = _dot(xm, win_ref[:, ffn_dim + c * chunk:ffn_dim + (c + 1) * chunk])
        h = (a * jax.nn.sigmoid(a) * b).astype(BF16)
        acc = acc + _dot(h, wout_ref[c * chunk:(c + 1) * chunk, :])
    o_ref[0] = x + FFN_RES * gt_ref[0] * _rms(acc, gpost_ref[...])


def _ffn(x, shift, scale, gate, g_pre, g_post, w_in, w_out):
    bsz, n, d = x.shape
    ffn_dim = w_out.shape[0]
    tm = min(512, n)
    tok = pl.BlockSpec((1, tm, d), lambda b, i: (b, i, 0))
    gain = pl.BlockSpec((1, d), lambda b, i: (0, 0))
    return pl.pallas_call(
        functools.partial(_ffn_kernel, ffn_dim=ffn_dim, chunk=256),
        grid=(bsz, n // tm),
        in_specs=[tok, _mod_spec(shift), _mod_spec(scale), _mod_spec(gate), gain, gain,
                  _resident(w_in.shape), _resident(w_out.shape)],
        out_specs=tok,
        out_shape=jax.ShapeDtypeStruct(x.shape, F32),
        compiler_params=_cparams(("parallel", "parallel")),
        name="ffn",
    )(x, shift, scale, gate, g_pre, g_post, w_in, w_out)


def _rope_masks(tm):
    lane = lax.broadcasted_iota(jnp.int32, (tm, LANES), 1)
    even = ((lane // (ROPE_DIM // 4)) & 1) == 0
    lo = lane < (LANES // 2)
    return even, lo


def _rope(blk, cos, sin, even):
    q = ROPE_DIM // 4
    partner = jnp.where(even, pltpu.roll(blk, LANES - q, 1), pltpu.roll(blk, q, 1))
    return blk * cos + partner * sin


def _proj_diff_kernel(x_ref, sh_ref, sc_ref, g_ref, cos_ref, sin_ref, w_ref,
                      q_ref, k_ref, v_ref):
    hm = _norm_mod(x_ref[0], g_ref[...], sh_ref[0], sc_ref[0]).astype(BF16)
    tm = hm.shape[0]
    cos = cos_ref[...]
    sin = sin_ref[...]
    even, lo = _rope_masks(tm)
    hw = DIFF_HEADS * 2 * DIFF_D
    q = _dot(hm, w_ref[:, 0:hw])
    k = _dot(hm, w_ref[:, hw:2 * hw])
    scale = LOG2E / math.sqrt(DIFF_D)
    for h in range(DIFF_HEADS):
        qb = _rope(q[:, h * LANES:(h + 1) * LANES], cos, sin, even) * scale
        q_ref[0, :, (2 * h) * LANES:(2 * h + 1) * LANES] = jnp.where(lo, qb, 0.0).astype(BF16)
        q_ref[0, :, (2 * h + 1) * LANES:(2 * h + 2) * LANES] = jnp.where(lo, 0.0, qb).astype(BF16)
        kb = _rope(k[:, h * LANES:(h + 1) * LANES], cos, sin, even)
        k_ref[0, :, h * LANES:(h + 1) * LANES] = kb.astype(BF16)
    v_ref[0] = _dot(hm, w_ref[:, 2 * hw:3 * hw]).astype(BF16)


def _proj_diff(x, shift, scale, g, cos, sin, w):
    bsz, n, d = x.shape
    tm = min(512, n)
    hw = DIFF_HEADS * 2 * DIFF_D
    tok = lambda c: pl.BlockSpec((1, tm, c), lambda b, i: (b, i, 0))
    tab = pl.BlockSpec((tm, LANES), lambda b, i: (i, 0))
    return pl.pallas_call(
        _proj_diff_kernel,
        grid=(bsz, n // tm),
        in_specs=[tok(d), _mod_spec(shift), _mod_spec(scale),
                  pl.BlockSpec((1, d), lambda b, i: (0, 0)), tab, tab, _resident(w.shape)],
        out_specs=[tok(2 * hw), tok(hw), tok(hw)],
        out_shape=[jax.ShapeDtypeStruct((bsz, n, 2 * hw), BF16),
                   jax.ShapeDtypeStruct((bsz, n, hw), BF16),
                   jax.ShapeDtypeStruct((bsz, n, hw), BF16)],
        compiler_params=_cparams(("parallel", "parallel")),
        name="proj_diff",
    )(x, shift, scale, g, cos, sin, w)


_WQ0, _WK0, _WV0 = 0, 1024, 1280
_CQ0, _CKV0, _KR0, _PB_COLS = 1536, 1920, 2176, 2304


def _proj_wm_kernel(x_ref, sh_ref, sc_ref, g_ref, cos_ref, sin_ref, w_ref, wuq_ref, wukv_ref,
                    qn_ref, kvn_ref, wq_ref, wk_ref, wv_ref, mq_ref, mk_ref, mv_ref):
    hm = _norm_mod(x_ref[0], g_ref[...], sh_ref[0], sc_ref[0]).astype(BF16)
    tm = hm.shape[0]
    cos = cos_ref[...]
    sin = sin_ref[...]
    even, lo = _rope_masks(tm)
    half = LANES // 2
    p = _dot(hm, w_ref[...])

    def split_heads(blk, out_ref, j):
        out_ref[0, :, (2 * j) * LANES:(2 * j + 1) * LANES] = jnp.where(lo, blk, 0.0).astype(BF16)
        out_ref[0, :, (2 * j + 1) * LANES:(2 * j + 2) * LANES] = (
            jnp.where(lo, pltpu.roll(blk, half, 1), 0.0).astype(BF16))

    for j in range(WIN_Q_HEADS // 2):
        blk = _rope(p[:, _WQ0 + j * LANES:_WQ0 + (j + 1) * LANES], cos, sin, even) * WIN_SCALE
        split_heads(blk, wq_ref, j)
    for j in range(WIN_KV_HEADS // 2):
        split_heads(_rope(p[:, _WK0 + j * LANES:_WK0 + (j + 1) * LANES], cos, sin, even), wk_ref, j)
        split_heads(p[:, _WV0 + j * LANES:_WV0 + (j + 1) * LANES], wv_ref, j)

    cq = _rms(p[:, _CQ0:_CQ0 + MLA_Q_RANK], qn_ref[...]).astype(BF16)
    q2 = _dot(cq, wuq_ref[...])
    ckv = _rms(p[:, _CKV0:_CKV0 + MLA_KV_RANK], kvn_ref[...]).astype(BF16)
    kv = _dot(ckv, wukv_ref[...])
    kr = _rope(p[:, _KR0:_KR0 + LANES], cos, sin, even).astype(BF16)
    for h in range(MLA_HEADS):
        c0 = 2 * h * LANES
        mq_ref[0, :, c0:c0 + LANES] = (q2[:, c0:c0 + LANES] * (MLA_SCALE * LOG2E)).astype(BF16)
        qr = _rope(q2[:, c0 + LANES:c0 + 2 * LANES], cos, sin, even) * (MLA_SCALE * LOG2E)
        mq_ref[0, :, c0 + LANES:c0 + 2 * LANES] = qr.astype(BF16)
        mk_ref[0, :, c0:c0 + LANES] = kv[:, c0:c0 + LANES].astype(BF16)
        mk_ref[0, :, c0 + LANES:c0 + 2 * LANES] = kr
        mv_ref[0, :, h * LANES:(h + 1) * LANES] = kv[:, c0 + LANES:c0 + 2 * LANES].astype(BF16)


def _proj_wm(x, shift, scale, g, cos, sin, w, wuq, wukv, qn, kvn):
    bsz, n, d = x.shape
    tm = min(512, n)
    tok = lambda c: pl.BlockSpec((1, tm, c), lambda b, i: (b, i, 0))
    tab = pl.BlockSpec((tm, LANES), lambda b, i: (i, 0))
    row = lambda c: pl.BlockSpec((1, c), lambda b, i: (0, 0))
    widths = (WIN_Q_HEADS * LANES, WIN_KV_HEADS * LANES, WIN_KV_HEADS * LANES,
              MLA_HEADS * 2 * LANES, MLA_HEADS * 2 * LANES, MLA_HEADS * LANES)
    return pl.pallas_call(
        _proj_wm_kernel,
        grid=(bsz, n // tm),
        in_specs=[tok(d), _mod_spec(shift), _mod_spec(scale), row(d), tab, tab,
                  _resident(w.shape), _resident(wuq.shape), _resident(wukv.shape),
                  row(MLA_Q_RANK), row(MLA_KV_RANK)],
        out_specs=[tok(c) for c in widths],
        out_shape=[jax.ShapeDtypeStruct((bsz, n, c), BF16) for c in widths],
        compiler_params=_cparams(("parallel", "parallel")),
        name="proj_wm",
    )(x, shift, scale, g, cos, sin, w, wuq, wukv, qn, kvn)


def _flash_loop(qt, k_ref, vt_ref, scr, nc, tk):
    s_sc, c_sc, m_sc, l_sc, acc_sc = scr[0:2], scr[2:4], scr[4], scr[5], scr[6]
    m_sc[...] = jnp.full(m_sc.shape, NEG_INF, F32)
    l_sc[...] = jnp.zeros(l_sc.shape, F32)
    acc_sc[...] = jnp.zeros(acc_sc.shape, F32)

    def scores_at(j, par):
        start = pl.multiple_of(j * tk, tk)
        s = _dot(k_ref[0, 0, pl.ds(start, tk), :], qt)
        s_sc[par][...] = s
        c_sc[par][...] = jnp.max(s, axis=0, keepdims=True)

    def update(j, par):
        m_prev = m_sc[...]
        m_new = jnp.maximum(m_prev, c_sc[par][...])
        alpha = jnp.exp2(m_prev - m_new)
        p = jnp.exp2(s_sc[par][...] - m_new)
        l_sc[...] = alpha * l_sc[...] + jnp.sum(p, axis=0, keepdims=True)
        acc_sc[...] = alpha * acc_sc[...] + _dot(vt_ref[0, 0, j], p.astype(BF16))
        m_sc[...] = m_new

    scores_at(0, 0)

    def pair(i, carry):
        j = 2 * i
        scores_at(j + 1, 1)
        update(j, 0)
        scores_at(j + 2, 0)
        update(j + 1, 1)
        return carry

    npairs = (nc - 1) // 2
    lax.fori_loop(0, npairs, pair, 0)
    if nc - 1 - 2 * npairs == 1:
        scores_at(nc - 1, (nc - 1) % 2)
        update(nc - 2, (nc - 2) % 2)
        update(nc - 1, (nc - 1) % 2)
    else:
        update(nc - 1, (nc - 1) % 2)


def _diff_flash_kernel(qt_ref, k_ref, vt_ref, dl_ref, g_ref, o_ref, *scr, nc, tk, lambda_init):
    _flash_loop(qt_ref[0, 0, 0], k_ref, vt_ref, scr, nc, tk)
    l_sc, acc_sc = scr[5], scr[6]
    tq = o_ref.shape[1]
    o = acc_sc[...] / l_sc[...]
    dl = dl_ref[...]
    lam = (jnp.exp(jnp.sum(dl[0:1] * dl[1:2], axis=1, keepdims=True))
           - jnp.exp(jnp.sum(dl[2:3] * dl[3:4], axis=1, keepdims=True)) + lambda_init)
    y = (o[:, :tq] - lam * o[:, tq:]).T
    o_ref[0] = (_rms(y, g_ref[...]) * (1.0 - lambda_init)).astype(BF16)


def _mla_flash_kernel(qt_ref, k_ref, vt_ref, o_ref, *scr, nc, tk):
    _flash_loop(qt_ref[0, 0, 0], k_ref, vt_ref, scr, nc, tk)
    l_sc, acc_sc = scr[5], scr[6]
    o_ref[0] = (acc_sc[...] / l_sc[...]).T.astype(BF16)


def _pick_chunk(nk):
    for cand in (1280, 1024, 768, 512, 256, 128):
        if nk % cand == 0:
            return cand
    raise ValueError(f"unsupported key count {nk}")


def _flash(mode, q, k, v, tq, extra=(), lambda_init=0.0):
    bsz, s, _ = q.shape
    nk = k.shape[1]
    heads = v.shape[2] // LANES
    dk = k.shape[2] // heads
    maps = q.shape[2] // (heads * dk)
    tq = min(tq, s)
    nq = s // tq
    w = maps * tq
    tk = _pick_chunk(nk)
    nc = nk // tk
    qt = q.reshape(bsz, nq, tq, heads, maps, dk).transpose(0, 3, 1, 5, 4, 2).reshape(bsz, heads, nq, dk, w)
    kh = k.reshape(bsz, nk, heads, dk).transpose(0, 2, 1, 3)
    vt = v.reshape(bsz, nc, tk, heads, LANES).transpose(0, 3, 1, 4, 2)
    in_specs = [pl.BlockSpec((1, 1, 1, dk, w), lambda b, h, i: (b, h, i, 0, 0)),
                pl.BlockSpec((1, 1, nk, dk), lambda b, h, i: (b, h, 0, 0)),
                pl.BlockSpec((1, 1, nc, LANES, tk), lambda b, h, i: (b, h, 0, 0, 0))]
    if mode == "diff":
        kern = functools.partial(_diff_flash_kernel, nc=nc, tk=tk, lambda_init=lambda_init)
        in_specs += [pl.BlockSpec(extra[0].shape, lambda b, h, i: (0, 0)),
                     pl.BlockSpec(extra[1].shape, lambda b, h, i: (0, 0))]
    else:
        kern = functools.partial(_mla_flash_kernel, nc=nc, tk=tk)
    return pl.pallas_call(
        kern,
        grid=(bsz, heads, nq),
        in_specs=in_specs,
        out_specs=pl.BlockSpec((1, tq, LANES), lambda b, h, i: (b, i, h)),
        out_shape=jax.ShapeDtypeStruct((bsz, s, heads * LANES), BF16),
        scratch_shapes=[pltpu.VMEM((tk, w), F32), pltpu.VMEM((tk, w), F32),
                        pltpu.VMEM((1, w), F32), pltpu.VMEM((1, w), F32),
                        pltpu.VMEM((1, w), F32), pltpu.VMEM((1, w), F32),
                        pltpu.VMEM((LANES, w), F32)],
        compiler_params=_cparams(("parallel", "parallel", "arbitrary")),
        name=f"{mode}_flash",
    )(qt, kh, vt, *extra)


def _win_kernel(*refs, band, nblk):
    if band:
        (q_ref, kp_ref, ko_ref, kn_ref, vp_ref, vo_ref, vn_ref, kc_ref, vc_ref,
         sink_ref, o_ref) = refs
    else:
        q_ref, kc_ref, vc_ref, sink_ref, o_ref = refs
    t = q_ref.shape[1]
    nctx = kc_ref.shape[1]
    rows = WIN_GROUP * t
    if band:
        i = pl.program_id(1)
        ncols = 3 * t + nctx
        qq = lax.broadcasted_iota(jnp.int32, (rows, ncols), 0) % t
        cc = lax.broadcasted_iota(jnp.int32, (rows, ncols), 1)
        off_prev = jnp.where(i > 0, 0, t)
        off_next = jnp.where(i < nblk - 1, 0, t)
        is_prev = cc < t
        is_next = jnp.logical_and(cc >= 2 * t, cc < 3 * t)
        bad_prev = jnp.logical_and(is_prev, cc < qq + off_prev)
        bad_next = jnp.logical_and(is_next, cc - 2 * t > qq - off_next)
        mask = jnp.logical_not(jnp.logical_or(bad_prev, bad_next))
    grp = lax.broadcasted_iota(jnp.int32, (rows, 1), 0) // t
    for kvh in range(WIN_KV_HEADS):
        ksl = slice(kvh * LANES, (kvh + 1) * LANES)
        if band:
            kcat = jnp.concatenate([kp_ref[0, :, ksl], ko_ref[0, :, ksl], kn_ref[0, :, ksl],
                                    kc_ref[0, :, ksl]], axis=0)
            vcat = jnp.concatenate([vp_ref[0, :, ksl], vo_ref[0, :, ksl], vn_ref[0, :, ksl],
                                    vc_ref[0, :, ksl]], axis=0)
        else:
            kcat = kc_ref[0, :, ksl]
            vcat = vc_ref[0, :, ksl]
        h0 = kvh * WIN_GROUP
        qs = jnp.concatenate([q_ref[0, :, (h0 + g) * LANES:(h0 + g + 1) * LANES]
                              for g in range(WIN_GROUP)], axis=0)
        sink = jnp.zeros((rows, 1), F32)
        for g in range(WIN_GROUP):
            sink = jnp.where(grp == g, sink_ref[h0 + g:h0 + g + 1, 0:1], sink)
        s = _dot_nt(qs, kcat)
        if band:
            s = jnp.where(mask, s, NEG_INF)
        m = jnp.maximum(jnp.max(s, axis=1, keepdims=True), sink)
        p = jnp.exp(s - m)
        l = jnp.sum(p, axis=1, keepdims=True) + jnp.exp(sink - m)
        o = _dot(p.astype(BF16), vcat) / l
        for pair in range(WIN_GROUP // 2):
            a = o[(2 * pair) * t:(2 * pair + 1) * t]
            b = o[(2 * pair + 1) * t:(2 * pair + 2) * t]
            blk = kvh * (WIN_GROUP // 2) + pair
            o_ref[0, :, blk * LANES:(blk + 1) * LANES] = (a + pltpu.roll(b, LANES // 2, 1)).astype(BF16)


def _win_attn(q, k, v, kc, vc, sink, band):
    bsz, s, qw = q.shape
    kw = kc.shape[2]
    nctx = kc.shape[1]
    out_w = WIN_Q_HEADS * WIN_HEAD_DIM
    if band:
        t = QBLK
        nblk = s // t
        kspec = lambda f: pl.BlockSpec((1, t, kw), f)
        prev = lambda b, i: (b, jnp.maximum(i - 1, 0), 0)
        own = lambda b, i: (b, i, 0)
        nxt = lambda b, i: (b, jnp.minimum(i + 1, nblk - 1), 0)
        ctx = pl.BlockSpec((1, nctx, kw), lambda b, i: (b, 0, 0))
        in_specs = [pl.BlockSpec((1, t, qw), own), kspec(prev), kspec(own), kspec(nxt),
                    kspec(prev), kspec(own), kspec(nxt), ctx, ctx]
        args = (q, k, k, k, v, v, v, kc, vc)
    else:
        t = s
        nblk = 1
        ctx = pl.BlockSpec((1, nctx, kw), lambda b, i: (b, 0, 0))
        in_specs = [pl.BlockSpec((1, t, qw), lambda b, i: (b, i, 0)), ctx, ctx]
        args = (q, kc, vc)
    in_specs.append(pl.BlockSpec(sink.shape, lambda b, i: (0, 0)))
    return pl.pallas_call(
        functools.partial(_win_kernel, band=band, nblk=nblk),
        grid=(bsz, nblk),
        in_specs=in_specs,
        out_specs=pl.BlockSpec((1, t, out_w), lambda b, i: (b, i, 0)),
        out_shape=jax.ShapeDtypeStruct((bsz, s, out_w), BF16),
        compiler_params=_cparams(("parallel", "parallel")),
        name="win_attn" if band else "win_attn_ctx",
    )(*args, sink)


def _merge_kernel(x_ref, ya_ref, yb_ref, yc_ref, sh_ref, sc_ref, gt_ref, g2_ref, g3_ref,
                  wg_ref, bw_ref, wo_ref, o_ref):
    x = x_ref[0]
    d = x.shape[1]
    hm = _norm_mod(x, g2_ref[...], sh_ref[0], sc_ref[0]).astype(BF16)
    merged = jnp.zeros(x.shape, F32)
    for i, y_ref in enumerate((ya_ref, yb_ref, yc_ref)):
        gate = jax.nn.sigmoid(_dot(hm, wg_ref[:, i * d:(i + 1) * d]))
        merged = merged + gate * _dot(y_ref[0], bw_ref[i])
    y = _dot(merged.astype(BF16), wo_ref[...])
    o_ref[0] = x + gt_ref[0] * _rms(y, g3_ref[...])


def _merge(x, ya, yb, yc, shift, scale, gate, g2, g3, wg, bw, wo):
    bsz, n, d = x.shape
    tm = min(512, n)
    tok = pl.BlockSpec((1, tm, d), lambda b, i: (b, i, 0))
    gain = pl.BlockSpec((1, d), lambda b, i: (0, 0))
    return pl.pallas_call(
        _merge_kernel,
        grid=(bsz, n // tm),
        in_specs=[tok, tok, tok, tok, _mod_spec(shift), _mod_spec(scale), _mod_spec(gate),
                  gain, gain, _resident(wg.shape), _resident(bw.shape), _resident(wo.shape)],
        out_specs=tok,
        out_shape=jax.ShapeDtypeStruct(x.shape, F32),
        compiler_params=_cparams(("parallel", "parallel")),
        name="merge",
    )(x, ya, yb, yc, shift, scale, gate, g2, g3, wg, bw, wo)


def _rope_tables(n):
    quarter = ROPE_DIM // 4
    pos = jnp.arange(n)
    row = (pos // GRID_W).astype(F32)
    col = (pos % GRID_W).astype(F32)
    inv_freq = 1.0 / (ROPE_BASE ** (jnp.arange(quarter, dtype=F32) / quarter))
    ang_r = row[:, None] * inv_freq
    ang_c = col[:, None] * inv_freq
    cos = jnp.concatenate([jnp.cos(ang_r)] * 2 + [jnp.cos(ang_c)] * 2, axis=1)
    sin = jnp.concatenate([-jnp.sin(ang_r), jnp.sin(ang_r), -jnp.sin(ang_c), jnp.sin(ang_c)], axis=1)
    return jnp.tile(cos, (1, 2)), jnp.tile(sin, (1, 2))


def _layer_weights(l, ffn_w_in, ffn_w_out, mix_w_in, mla_w_uq, mla_w_ukv, branch_w, mix_w_out):
    d = mix_w_in.shape[1]
    w = mix_w_in[l]
    hw = DIFF_HEADS * 2 * DIFF_D
    wq_n = WIN_Q_HEADS * WIN_HEAD_DIM
    wk_n = WIN_KV_HEADS * WIN_HEAD_DIM
    c = 3 * hw
    w_diff = w[:, :c]
    rest_n = wq_n + 2 * wk_n + MLA_Q_RANK + MLA_KV_RANK + MLA_ROPE
    w_wm = jnp.concatenate([w[:, c:c + rest_n], jnp.zeros((d, _PB_COLS - rest_n), w.dtype)], axis=1)
    w_gate = w[:, c + rest_n:]
    wuq = mla_w_uq[l].reshape(MLA_Q_RANK, MLA_HEADS, MLA_NOPE + MLA_ROPE)
    wuq = jnp.pad(wuq, ((0, 0), (0, 0), (0, 2 * LANES - MLA_NOPE - MLA_ROPE)))
    wuq = wuq.reshape(MLA_Q_RANK, MLA_HEADS * 2 * LANES)
    return dict(
        ffn_in=[ffn_w_in[l, i].astype(BF16) for i in range(2)],
        ffn_out=[ffn_w_out[l, i].astype(BF16) for i in range(2)],
        w_diff=w_diff.astype(BF16), w_wm=w_wm.astype(BF16), w_gate=w_gate.astype(BF16),
        wuq=wuq.astype(BF16), wukv=mla_w_ukv[l].astype(BF16),
        bw=branch_w[l].astype(BF16), wo=mix_w_out[l].astype(BF16))


def kernel(x, c, ctx, c_ctx, ada_w, ada_b, norm_g, ffn_w_in, ffn_w_out, mix_w_in, diff_lambda,
           diff_subln_g, win_sink, mla_q_norm_g, mla_kv_norm_g, mla_w_uq, mla_w_ukv, branch_w,
           mix_w_out):
    bsz, s, d = x.shape
    nctx = ctx.shape[1]
    depth = ada_w.shape[0]

    cvecs = jnp.zeros((8, d), F32).at[:bsz].set(c).at[bsz].set(c_ctx)
    mods = _ada_mods(cvecs, ada_w, ada_b).reshape(depth, 8, N_MOD, d)

    cos_x, sin_x = _rope_tables(s)
    cos_c = jnp.ones((nctx, LANES), F32)
    sin_c = jnp.zeros((nctx, LANES), F32)

    h = ctx
    for l in range(depth):
        last = l == depth - 1
        lambda_init = 0.8 - 0.6 * math.exp(-0.3 * l)
        wts = _layer_weights(l, ffn_w_in, ffn_w_out, mix_w_in, mla_w_uq, mla_w_ukv, branch_w,
                             mix_w_out)
        mx = [mods[l, :bsz, k][:, None, :] for k in range(N_MOD)]
        mc = [mods[l, bsz:bsz + 1, k][:, None, :] for k in range(N_MOD)]
        g = [norm_g[l, k][None, :] for k in range(6)]
        qn = mla_q_norm_g[l][None, :]
        kvn = mla_kv_norm_g[l][None, :]
        dl = diff_lambda[l].astype(F32)
        subln = diff_subln_g[l][None, :]
        sink = jnp.broadcast_to(win_sink[l].astype(F32)[:, None], (WIN_Q_HEADS, LANES))

        x = _ffn(x, mx[0], mx[1], mx[2], g[0], g[1], wts["ffn_in"][0], wts["ffn_out"][0])
        h = _ffn(h, mc[0], mc[1], mc[2], g[0], g[1], wts["ffn_in"][0], wts["ffn_out"][0])

        dq_x, dk_x, dv_x = _proj_diff(x, mx[3], mx[4], g[2], cos_x, sin_x, wts["w_diff"])
        dq_c, dk_c, dv_c = _proj_diff(h, mc[3], mc[4], g[2], cos_c, sin_c, wts["w_diff"])
        wq_x, wk_x, wv_x, mq_x, mk_x, mv_x = _proj_wm(
            x, mx[3], mx[4], g[2], cos_x, sin_x, wts["w_wm"], wts["wuq"], wts["wukv"], qn, kvn)
        wq_c, wk_c, wv_c, mq_c, mk_c, mv_c = _proj_wm(
            h, mc[3], mc[4], g[2], cos_c, sin_c, wts["w_wm"], wts["wuq"], wts["wukv"], qn, kvn)

        cat = lambda u, w: jnp.concatenate([u, w], axis=1)
        ya = _flash("diff", dq_x, cat(dk_x, dk_c), cat(dv_x, dv_c), 256, (dl, subln), lambda_init)
        yb = _win_attn(wq_x, wk_x, wv_x, wk_c, wv_c, sink, band=True)
        yc = _flash("mla", mq_x, cat(mk_x, mk_c), cat(mv_x, mv_c), 512)
        x_new = _merge(x, ya, yb, yc, mx[3], mx[4], mx[5], g[2], g[3],
                       wts["w_gate"], wts["bw"], wts["wo"])
        if not last:
            ca = _flash("diff", dq_c, dk_c, dv_c, 256, (dl, subln), lambda_init)
            cb = _win_attn(wq_c, None, None, wk_c, wv_c, sink, band=False)
            cc = _flash("mla", mq_c, mk_c, mv_c, 512)
            h = _merge(h, ca, cb, cc, mc[3], mc[4], mc[5], g[2], g[3],
                       wts["w_gate"], wts["bw"], wts["wo"])
            h = _ffn(h, mc[6], mc[7], mc[8], g[4], g[5], wts["ffn_in"][1], wts["ffn_out"][1])
        x = _ffn(x_new, mx[6], mx[7], mx[8], g[4], g[5], wts["ffn_in"][1], wts["ffn_out"][1])
    return x
```

```python
import functools
import math

import jax
import jax.numpy as jnp
from jax import lax
from jax.experimental import pallas as pl
from jax.experimental.pallas import tpu as pltpu

F32 = jnp.float32
BF16 = jnp.bfloat16

GRID_W = 64
QBLK = 128
EPS = 1e-6
NEG_INF = -1e30
ROPE_DIM = 64
ROPE_BASE = 10000.0
N_MOD = 9
FFN_RES = 0.5
DIFF_HEADS = 8
DIFF_D = 64
WIN_Q_HEADS = 16
WIN_KV_HEADS = 4
WIN_GROUP = 4
WIN_HEAD_DIM = 64
WIN_SCALE = WIN_HEAD_DIM ** -0.5
MLA_HEADS = 8
MLA_Q_RANK = 384
MLA_KV_RANK = 256
MLA_NOPE = 128
MLA_ROPE = 64
MLA_V = 128
MLA_SCALE = (MLA_NOPE + MLA_ROPE) ** -0.5
LOG2E = math.log2(math.e)

LANES = 128
VMEM_LIMIT = 56 * 1024 * 1024


def _cparams(sem):
    return pltpu.CompilerParams(dimension_semantics=sem, vmem_limit_bytes=VMEM_LIMIT)


def _resident(shape):
    nd = len(shape)
    return pl.BlockSpec(shape, lambda *_: (0,) * nd, pipeline_mode=pl.Buffered(1))


def _rms(x, g):
    return x * lax.rsqrt(jnp.mean(x * x, axis=-1, keepdims=True) + EPS) * g


def _norm_mod(x, g, shift, scale):
    return _rms(x, g) * (1.0 + scale) + shift


def _dot(a, b):
    return jnp.dot(a, b, preferred_element_type=F32)


def _dot_nt(a, b):
    return lax.dot_general(a, b, (((1,), (1,)), ((), ())), preferred_element_type=F32)


def _mod_spec(arr):
    d = arr.shape[-1]
    if arr.shape[0] == 1:
        return pl.BlockSpec((1, 1, d), lambda b, *_: (0, 0, 0))
    return pl.BlockSpec((1, 1, d), lambda b, *_: (b, 0, 0))


def _ada_kernel(c_ref, w_ref, b_ref, o_ref):
    c = c_ref[...]
    a = c * jax.nn.sigmoid(c)
    o_ref[0] = jnp.dot(a, w_ref[0], preferred_element_type=F32,
                       precision=lax.Precision.HIGHEST) + b_ref[0]


def _ada_mods(cvecs, ada_w, ada_b):
    depth, d, nd = ada_w.shape
    rows = cvecs.shape[0]
    tn = 1152 if nd % 1152 == 0 else nd
    return pl.pallas_call(
        _ada_kernel,
        grid=(depth, nd // tn),
        in_specs=[pl.BlockSpec((rows, d), lambda l, j: (0, 0)),
                  pl.BlockSpec((1, d, tn), lambda l, j: (l, 0, j)),
                  pl.BlockSpec((1, 1, tn), lambda l, j: (l, 0, j))],
        out_specs=pl.BlockSpec((1, rows, tn), lambda l, j: (l, 0, j)),
        out_shape=jax.ShapeDtypeStruct((depth, rows, nd), F32),
        compiler_params=_cparams(("parallel", "parallel")),
        name="ada_mods",
    )(cvecs, ada_w, ada_b.reshape(depth, 1, nd))


def _ffn_kernel(x_ref, sh_ref, sc_ref, gt_ref, gpre_ref, gpost_ref, win_ref, wout_ref, o_ref,
                *, ffn_dim, chunk):
    x = x_ref[0]
    xm = _norm_mod(x, gpre_ref[...], sh_ref[0], sc_ref[0]).astype(BF16)
    acc = jnp.zeros(x.shape, F32)
    for c in range(ffn_dim // chunk):
        a = _dot(xm, win_ref[:, c * chunk:(c + 1) * chunk])
        b = _dot(xm, win_ref[:, ffn_dim + c * chunk:ffn_dim + (c + 1) * chunk])
        h = (a * jax.nn.sigmoid(a) * b).astype(BF16)
        acc = acc + _dot(h, wout_ref[c * chunk:(c + 1) * chunk, :])
    o_ref[0] = x + FFN_RES * gt_ref[0] * _rms(acc, gpost_ref[...])


def _ffn(x, shift, scale, gate, g_pre, g_post, w_in, w_out):
    bsz, n, d = x.shape
    ffn_dim = w_out.shape[0]
    tm = min(512, n)
    tok = pl.BlockSpec((1, tm, d), lambda b, i: (b, i, 0))
    gain = pl.BlockSpec((1, d), lambda b, i: (0, 0))
    return pl.pallas_call(
        functools.partial(_ffn_kernel, ffn_dim=ffn_dim, chunk=256),
        grid=(bsz, n // tm),
        in_specs=[tok, _mod_spec(shift), _mod_spec(scale), _mod_spec(gate), gain, gain,
                  _resident(w_in.shape), _resident(w_out.shape)],
        out_specs=tok,
        out_shape=jax.ShapeDtypeStruct(x.shape, F32),
        compiler_params=_cparams(("parallel", "parallel")),
        name="ffn",
    )(x, shift, scale, gate, g_pre, g_post, w_in, w_out)


def _rope_masks(tm):
    lane = lax.broadcasted_iota(jnp.int32, (tm, LANES), 1)
    even = ((lane // (ROPE_DIM // 4)) & 1) == 0
    lo = lane < (LANES // 2)
    return even, lo


def _rope(blk, cos, sin, even):
    q = ROPE_DIM // 4
    partner = jnp.where(even, pltpu.roll(blk, LANES - q, 1), pltpu.roll(blk, q, 1))
    return blk * cos + partner * sin


def _proj_diff_kernel(x_ref, sh_ref, sc_ref, g_ref, cos_ref, sin_ref, w_ref,
                      q_ref, k_ref, v_ref):
    hm = _norm_mod(x_ref[0], g_ref[...], sh_ref[0], sc_ref[0]).astype(BF16)
    tm = hm.shape[0]
    cos = cos_ref[...]
    sin = sin_ref[...]
    even, lo = _rope_masks(tm)
    hw = DIFF_HEADS * 2 * DIFF_D
    q = _dot(hm, w_ref[:, 0:hw])
    k = _dot(hm, w_ref[:, hw:2 * hw])
    scale = LOG2E / math.sqrt(DIFF_D)
    for h in range(DIFF_HEADS):
        qb = _rope(q[:, h * LANES:(h + 1) * LANES], cos, sin, even) * scale
        q_ref[0, :, (2 * h) * LANES:(2 * h + 1) * LANES] = jnp.where(lo, qb, 0.0).astype(BF16)
        q_ref[0, :, (2 * h + 1) * LANES:(2 * h + 2) * LANES] = jnp.where(lo, 0.0, qb).astype(BF16)
        kb = _rope(k[:, h * LANES:(h + 1) * LANES], cos, sin, even)
        k_ref[0, :, h * LANES:(h + 1) * LANES] = kb.astype(BF16)
    v_ref[0] = _dot(hm, w_ref[:, 2 * hw:3 * hw]).astype(BF16)


def _proj_diff(x, shift, scale, g, cos, sin, w):
    bsz, n, d = x.shape
    tm = min(512, n)
    hw = DIFF_HEADS * 2 * DIFF_D
    tok = lambda c: pl.BlockSpec((1, tm, c), lambda b, i: (b, i, 0))
    tab = pl.BlockSpec((tm, LANES), lambda b, i: (i, 0))
    return pl.pallas_call(
        _proj_diff_kernel,
        grid=(bsz, n // tm),
        in_specs=[tok(d), _mod_spec(shift), _mod_spec(scale),
                  pl.BlockSpec((1, d), lambda b, i: (0, 0)), tab, tab, _resident(w.shape)],
        out_specs=[tok(2 * hw), tok(hw), tok(hw)],
        out_shape=[jax.ShapeDtypeStruct((bsz, n, 2 * hw), BF16),
                   jax.ShapeDtypeStruct((bsz, n, hw), BF16),
                   jax.ShapeDtypeStruct((bsz, n, hw), BF16)],
        compiler_params=_cparams(("parallel", "parallel")),
        name="proj_diff",
    )(x, shift, scale, g, cos, sin, w)


_WQ0, _WK0, _WV0 = 0, 1024, 1280
_CQ0, _CKV0, _KR0, _PB_COLS = 1536, 1920, 2176, 2304


def _proj_wm_kernel(x_ref, sh_ref, sc_ref, g_ref, cos_ref, sin_ref, w_ref, wuq_ref, wukv_ref,
                    qn_ref, kvn_ref, wq_ref, wk_ref, wv_ref, mq_ref, mk_ref, mv_ref):
    hm = _norm_mod(x_ref[0], g_ref[...], sh_ref[0], sc_ref[0]).astype(BF16)
    tm = hm.shape[0]
    cos = cos_ref[...]
    sin = sin_ref[...]
    even, lo = _rope_masks(tm)
    half = LANES // 2
    p = _dot(hm, w_ref[...])

    def split_heads(blk, out_ref, j):
        out_ref[0, :, (2 * j) * LANES:(2 * j + 1) * LANES] = jnp.where(lo, blk, 0.0).astype(BF16)
        out_ref[0, :, (2 * j + 1) * LANES:(2 * j + 2) * LANES] = (
            jnp.where(lo, pltpu.roll(blk, half, 1), 0.0).astype(BF16))

    for j in range(WIN_Q_HEADS // 2):
        blk = _rope(p[:, _WQ0 + j * LANES:_WQ0 + (j + 1) * LANES], cos, sin, even) * WIN_SCALE
        split_heads(blk, wq_ref, j)
    for j in range(WIN_KV_HEADS // 2):
        split_heads(_rope(p[:, _WK0 + j * LANES:_WK0 + (j + 1) * LANES], cos, sin, even), wk_ref, j)
        split_heads(p[:, _WV0 + j * LANES:_WV0 + (j + 1) * LANES], wv_ref, j)

    cq = _rms(p[:, _CQ0:_CQ0 + MLA_Q_RANK], qn_ref[...]).astype(BF16)
    q2 = _dot(cq, wuq_ref[...])
    ckv = _rms(p[:, _CKV0:_CKV0 + MLA_KV_RANK], kvn_ref[...]).astype(BF16)
    kv = _dot(ckv, wukv_ref[...])
    kr = _rope(p[:, _KR0:_KR0 + LANES], cos, sin, even).astype(BF16)
    for h in range(MLA_HEADS):
        c0 = 2 * h * LANES
        mq_ref[0, :, c0:c0 + LANES] = (q2[:, c0:c0 + LANES] * (MLA_SCALE * LOG2E)).astype(BF16)
        qr = _rope(q2[:, c0 + LANES:c0 + 2 * LANES], cos, sin, even) * (MLA_SCALE * LOG2E)
        mq_ref[0, :, c0 + LANES:c0 + 2 * LANES] = qr.astype(BF16)
        mk_ref[0, :, c0:c0 + LANES] = kv[:, c0:c0 + LANES].astype(BF16)
        mk_ref[0, :, c0 + LANES:c0 + 2 * LANES] = kr
        mv_ref[0, :, h * LANES:(h + 1) * LANES] = kv[:, c0 + LANES:c0 + 2 * LANES].astype(BF16)


def _proj_wm(x, shift, scale, g, cos, sin, w, wuq, wukv, qn, kvn):
    bsz, n, d = x.shape
    tm = min(512, n)
    tok = lambda c: pl.BlockSpec((1, tm, c), lambda b, i: (b, i, 0))
    tab = pl.BlockSpec((tm, LANES), lambda b, i: (i, 0))
    row = lambda c: pl.BlockSpec((1, c), lambda b, i: (0, 0))
    widths = (WIN_Q_HEADS * LANES, WIN_KV_HEADS * LANES, WIN_KV_HEADS * LANES,
              MLA_HEADS * 2 * LANES, MLA_HEADS * 2 * LANES, MLA_HEADS * LANES)
    return pl.pallas_call(
        _proj_wm_kernel,
        grid=(bsz, n // tm),
        in_specs=[tok(d), _mod_spec(shift), _mod_spec(scale), row(d), tab, tab,
                  _resident(w.shape), _resident(wuq.shape), _resident(wukv.shape),
                  row(MLA_Q_RANK), row(MLA_KV_RANK)],
        out_specs=[tok(c) for c in widths],
        out_shape=[jax.ShapeDtypeStruct((bsz, n, c), BF16) for c in widths],
        compiler_params=_cparams(("parallel", "parallel")),
        name="proj_wm",
    )(x, shift, scale, g, cos, sin, w, wuq, wukv, qn, kvn)


def _flash_loop(qt, k_ref, vt_ref, scr, nc, tk):
    s_sc, c_sc, m_sc, acc_sc = scr[0:2], scr[2:4], scr[4], scr[5]
    m_sc[...] = jnp.full(m_sc.shape, NEG_INF, F32)
    acc_sc[...] = jnp.zeros(acc_sc.shape, F32)

    def scores_at(j, par):
        start = pl.multiple_of(j * tk, tk)
        s = _dot(k_ref[0, 0, pl.ds(start, tk), :], qt)
        s_sc[par][...] = s
        c_sc[par][...] = jnp.max(s, axis=0, keepdims=True)

    def update(j, par):
        m_prev = m_sc[...]
        m_new = jnp.maximum(m_prev, c_sc[par][...])
        alpha = jnp.exp2(m_prev - m_new)
        p = jnp.exp2(s_sc[par][...] - m_new)
        acc_sc[...] = alpha * acc_sc[...] + _dot(vt_ref[0, 0, j], p.astype(BF16))
        m_sc[...] = m_new

    scores_at(0, 0)

    def pair(i, carry):
        j = 2 * i
        scores_at(j + 1, 1)
        update(j, 0)
        scores_at(j + 2, 0)
        update(j + 1, 1)
        return carry

    npairs = (nc - 1) // 2
    lax.fori_loop(0, npairs, pair, 0)
    if nc - 1 - 2 * npairs == 1:
        scores_at(nc - 1, (nc - 1) % 2)
        update(nc - 2, (nc - 2) % 2)
        update(nc - 1, (nc - 1) % 2)
    else:
        update(nc - 1, (nc - 1) % 2)


V_ROWS = LANES + 16
SPEC_JUMP_LIMIT = 64.0
SPEC_PROBE_ROWS = 128
SPEC_UNROLL = 13


def _flash_loop_spec(qt, k_ref, vt_ref, scr, nc, tk):
    m_sc, acc_sc, jump_sc = scr[4], scr[5], scr[6]
    probe = _dot(k_ref[0, 0, 0:SPEC_PROBE_ROWS, :], qt)
    m_sc[...] = jnp.max(probe, axis=0, keepdims=True)
    acc_sc[...] = jnp.zeros(acc_sc.shape, F32)
    jump_sc[...] = jnp.zeros(jump_sc.shape, F32)

    def body(j, carry):
        start = pl.multiple_of(j * tk, tk)
        s = _dot(k_ref[0, 0, pl.ds(start, tk), :], qt)
        m_prev = m_sc[...]
        cmax = jnp.max(s, axis=0, keepdims=True)
        p = jnp.exp2(s - m_prev)
        pv = _dot(vt_ref[0, 0, j], p.astype(BF16))
        m_new = jnp.maximum(m_prev, cmax)
        alpha = jnp.exp2(m_prev - m_new)
        acc_sc[...] = (acc_sc[...] + pv) * alpha
        jump_sc[...] = jnp.maximum(jump_sc[...], cmax - m_prev)
        m_sc[...] = m_new
        return carry

    lax.fori_loop(0, nc, body, 0, unroll=min(nc, SPEC_UNROLL))


def _flash_tile(qt, k_ref, vt_ref, scr, nc, tk):
    _flash_loop_spec(qt, k_ref, vt_ref, scr, nc, tk)
    worst = jnp.max(scr[6][...])

    @pl.when(jnp.logical_not(worst <= SPEC_JUMP_LIMIT))
    def _redo():
        _flash_loop(qt, k_ref, vt_ref, scr, nc, tk)


def _diff_flash_kernel(qt_ref, k_ref, vt_ref, dl_ref, g_ref, o_ref, *scr, nc, tk, lambda_init):
    _flash_tile(qt_ref[0, 0, 0], k_ref, vt_ref, scr, nc, tk)
    acc_sc = scr[5]
    tq = o_ref.shape[1]
    o = acc_sc[0:LANES, :] / acc_sc[LANES:LANES + 1, :]
    dl = dl_ref[...]
    lam = (jnp.exp(jnp.sum(dl[0:1] * dl[1:2], axis=1, keepdims=True))
           - jnp.exp(jnp.sum(dl[2:3] * dl[3:4], axis=1, keepdims=True)) + lambda_init)
    y = (o[:, :tq] - lam * o[:, tq:]).T
    o_ref[0] = (_rms(y, g_ref[...]) * (1.0 - lambda_init)).astype(BF16)


def _mla_flash_kernel(qt_ref, k_ref, vt_ref, o_ref, *scr, nc, tk):
    _flash_tile(qt_ref[0, 0, 0], k_ref, vt_ref, scr, nc, tk)
    acc_sc = scr[5]
    o_ref[0] = (acc_sc[0:LANES, :] / acc_sc[LANES:LANES + 1, :]).T.astype(BF16)


def _pick_chunk(nk):
    for cand in (1280, 1024, 768, 512, 256, 128):
        if nk % cand == 0:
            return cand
    raise ValueError(f"unsupported key count {nk}")


def _flash(mode, q, k, v, tq, extra=(), lambda_init=0.0):
    bsz, s, _ = q.shape
    nk = k.shape[1]
    heads = v.shape[2] // LANES
    dk = k.shape[2] // heads
    maps = q.shape[2] // (heads * dk)
    tq = min(tq, s)
    nq = s // tq
    w = maps * tq
    tk = _pick_chunk(nk)
    nc = nk // tk
    qt = q.reshape(bsz, nq, tq, heads, maps, dk).transpose(0, 3, 1, 5, 4, 2).reshape(bsz, heads, nq, dk, w)
    kh = k.reshape(bsz, nk, heads, dk).transpose(0, 2, 1, 3)
    vt = v.reshape(bsz, nc, tk, heads, LANES).transpose(0, 3, 1, 4, 2)
    ones_rows = jnp.zeros((bsz, heads, nc, V_ROWS - LANES, tk), BF16).at[:, :, :, 0, :].set(1.0)
    vt = jnp.concatenate([vt, ones_rows], axis=3)
    in_specs = [pl.BlockSpec((1, 1, 1, dk, w), lambda b, h, i: (b, h, i, 0, 0)),
                pl.BlockSpec((1, 1, nk, dk), lambda b, h, i: (b, h, 0, 0)),
                pl.BlockSpec((1, 1, nc, V_ROWS, tk), lambda b, h, i: (b, h, 0, 0, 0))]
    if mode == "diff":
        kern = functools.partial(_diff_flash_kernel, nc=nc, tk=tk, lambda_init=lambda_init)
        in_specs += [pl.BlockSpec(extra[0].shape, lambda b, h, i: (0, 0)),
                     pl.BlockSpec(extra[1].shape, lambda b, h, i: (0, 0))]
    else:
        kern = functools.partial(_mla_flash_kernel, nc=nc, tk=tk)
    return pl.pallas_call(
        kern,
        grid=(bsz, heads, nq),
        in_specs=in_specs,
        out_specs=pl.BlockSpec((1, tq, LANES), lambda b, h, i: (b, i, h)),
        out_shape=jax.ShapeDtypeStruct((bsz, s, heads * LANES), BF16),
        scratch_shapes=[pltpu.VMEM((tk, w), F32), pltpu.VMEM((tk, w), F32),
                        pltpu.VMEM((1, w), F32), pltpu.VMEM((1, w), F32),
                        pltpu.VMEM((1, w), F32),
                        pltpu.VMEM((V_ROWS, w), F32), pltpu.VMEM((1, w), F32)],
        compiler_params=_cparams(("parallel", "parallel", "arbitrary")),
        name=f"{mode}_flash",
    )(qt, kh, vt, *extra)


def _win_kernel(*refs, band, nblk):
    if band:
        (q_ref, kp_ref, ko_ref, kn_ref, vp_ref, vo_ref, vn_ref, kc_ref, vc_ref,
         sink_ref, o_ref) = refs
    else:
        q_ref, kc_ref, vc_ref, sink_ref, o_ref = refs
    t = q_ref.shape[1]
    nctx = kc_ref.shape[1]
    rows = WIN_GROUP * t
    if band:
        i = pl.program_id(1)
        ncols = 3 * t + nctx
        qq = lax.broadcasted_iota(jnp.int32, (rows, ncols), 0) % t
        cc = lax.broadcasted_iota(jnp.int32, (rows, ncols), 1)
        off_prev = jnp.where(i > 0, 0, t)
        off_next = jnp.where(i < nblk - 1, 0, t)
        is_prev = cc < t
        is_next = jnp.logical_and(cc >= 2 * t, cc < 3 * t)
        bad_prev = jnp.logical_and(is_prev, cc < qq + off_prev)
        bad_next = jnp.logical_and(is_next, cc - 2 * t > qq - off_next)
        mask = jnp.logical_not(jnp.logical_or(bad_prev, bad_next))
    grp = lax.broadcasted_iota(jnp.int32, (rows, 1), 0) // t
    for kvh in range(WIN_KV_HEADS):
        ksl = slice(kvh * LANES, (kvh + 1) * LANES)
        if band:
            kcat = jnp.concatenate([kp_ref[0, :, ksl], ko_ref[0, :, ksl], kn_ref[0, :, ksl],
                                    kc_ref[0, :, ksl]], axis=0)
            vcat = jnp.concatenate([vp_ref[0, :, ksl], vo_ref[0, :, ksl], vn_ref[0, :, ksl],
                                    vc_ref[0, :, ksl]], axis=0)
        else:
            kcat = kc_ref[0, :, ksl]
            vcat = vc_ref[0, :, ksl]
        h0 = kvh * WIN_GROUP
        qs = jnp.concatenate([q_ref[0, :, (h0 + g) * LANES:(h0 + g + 1) * LANES]
                              for g in range(WIN_GROUP)], axis=0)
        sink = jnp.zeros((rows, 1), F32)
        for g in range(WIN_GROUP):
            sink = jnp.where(grp == g, sink_ref[h0 + g:h0 + g + 1, 0:1], sink)
        s = _dot_nt(qs, kcat)
        if band:
            s = jnp.where(mask, s, NEG_INF)
        m = jnp.maximum(jnp.max(s, axis=1, keepdims=True), sink)
        p = jnp.exp(s - m)
        l = jnp.sum(p, axis=1, keepdims=True) + jnp.exp(sink - m)
        o = _dot(p.astype(BF16), vcat) / l
        for pair in range(WIN_GROUP // 2):
            a = o[(2 * pair) * t:(2 * pair + 1) * t]
            b = o[(2 * pair + 1) * t:(2 * pair + 2) * t]
            blk = kvh * (WIN_GROUP // 2) + pair
            o_ref[0, :, blk * LANES:(blk + 1) * LANES] = (a + pltpu.roll(b, LANES // 2, 1)).astype(BF16)


def _win_attn(q, k, v, kc, vc, sink, band):
    bsz, s, qw = q.shape
    kw = kc.shape[2]
    nctx = kc.shape[1]
    out_w = WIN_Q_HEADS * WIN_HEAD_DIM
    if band:
        t = QBLK
        nblk = s // t
        kspec = lambda f: pl.BlockSpec((1, t, kw), f)
        prev = lambda b, i: (b, jnp.maximum(i - 1, 0), 0)
        own = lambda b, i: (b, i, 0)
        nxt = lambda b, i: (b, jnp.minimum(i + 1, nblk - 1), 0)
        ctx = pl.BlockSpec((1, nctx, kw), lambda b, i: (b, 0, 0))
        in_specs = [pl.BlockSpec((1, t, qw), own), kspec(prev), kspec(own), kspec(nxt),
                    kspec(prev), kspec(own), kspec(nxt), ctx, ctx]
        args = (q, k, k, k, v, v, v, kc, vc)
    else:
        t = s
        nblk = 1
        ctx = pl.BlockSpec((1, nctx, kw), lambda b, i: (b, 0, 0))
        in_specs = [pl.BlockSpec((1, t, qw), lambda b, i: (b, i, 0)), ctx, ctx]
        args = (q, kc, vc)
    in_specs.append(pl.BlockSpec(sink.shape, lambda b, i: (0, 0)))
    return pl.pallas_call(
        functools.partial(_win_kernel, band=band, nblk=nblk),
        grid=(bsz, nblk),
        in_specs=in_specs,
        out_specs=pl.BlockSpec((1, t, out_w), lambda b, i: (b, i, 0)),
        out_shape=jax.ShapeDtypeStruct((bsz, s, out_w), BF16),
        compiler_params=_cparams(("parallel", "parallel")),
        name="win_attn" if band else "win_attn_ctx",
    )(*args, sink)


def _merge_kernel(x_ref, ya_ref, yb_ref, yc_ref, sh_ref, sc_ref, gt_ref, g2_ref, g3_ref,
                  wg_ref, bw_ref, wo_ref, o_ref):
    x = x_ref[0]
    d = x.shape[1]
    hm = _norm_mod(x, g2_ref[...], sh_ref[0], sc_ref[0]).astype(BF16)
    merged = jnp.zeros(x.shape, F32)
    for i, y_ref in enumerate((ya_ref, yb_ref, yc_ref)):
        gate = jax.nn.sigmoid(_dot(hm, wg_ref[:, i * d:(i + 1) * d]))
        merged = merged + gate * _dot(y_ref[0], bw_ref[i])
    y = _dot(merged.astype(BF16), wo_ref[...])
    o_ref[0] = x + gt_ref[0] * _rms(y, g3_ref[...])


def _merge(x, ya, yb, yc, shift, scale, gate, g2, g3, wg, bw, wo):
    bsz, n, d = x.shape
    tm = min(512, n)
    tok = pl.BlockSpec((1, tm, d), lambda b, i: (b, i, 0))
    gain = pl.BlockSpec((1, d), lambda b, i: (0, 0))
    return pl.pallas_call(
        _merge_kernel,
        grid=(bsz, n // tm),
        in_specs=[tok, tok, tok, tok, _mod_spec(shift), _mod_spec(scale), _mod_spec(gate),
                  gain, gain, _resident(wg.shape), _resident(bw.shape), _resident(wo.shape)],
        out_specs=tok,
        out_shape=jax.ShapeDtypeStruct(x.shape, F32),
        compiler_params=_cparams(("parallel", "parallel")),
        name="merge",
    )(x, ya, yb, yc, shift, scale, gate, g2, g3, wg, bw, wo)


def _rope_tables(n):
    quarter = ROPE_DIM // 4
    pos = jnp.arange(n)
    row = (pos // GRID_W).astype(F32)
    col = (pos % GRID_W).astype(F32)
    inv_freq = 1.0 / (ROPE_BASE ** (jnp.arange(quarter, dtype=F32) / quarter))
    ang_r = row[:, None] * inv_freq
    ang_c = col[:, None] * inv_freq
    cos = jnp.concatenate([jnp.cos(ang_r)] * 2 + [jnp.cos(ang_c)] * 2, axis=1)
    sin = jnp.concatenate([-jnp.sin(ang_r), jnp.sin(ang_r), -jnp.sin(ang_c), jnp.sin(ang_c)], axis=1)
    return jnp.tile(cos, (1, 2)), jnp.tile(sin, (1, 2))


def _layer_weights(l, ffn_w_in, ffn_w_out, mix_w_in, mla_w_uq, mla_w_ukv, branch_w, mix_w_out):
    d = mix_w_in.shape[1]
    w = mix_w_in[l]
    hw = DIFF_HEADS * 2 * DIFF_D
    wq_n = WIN_Q_HEADS * WIN_HEAD_DIM
    wk_n = WIN_KV_HEADS * WIN_HEAD_DIM
    c = 3 * hw
    w_diff = w[:, :c]
    rest_n = wq_n + 2 * wk_n + MLA_Q_RANK + MLA_KV_RANK + MLA_ROPE
    w_wm = jnp.concatenate([w[:, c:c + rest_n], jnp.zeros((d, _PB_COLS - rest_n), w.dtype)], axis=1)
    w_gate = w[:, c + rest_n:]
    wuq = mla_w_uq[l].reshape(MLA_Q_RANK, MLA_HEADS, MLA_NOPE + MLA_ROPE)
    wuq = jnp.pad(wuq, ((0, 0), (0, 0), (0, 2 * LANES - MLA_NOPE - MLA_ROPE)))
    wuq = wuq.reshape(MLA_Q_RANK, MLA_HEADS * 2 * LANES)
    return dict(
        ffn_in=[ffn_w_in[l, i].astype(BF16) for i in range(2)],
        ffn_out=[ffn_w_out[l, i].astype(BF16) for i in range(2)],
        w_diff=w_diff.astype(BF16), w_wm=w_wm.astype(BF16), w_gate=w_gate.astype(BF16),
        wuq=wuq.astype(BF16), wukv=mla_w_ukv[l].astype(BF16),
        bw=branch_w[l].astype(BF16), wo=mix_w_out[l].astype(BF16))


def kernel(x, c, ctx, c_ctx, ada_w, ada_b, norm_g, ffn_w_in, ffn_w_out, mix_w_in, diff_lambda,
           diff_subln_g, win_sink, mla_q_norm_g, mla_kv_norm_g, mla_w_uq, mla_w_ukv, branch_w,
           mix_w_out):
    bsz, s, d = x.shape
    nctx = ctx.shape[1]
    depth = ada_w.shape[0]

    cvecs = jnp.zeros((8, d), F32).at[:bsz].set(c).at[bsz].set(c_ctx)
    mods = _ada_mods(cvecs, ada_w, ada_b).reshape(depth, 8, N_MOD, d)

    cos_x, sin_x = _rope_tables(s)
    cos_c = jnp.ones((nctx, LANES), F32)
    sin_c = jnp.zeros((nctx, LANES), F32)

    h = ctx
    for l in range(depth):
        last = l == depth - 1
        lambda_init = 0.8 - 0.6 * math.exp(-0.3 * l)
        wts = _layer_weights(l, ffn_w_in, ffn_w_out, mix_w_in, mla_w_uq, mla_w_ukv, branch_w,
                             mix_w_out)
        mx = [mods[l, :bsz, k][:, None, :] for k in range(N_MOD)]
        mc = [mods[l, bsz:bsz + 1, k][:, None, :] for k in range(N_MOD)]
        g = [norm_g[l, k][None, :] for k in range(6)]
        qn = mla_q_norm_g[l][None, :]
        kvn = mla_kv_norm_g[l][None, :]
        dl = diff_lambda[l].astype(F32)
        subln = diff_subln_g[l][None, :]
        sink = jnp.broadcast_to(win_sink[l].astype(F32)[:, None], (WIN_Q_HEADS, LANES))

        x = _ffn(x, mx[0], mx[1], mx[2], g[0], g[1], wts["ffn_in"][0], wts["ffn_out"][0])
        h = _ffn(h, mc[0], mc[1], mc[2], g[0], g[1], wts["ffn_in"][0], wts["ffn_out"][0])

        dq_x, dk_x, dv_x = _proj_diff(x, mx[3], mx[4], g[2], cos_x, sin_x, wts["w_diff"])
        dq_c, dk_c, dv_c = _proj_diff(h, mc[3], mc[4], g[2], cos_c, sin_c, wts["w_diff"])
        wq_x, wk_x, wv_x, mq_x, mk_x, mv_x = _proj_wm(
            x, mx[3], mx[4], g[2], cos_x, sin_x, wts["w_wm"], wts["wuq"], wts["wukv"], qn, kvn)
        wq_c, wk_c, wv_c, mq_c, mk_c, mv_c = _proj_wm(
            h, mc[3], mc[4], g[2], cos_c, sin_c, wts["w_wm"], wts["wuq"], wts["wukv"], qn, kvn)

        cat = lambda u, w: jnp.concatenate([u, w], axis=1)
        ya = _flash("diff", dq_x, cat(dk_x, dk_c), cat(dv_x, dv_c), 256, (dl, subln), lambda_init)
        yb = _win_attn(wq_x, wk_x, wv_x, wk_c, wv_c, sink, band=True)
        yc = _flash("mla", mq_x, cat(mk_x, mk_c), cat(mv_x, mv_c), 512)
        x_new = _merge(x, ya, yb, yc, mx[3], mx[4], mx[5], g[2], g[3],
                       wts["w_gate"], wts["bw"], wts["wo"])
        if not last:
            ca = _flash("diff", dq_c, dk_c, dv_c, 256, (dl, subln), lambda_init)
            cb = _win_attn(wq_c, None, None, wk_c, wv_c, sink, band=False)
            cc = _flash("mla", mq_c, mk_c, mv_c, 512)
            h = _merge(h, ca, cb, cc, mc[3], mc[4], mc[5], g[2], g[3],
                       wts["w_gate"], wts["bw"], wts["wo"])
            h = _ffn(h, mc[6], mc[7], mc[8], g[4], g[5], wts["ffn_in"][1], wts["ffn_out"][1])
        x = _ffn(x_new, mx[6], mx[7], mx[8], g[4], g[5], wts["ffn_in"][1], wts["ffn_out"][1])
    return x
```

```python
import functools
import math

import jax
import jax.numpy as jnp
from jax import lax
from jax.experimental import pallas as pl
from jax.experimental.pallas import tpu as pltpu

F32 = jnp.float32
BF16 = jnp.bfloat16

GRID_W = 64
QBLK = 128
EPS = 1e-6
NEG_INF = -1e30
ROPE_DIM = 64
ROPE_BASE = 10000.0
N_MOD = 9
FFN_RES = 0.5
DIFF_HEADS = 8
DIFF_D = 64
WIN_Q_HEADS = 16
WIN_KV_HEADS = 4
WIN_GROUP = 4
WIN_HEAD_DIM = 64
WIN_SCALE = WIN_HEAD_DIM ** -0.5
MLA_HEADS = 8
MLA_Q_RANK = 384
MLA_KV_RANK = 256
MLA_NOPE = 128
MLA_ROPE = 64
MLA_V = 128
MLA_SCALE = (MLA_NOPE + MLA_ROPE) ** -0.5
LOG2E = math.log2(math.e)

LANES = 128
VMEM_LIMIT = 56 * 1024 * 1024

FLASH_TQ = 512
KV_CHUNK = 1024
V_ROWS = LANES + 16
SPEC_JUMP_LIMIT = 64.0
SPEC_PROBE_ROWS = 128


def _cparams(sem):
    return pltpu.CompilerParams(dimension_semantics=sem, vmem_limit_bytes=VMEM_LIMIT)


def _resident(shape):
    nd = len(shape)
    return pl.BlockSpec(shape, lambda *_: (0,) * nd, pipeline_mode=pl.Buffered(1))


def _rms(x, g):
    return x * lax.rsqrt(jnp.mean(x * x, axis=-1, keepdims=True) + EPS) * g


def _norm_mod(x, g, shift, scale):
    return _rms(x, g) * (1.0 + scale) + shift


def _dot(a, b):
    return jnp.dot(a, b, preferred_element_type=F32)


def _dot_nt(a, b):
    return lax.dot_general(a, b, (((1,), (1,)), ((), ())), preferred_element_type=F32)


def _mod_spec(arr):
    d = arr.shape[-1]
    if arr.shape[0] == 1:
        return pl.BlockSpec((1, 1, d), lambda b, *_: (0, 0, 0))
    return pl.BlockSpec((1, 1, d), lambda b, *_: (b, 0, 0))


def _ada_kernel(c_ref, w_ref, b_ref, o_ref):
    c = c_ref[...]
    a = c * jax.nn.sigmoid(c)
    o_ref[0] = jnp.dot(a, w_ref[0], preferred_element_type=F32,
                       precision=lax.Precision.HIGHEST) + b_ref[0]


def _ada_mods(cvecs, ada_w, ada_b):
    depth, d, nd = ada_w.shape
    rows = cvecs.shape[0]
    tn = 1152 if nd % 1152 == 0 else nd
    return pl.pallas_call(
        _ada_kernel,
        grid=(depth, nd // tn),
        in_specs=[pl.BlockSpec((rows, d), lambda l, j: (0, 0)),
                  pl.BlockSpec((1, d, tn), lambda l, j: (l, 0, j)),
                  pl.BlockSpec((1, 1, tn), lambda l, j: (l, 0, j))],
        out_specs=pl.BlockSpec((1, rows, tn), lambda l, j: (l, 0, j)),
        out_shape=jax.ShapeDtypeStruct((depth, rows, nd), F32),
        compiler_params=_cparams(("parallel", "parallel")),
        name="ada_mods",
    )(cvecs, ada_w, ada_b.reshape(depth, 1, nd))


def _ffn_kernel(x_ref, sh_ref, sc_ref, gt_ref, gpre_ref, gpost_ref, win_ref, wout_ref, o_ref,
                *, ffn_dim, chunk):
    x = x_ref[0]
    xm = _norm_mod(x, gpre_ref[...], sh_ref[0], sc_ref[0]).astype(BF16)
    acc = jnp.zeros(x.shape, F32)
    for c in range(ffn_dim // chunk):
        a = _dot(xm, win_ref[:, c * chunk:(c + 1) * chunk])
        b = _dot(xm, win_ref[:, ffn_dim + c * chunk:ffn_dim + (c + 1) * chunk])
        h = (a * jax.nn.sigmoid(a) * b).astype(BF16)
        acc = acc + _dot(h, wout_ref[c * chunk:(c + 1) * chunk, :])
    o_ref[0] = x + FFN_RES * gt_ref[0] * _rms(acc, gpost_ref[...])


def _ffn(x, shift, scale, gate, g_pre, g_post, w_in, w_out):
    bsz, n, d = x.shape
    ffn_dim = w_out.shape[0]
    tm = min(512, n)
    tok = pl.BlockSpec((1, tm, d), lambda b, i: (b, i, 0))
    gain = pl.BlockSpec((1, d), lambda b, i: (0, 0))
    return pl.pallas_call(
        functools.partial(_ffn_kernel, ffn_dim=ffn_dim, chunk=256),
        grid=(bsz, n // tm),
        in_specs=[tok, _mod_spec(shift), _mod_spec(scale), _mod_spec(gate), gain, gain,
                  _resident(w_in.shape), _resident(w_out.shape)],
        out_specs=tok,
        out_shape=jax.ShapeDtypeStruct(x.shape, F32),
        compiler_params=_cparams(("parallel", "parallel")),
        name="ffn",
    )(x, shift, scale, gate, g_pre, g_post, w_in, w_out)


def _rope_masks(tm):
    lane = lax.broadcasted_iota(jnp.int32, (tm, LANES), 1)
    even = ((lane // (ROPE_DIM // 4)) & 1) == 0
    lo = lane < (LANES // 2)
    return even, lo


def _rope(blk, cos, sin, even):
    q = ROPE_DIM // 4
    partner = jnp.where(even, pltpu.roll(blk, LANES - q, 1), pltpu.roll(blk, q, 1))
    return blk * cos + partner * sin


def _store_vt(vt_ref, h, v_blk):
    tm = v_blk.shape[0]
    vt_ref[0, h, 0, 0:LANES, :] = v_blk.T.astype(BF16)
    row = lax.broadcasted_iota(jnp.int32, (V_ROWS - LANES, tm), 0)
    vt_ref[0, h, 0, LANES:V_ROWS, :] = jnp.where(row == 0, 1.0, 0.0).astype(BF16)


def _kv_specs(bsz, n, tm, heads, dk):
    chunk = min(KV_CHUNK, n)
    per = chunk // tm
    specs = [pl.BlockSpec((1, heads, tm, dk), lambda b, i: (b, 0, i, 0)),
             pl.BlockSpec((1, heads, 1, V_ROWS, tm), lambda b, i: (b, 0, i // per, 0, i % per))]
    shapes = [jax.ShapeDtypeStruct((bsz, heads, n, dk), BF16),
              jax.ShapeDtypeStruct((bsz, heads, n // chunk, V_ROWS, chunk), BF16)]
    return specs, shapes


def _proj_diff_kernel(x_ref, sh_ref, sc_ref, g_ref, cos_ref, sin_ref, w_ref,
                      qt_ref, k_ref, vt_ref):
    hm = _norm_mod(x_ref[0], g_ref[...], sh_ref[0], sc_ref[0]).astype(BF16)
    tm = hm.shape[0]
    cos = cos_ref[...]
    sin = sin_ref[...]
    even, _ = _rope_masks(tm)
    hw = DIFF_HEADS * 2 * DIFF_D
    q = _dot(hm, w_ref[:, 0:hw])
    k = _dot(hm, w_ref[:, hw:2 * hw])
    v = _dot(hm, w_ref[:, 2 * hw:3 * hw])
    scale = LOG2E / math.sqrt(DIFF_D)
    top = lax.broadcasted_iota(jnp.int32, (LANES, tm), 0) < DIFF_D
    for h in range(DIFF_HEADS):
        hs = slice(h * LANES, (h + 1) * LANES)
        qbt = (_rope(q[:, hs], cos, sin, even) * scale).T
        qt_ref[0, h, 0, :, 0:tm] = jnp.where(top, qbt, 0.0).astype(BF16)
        qt_ref[0, h, 0, :, tm:2 * tm] = jnp.where(top, 0.0, qbt).astype(BF16)
        k_ref[0, h] = _rope(k[:, hs], cos, sin, even).astype(BF16)
        _store_vt(vt_ref, h, v[:, hs])


def _proj_diff(x, shift, scale, g, cos, sin, w):
    bsz, n, d = x.shape
    tm = min(FLASH_TQ, n)
    tok = pl.BlockSpec((1, tm, d), lambda b, i: (b, i, 0))
    tab = pl.BlockSpec((tm, LANES), lambda b, i: (i, 0))
    kv_specs, kv_shapes = _kv_specs(bsz, n, tm, DIFF_HEADS, LANES)
    return pl.pallas_call(
        _proj_diff_kernel,
        grid=(bsz, n // tm),
        in_specs=[tok, _mod_spec(shift), _mod_spec(scale),
                  pl.BlockSpec((1, d), lambda b, i: (0, 0)), tab, tab, _resident(w.shape)],
        out_specs=[pl.BlockSpec((1, DIFF_HEADS, 1, LANES, 2 * tm), lambda b, i: (b, 0, i, 0, 0))]
        + kv_specs,
        out_shape=[jax.ShapeDtypeStruct((bsz, DIFF_HEADS, n // tm, LANES, 2 * tm), BF16)] + kv_shapes,
        compiler_params=_cparams(("parallel", "parallel")),
        name="proj_diff",
    )(x, shift, scale, g, cos, sin, w)


_WQ0, _WK0, _WV0 = 0, 1024, 1280
_CQ0, _CKV0, _KR0, _PB_COLS = 1536, 1920, 2176, 2304


def _proj_wm_kernel(x_ref, sh_ref, sc_ref, g_ref, cos_ref, sin_ref, w_ref, wuq_ref, wukv_ref,
                    qn_ref, kvn_ref, wq_ref, wk_ref, wv_ref, mqt_ref, mk_ref, mvt_ref):
    hm = _norm_mod(x_ref[0], g_ref[...], sh_ref[0], sc_ref[0]).astype(BF16)
    tm = hm.shape[0]
    cos = cos_ref[...]
    sin = sin_ref[...]
    even, lo = _rope_masks(tm)
    half = LANES // 2
    p = _dot(hm, w_ref[...])

    def split_heads(blk, out_ref, j):
        out_ref[0, :, (2 * j) * LANES:(2 * j + 1) * LANES] = jnp.where(lo, blk, 0.0).astype(BF16)
        out_ref[0, :, (2 * j + 1) * LANES:(2 * j + 2) * LANES] = (
            jnp.where(lo, pltpu.roll(blk, half, 1), 0.0).astype(BF16))

    for j in range(WIN_Q_HEADS // 2):
        blk = _rope(p[:, _WQ0 + j * LANES:_WQ0 + (j + 1) * LANES], cos, sin, even) * WIN_SCALE
        split_heads(blk, wq_ref, j)
    for j in range(WIN_KV_HEADS // 2):
        split_heads(_rope(p[:, _WK0 + j * LANES:_WK0 + (j + 1) * LANES], cos, sin, even), wk_ref, j)
        split_heads(p[:, _WV0 + j * LANES:_WV0 + (j + 1) * LANES], wv_ref, j)

    cq = _rms(p[:, _CQ0:_CQ0 + MLA_Q_RANK], qn_ref[...]).astype(BF16)
    q2 = _dot(cq, wuq_ref[...])
    ckv = _rms(p[:, _CKV0:_CKV0 + MLA_KV_RANK], kvn_ref[...]).astype(BF16)
    kv = _dot(ckv, wukv_ref[...])
    kr = _rope(p[:, _KR0:_KR0 + LANES], cos, sin, even).astype(BF16)
    for h in range(MLA_HEADS):
        c0 = 2 * h * LANES
        qn = q2[:, c0:c0 + LANES] * (MLA_SCALE * LOG2E)
        qr = _rope(q2[:, c0 + LANES:c0 + 2 * LANES], cos, sin, even) * (MLA_SCALE * LOG2E)
        mqt_ref[0, h, 0, 0:LANES, :] = qn.T.astype(BF16)
        mqt_ref[0, h, 0, LANES:2 * LANES, :] = qr.T.astype(BF16)
        mk_ref[0, h, :, 0:LANES] = kv[:, c0:c0 + LANES].astype(BF16)
        mk_ref[0, h, :, LANES:2 * LANES] = kr
        _store_vt(mvt_ref, h, kv[:, c0 + LANES:c0 + 2 * LANES])


def _proj_wm(x, shift, scale, g, cos, sin, w, wuq, wukv, qn, kvn):
    bsz, n, d = x.shape
    tm = min(FLASH_TQ, n)
    tok = lambda c: pl.BlockSpec((1, tm, c), lambda b, i: (b, i, 0))
    tab = pl.BlockSpec((tm, LANES), lambda b, i: (i, 0))
    row = lambda c: pl.BlockSpec((1, c), lambda b, i: (0, 0))
    widths = (WIN_Q_HEADS * LANES, WIN_KV_HEADS * LANES, WIN_KV_HEADS * LANES)
    kv_specs, kv_shapes = _kv_specs(bsz, n, tm, MLA_HEADS, 2 * LANES)
    return pl.pallas_call(
        _proj_wm_kernel,
        grid=(bsz, n // tm),
        in_specs=[tok(d), _mod_spec(shift), _mod_spec(scale), row(d), tab, tab,
                  _resident(w.shape), _resident(wuq.shape), _resident(wukv.shape),
                  row(MLA_Q_RANK), row(MLA_KV_RANK)],
        out_specs=[tok(c) for c in widths]
        + [pl.BlockSpec((1, MLA_HEADS, 1, 2 * LANES, tm), lambda b, i: (b, 0, i, 0, 0))] + kv_specs,
        out_shape=[jax.ShapeDtypeStruct((bsz, n, c), BF16) for c in widths]
        + [jax.ShapeDtypeStruct((bsz, MLA_HEADS, n // tm, 2 * LANES, tm), BF16)] + kv_shapes,
        compiler_params=_cparams(("parallel", "parallel")),
        name="proj_wm",
    )(x, shift, scale, g, cos, sin, w, wuq, wukv, qn, kvn)


def _segments(seg_refs):
    segs = []
    for k_ref, vt_ref in zip(seg_refs[0::2], seg_refs[1::2]):
        segs.append((k_ref, vt_ref, vt_ref.shape[2], vt_ref.shape[4]))
    return segs


def _flash_loop_exact(qt, segs, m_sc, acc_sc):
    m_sc[...] = jnp.full(m_sc.shape, NEG_INF, F32)
    acc_sc[...] = jnp.zeros(acc_sc.shape, F32)
    for k_ref, vt_ref, nc, tk in segs:
        def body(j, carry, k_ref=k_ref, vt_ref=vt_ref, tk=tk):
            start = pl.multiple_of(j * tk, tk)
            s = _dot(k_ref[0, 0, pl.ds(start, tk), :], qt)
            m_prev = m_sc[...]
            m_new = jnp.maximum(m_prev, jnp.max(s, axis=0, keepdims=True))
            p = jnp.exp2(s - m_new)
            acc_sc[...] = jnp.exp2(m_prev - m_new) * acc_sc[...] + _dot(vt_ref[0, 0, j], p.astype(BF16))
            m_sc[...] = m_new
            return carry
        lax.fori_loop(0, nc, body, 0)


def _flash_loop_spec(qt, segs, m_sc, acc_sc, jump_sc):
    probe = _dot(segs[0][0][0, 0, 0:SPEC_PROBE_ROWS, :], qt)
    m_sc[...] = jnp.max(probe, axis=0, keepdims=True)
    acc_sc[...] = jnp.zeros(acc_sc.shape, F32)
    jump_sc[...] = jnp.zeros(jump_sc.shape, F32)
    for k_ref, vt_ref, nc, tk in segs:
        for j in range(nc):
            s = _dot(k_ref[0, 0, j * tk:(j + 1) * tk, :], qt)
            m_prev = m_sc[...]
            cmax = jnp.max(s, axis=0, keepdims=True)
            p = jnp.exp2(s - m_prev)
            pv = _dot(vt_ref[0, 0, j], p.astype(BF16))
            m_new = jnp.maximum(m_prev, cmax)
            acc_sc[...] = (acc_sc[...] + pv) * jnp.exp2(m_prev - m_new)
            jump_sc[...] = jnp.maximum(jump_sc[...], cmax - m_prev)
            m_sc[...] = m_new


def _flash_tile(qt, segs, m_sc, acc_sc, jump_sc):
    _flash_loop_spec(qt, segs, m_sc, acc_sc, jump_sc)
    worst = jnp.max(jump_sc[...])

    @pl.when(jnp.logical_not(worst <= SPEC_JUMP_LIMIT))
    def _redo():
        _flash_loop_exact(qt, segs, m_sc, acc_sc)


def _diff_flash_kernel(*refs, nseg, lambda_init):
    qt_ref, seg_refs = refs[0], refs[1:1 + 2 * nseg]
    dl_ref, g_ref, o_ref, m_sc, acc_sc, jump_sc = refs[1 + 2 * nseg:]
    _flash_tile(qt_ref[0, 0, 0], _segments(seg_refs), m_sc, acc_sc, jump_sc)
    tq = o_ref.shape[1]
    o = acc_sc[0:LANES, :] / acc_sc[LANES:LANES + 1, :]
    dl = dl_ref[...]
    lam = (jnp.exp(jnp.sum(dl[0:1] * dl[1:2], axis=1, keepdims=True))
           - jnp.exp(jnp.sum(dl[2:3] * dl[3:4], axis=1, keepdims=True)) + lambda_init)
    y = (o[:, :tq] - lam * o[:, tq:]).T
    o_ref[0] = (_rms(y, g_ref[...]) * (1.0 - lambda_init)).astype(BF16)


def _mla_flash_kernel(*refs, nseg):
    qt_ref, seg_refs = refs[0], refs[1:1 + 2 * nseg]
    o_ref, m_sc, acc_sc, jump_sc = refs[1 + 2 * nseg:]
    _flash_tile(qt_ref[0, 0, 0], _segments(seg_refs), m_sc, acc_sc, jump_sc)
    o_ref[0] = (acc_sc[0:LANES, :] / acc_sc[LANES:LANES + 1, :]).T.astype(BF16)


def _flash(mode, qt, kv_segs, extra=(), lambda_init=0.0):
    bsz, heads, nq, dk, w = qt.shape
    maps = 2 if mode == "diff" else 1
    tq = w // maps
    in_specs = [pl.BlockSpec((1, 1, 1, dk, w), lambda b, h, i: (b, h, i, 0, 0))]
    args = [qt]
    for k, vt in kv_segs:
        in_specs += [pl.BlockSpec((1, 1) + k.shape[2:], lambda b, h, i: (b, h, 0, 0)),
                     pl.BlockSpec((1, 1) + vt.shape[2:], lambda b, h, i: (b, h, 0, 0, 0))]
        args += [k, vt]
    if mode == "diff":
        kern = functools.partial(_diff_flash_kernel, nseg=len(kv_segs), lambda_init=lambda_init)
        in_specs += [pl.BlockSpec(e.shape, lambda b, h, i: (0, 0)) for e in extra]
    else:
        kern = functools.partial(_mla_flash_kernel, nseg=len(kv_segs))
    return pl.pallas_call(
        kern,
        grid=(bsz, heads, nq),
        in_specs=in_specs,
        out_specs=pl.BlockSpec((1, tq, LANES), lambda b, h, i: (b, i, h)),
        out_shape=jax.ShapeDtypeStruct((bsz, nq * tq, heads * LANES), BF16),
        scratch_shapes=[pltpu.VMEM((1, w), F32), pltpu.VMEM((V_ROWS, w), F32),
                        pltpu.VMEM((1, w), F32)],
        compiler_params=_cparams(("parallel", "parallel", "arbitrary")),
        name=f"{mode}_flash",
    )(*args, *extra)


def _win_kernel(*refs, band, nblk):
    if band:
        (q_ref, kp_ref, ko_ref, kn_ref, vp_ref, vo_ref, vn_ref, kc_ref, vc_ref,
         sink_ref, o_ref) = refs
    else:
        q_ref, kc_ref, vc_ref, sink_ref, o_ref = refs
    t = q_ref.shape[1]
    nctx = kc_ref.shape[1]
    rows = WIN_GROUP * t
    if band:
        i = pl.program_id(1)
        ncols = 3 * t + nctx
        qq = lax.broadcasted_iota(jnp.int32, (rows, ncols), 0) % t
        cc = lax.broadcasted_iota(jnp.int32, (rows, ncols), 1)
        off_prev = jnp.where(i > 0, 0, t)
        off_next = jnp.where(i < nblk - 1, 0, t)
        is_prev = cc < t
        is_next = jnp.logical_and(cc >= 2 * t, cc < 3 * t)
        bad_prev = jnp.logical_and(is_prev, cc < qq + off_prev)
        bad_next = jnp.logical_and(is_next, cc - 2 * t > qq - off_next)
        mask = jnp.logical_not(jnp.logical_or(bad_prev, bad_next))
    grp = lax.broadcasted_iota(jnp.int32, (rows, 1), 0) // t
    for kvh in range(WIN_KV_HEADS):
        ksl = slice(kvh * LANES, (kvh + 1) * LANES)
        if band:
            kcat = jnp.concatenate([kp_ref[0, :, ksl], ko_ref[0, :, ksl], kn_ref[0, :, ksl],
                                    kc_ref[0, :, ksl]], axis=0)
            vcat = jnp.concatenate([vp_ref[0, :, ksl], vo_ref[0, :, ksl], vn_ref[0, :, ksl],
                                    vc_ref[0, :, ksl]], axis=0)
        else:
            kcat = kc_ref[0, :, ksl]
            vcat = vc_ref[0, :, ksl]
        h0 = kvh * WIN_GROUP
        qs = jnp.concatenate([q_ref[0, :, (h0 + g) * LANES:(h0 + g + 1) * LANES]
                              for g in range(WIN_GROUP)], axis=0)
        sink = jnp.zeros((rows, 1), F32)
        for g in range(WIN_GROUP):
            sink = jnp.where(grp == g, sink_ref[h0 + g:h0 + g + 1, 0:1], sink)
        s = _dot_nt(qs, kcat)
        if band:
            s = jnp.where(mask, s, NEG_INF)
        m = jnp.maximum(jnp.max(s, axis=1, keepdims=True), sink)
        p = jnp.exp(s - m)
        l = jnp.sum(p, axis=1, keepdims=True) + jnp.exp(sink - m)
        o = _dot(p.astype(BF16), vcat) / l
        for pair in range(WIN_GROUP // 2):
            a = o[(2 * pair) * t:(2 * pair + 1) * t]
            b = o[(2 * pair + 1) * t:(2 * pair + 2) * t]
            blk = kvh * (WIN_GROUP // 2) + pair
            o_ref[0, :, blk * LANES:(blk + 1) * LANES] = (a + pltpu.roll(b, LANES // 2, 1)).astype(BF16)


def _win_attn(q, k, v, kc, vc, sink, band):
    bsz, s, qw = q.shape
    kw = kc.shape[2]
    nctx = kc.shape[1]
    out_w = WIN_Q_HEADS * WIN_HEAD_DIM
    if band:
        t = QBLK
        nblk = s // t
        kspec = lambda f: pl.BlockSpec((1, t, kw), f)
        prev = lambda b, i: (b, jnp.maximum(i - 1, 0), 0)
        own = lambda b, i: (b, i, 0)
        nxt = lambda b, i: (b, jnp.minimum(i + 1, nblk - 1), 0)
        ctx = pl.BlockSpec((1, nctx, kw), lambda b, i: (b, 0, 0))
        in_specs = [pl.BlockSpec((1, t, qw), own), kspec(prev), kspec(own), kspec(nxt),
                    kspec(prev), kspec(own), kspec(nxt), ctx, ctx]
        args = (q, k, k, k, v, v, v, kc, vc)
    else:
        t = s
        nblk = 1
        ctx = pl.BlockSpec((1, nctx, kw), lambda b, i: (b, 0, 0))
        in_specs = [pl.BlockSpec((1, t, qw), lambda b, i: (b, i, 0)), ctx, ctx]
        args = (q, kc, vc)
    in_specs.append(pl.BlockSpec(sink.shape, lambda b, i: (0, 0)))
    return pl.pallas_call(
        functools.partial(_win_kernel, band=band, nblk=nblk),
        grid=(bsz, nblk),
        in_specs=in_specs,
        out_specs=pl.BlockSpec((1, t, out_w), lambda b, i: (b, i, 0)),
        out_shape=jax.ShapeDtypeStruct((bsz, s, out_w), BF16),
        compiler_params=_cparams(("parallel", "parallel")),
        name="win_attn" if band else "win_attn_ctx",
    )(*args, sink)


def _merge_kernel(x_ref, ya_ref, yb_ref, yc_ref, sh_ref, sc_ref, gt_ref, g2_ref, g3_ref,
                  wg_ref, bw_ref, wo_ref, o_ref):
    x = x_ref[0]
    d = x.shape[1]
    hm = _norm_mod(x, g2_ref[...], sh_ref[0], sc_ref[0]).astype(BF16)
    merged = jnp.zeros(x.shape, F32)
    for i, y_ref in enumerate((ya_ref, yb_ref, yc_ref)):
        gate = jax.nn.sigmoid(_dot(hm, wg_ref[:, i * d:(i + 1) * d]))
        merged = merged + gate * _dot(y_ref[0], bw_ref[i])
    y = _dot(merged.astype(BF16), wo_ref[...])
    o_ref[0] = x + gt_ref[0] * _rms(y, g3_ref[...])


def _merge(x, ya, yb, yc, shift, scale, gate, g2, g3, wg, bw, wo):
    bsz, n, d = x.shape
    tm = min(512, n)
    tok = pl.BlockSpec((1, tm, d), lambda b, i: (b, i, 0))
    gain = pl.BlockSpec((1, d), lambda b, i: (0, 0))
    return pl.pallas_call(
        _merge_kernel,
        grid=(bsz, n // tm),
        in_specs=[tok, tok, tok, tok, _mod_spec(shift), _mod_spec(scale), _mod_spec(gate),
                  gain, gain, _resident(wg.shape), _resident(bw.shape), _resident(wo.shape)],
        out_specs=tok,
        out_shape=jax.ShapeDtypeStruct(x.shape, F32),
        compiler_params=_cparams(("parallel", "parallel")),
        name="merge",
    )(x, ya, yb, yc, shift, scale, gate, g2, g3, wg, bw, wo)


def _rope_tables(n):
    quarter = ROPE_DIM // 4
    pos = jnp.arange(n)
    row = (pos // GRID_W).astype(F32)
    col = (pos % GRID_W).astype(F32)
    inv_freq = 1.0 / (ROPE_BASE ** (jnp.arange(quarter, dtype=F32) / quarter))
    ang_r = row[:, None] * inv_freq
    ang_c = col[:, None] * inv_freq
    cos = jnp.concatenate([jnp.cos(ang_r)] * 2 + [jnp.cos(ang_c)] * 2, axis=1)
    sin = jnp.concatenate([-jnp.sin(ang_r), jnp.sin(ang_r), -jnp.sin(ang_c), jnp.sin(ang_c)], axis=1)
    return jnp.tile(cos, (1, 2)), jnp.tile(sin, (1, 2))


def _layer_weights(l, ffn_w_in, ffn_w_out, mix_w_in, mla_w_uq, mla_w_ukv, branch_w, mix_w_out):
    d = mix_w_in.shape[1]
    w = mix_w_in[l]
    hw = DIFF_HEADS * 2 * DIFF_D
    wq_n = WIN_Q_HEADS * WIN_HEAD_DIM
    wk_n = WIN_KV_HEADS * WIN_HEAD_DIM
    c = 3 * hw
    w_diff = w[:, :c]
    rest_n = wq_n + 2 * wk_n + MLA_Q_RANK + MLA_KV_RANK + MLA_ROPE
    w_wm = jnp.concatenate([w[:, c:c + rest_n], jnp.zeros((d, _PB_COLS - rest_n), w.dtype)], axis=1)
    w_gate = w[:, c + rest_n:]
    wuq = mla_w_uq[l].reshape(MLA_Q_RANK, MLA_HEADS, MLA_NOPE + MLA_ROPE)
    wuq = jnp.pad(wuq, ((0, 0), (0, 0), (0, 2 * LANES - MLA_NOPE - MLA_ROPE)))
    wuq = wuq.reshape(MLA_Q_RANK, MLA_HEADS * 2 * LANES)
    return dict(
        ffn_in=[ffn_w_in[l, i].astype(BF16) for i in range(2)],
        ffn_out=[ffn_w_out[l, i].astype(BF16) for i in range(2)],
        w_diff=w_diff.astype(BF16), w_wm=w_wm.astype(BF16), w_gate=w_gate.astype(BF16),
        wuq=wuq.astype(BF16), wukv=mla_w_ukv[l].astype(BF16),
        bw=branch_w[l].astype(BF16), wo=mix_w_out[l].astype(BF16))


def kernel(x, c, ctx, c_ctx, ada_w, ada_b, norm_g, ffn_w_in, ffn_w_out, mix_w_in, diff_lambda,
           diff_subln_g, win_sink, mla_q_norm_g, mla_kv_norm_g, mla_w_uq, mla_w_ukv, branch_w,
           mix_w_out):
    bsz, s, d = x.shape
    nctx = ctx.shape[1]
    depth = ada_w.shape[0]

    cvecs = jnp.zeros((8, d), F32).at[:bsz].set(c).at[bsz].set(c_ctx)
    mods = _ada_mods(cvecs, ada_w, ada_b).reshape(depth, 8, N_MOD, d)

    cos_x, sin_x = _rope_tables(s)
    cos_c = jnp.ones((nctx, LANES), F32)
    sin_c = jnp.zeros((nctx, LANES), F32)

    h = ctx
    for l in range(depth):
        last = l == depth - 1
        lambda_init = 0.8 - 0.6 * math.exp(-0.3 * l)
        wts = _layer_weights(l, ffn_w_in, ffn_w_out, mix_w_in, mla_w_uq, mla_w_ukv, branch_w,
                             mix_w_out)
        mx = [mods[l, :bsz, k][:, None, :] for k in range(N_MOD)]
        mc = [mods[l, bsz:bsz + 1, k][:, None, :] for k in range(N_MOD)]
        g = [norm_g[l, k][None, :] for k in range(6)]
        qn = mla_q_norm_g[l][None, :]
        kvn = mla_kv_norm_g[l][None, :]
        dl = diff_lambda[l].astype(F32)
        subln = diff_subln_g[l][None, :]
        sink = jnp.broadcast_to(win_sink[l].astype(F32)[:, None], (WIN_Q_HEADS, LANES))

        x = _ffn(x, mx[0], mx[1], mx[2], g[0], g[1], wts["ffn_in"][0], wts["ffn_out"][0])
        h = _ffn(h, mc[0], mc[1], mc[2], g[0], g[1], wts["ffn_in"][0], wts["ffn_out"][0])

        dqt_x, dk_x, dvt_x = _proj_diff(x, mx[3], mx[4], g[2], cos_x, sin_x, wts["w_diff"])
        dqt_c, dk_c, dvt_c = _proj_diff(h, mc[3], mc[4], g[2], cos_c, sin_c, wts["w_diff"])
        wq_x, wk_x, wv_x, mqt_x, mk_x, mvt_x = _proj_wm(
            x, mx[3], mx[4], g[2], cos_x, sin_x, wts["w_wm"], wts["wuq"], wts["wukv"], qn, kvn)
        wq_c, wk_c, wv_c, mqt_c, mk_c, mvt_c = _proj_wm(
            h, mc[3], mc[4], g[2], cos_c, sin_c, wts["w_wm"], wts["wuq"], wts["wukv"], qn, kvn)

        ya = _flash("diff", dqt_x, [(dk_x, dvt_x), (dk_c, dvt_c)], (dl, subln), lambda_init)
        yb = _win_attn(wq_x, wk_x, wv_x, wk_c, wv_c, sink, band=True)
        yc = _flash("mla", mqt_x, [(mk_x, mvt_x), (mk_c, mvt_c)])
        x_new = _merge(x, ya, yb, yc, mx[3], mx[4], mx[5], g[2], g[3],
                       wts["w_gate"], wts["bw"], wts["wo"])
        if not last:
            ca = _flash("diff", dqt_c, [(dk_c, dvt_c)], (dl, subln), lambda_init)
            cb = _win_attn(wq_c, None, None, wk_c, wv_c, sink, band=False)
            cc = _flash("mla", mqt_c, [(mk_c, mvt_c)])
            h = _merge(h, ca, cb, cc, mc[3], mc[4], mc[5], g[2], g[3],
                       wts["w_gate"], wts["bw"], wts["wo"])
            h = _ffn(h, mc[6], mc[7], mc[8], g[4], g[5], wts["ffn_in"][1], wts["ffn_out"][1])
        x = _ffn(x_new, mx[6], mx[7], mx[8], g[4], g[5], wts["ffn_in"][1], wts["ffn_out"][1])
    return x
```

```python
import functools
import math

import jax
import jax.numpy as jnp
from jax import lax
from jax.experimental import pallas as pl
from jax.experimental.pallas import tpu as pltpu

F32 = jnp.float32
BF16 = jnp.bfloat16

GRID_W = 64
QBLK = 128
EPS = 1e-6
NEG_INF = -1e30
ROPE_DIM = 64
ROPE_BASE = 10000.0
N_MOD = 9
FFN_RES = 0.5
DIFF_HEADS = 8
DIFF_D = 64
WIN_Q_HEADS = 16
WIN_KV_HEADS = 4
WIN_GROUP = 4
WIN_HEAD_DIM = 64
WIN_SCALE = WIN_HEAD_DIM ** -0.5
MLA_HEADS = 8
MLA_Q_RANK = 384
MLA_KV_RANK = 256
MLA_NOPE = 128
MLA_ROPE = 64
MLA_V = 128
MLA_SCALE = (MLA_NOPE + MLA_ROPE) ** -0.5
LOG2E = math.log2(math.e)

LANES = 128
VMEM_LIMIT = 56 * 1024 * 1024

FLASH_TQ = 512
KV_CHUNK = 2048
V_ROWS = LANES + 16
SPEC_JUMP_LIMIT = 64.0
SPEC_PROBE_ROWS = 128


def _cparams(sem):
    return pltpu.CompilerParams(dimension_semantics=sem, vmem_limit_bytes=VMEM_LIMIT)


def _resident(shape):
    nd = len(shape)
    return pl.BlockSpec(shape, lambda *_: (0,) * nd, pipeline_mode=pl.Buffered(1))


def _rms(x, g):
    return x * lax.rsqrt(jnp.mean(x * x, axis=-1, keepdims=True) + EPS) * g


def _norm_mod(x, g, shift, scale):
    return _rms(x, g) * (1.0 + scale) + shift


def _dot(a, b):
    return jnp.dot(a, b, preferred_element_type=F32)


def _dot_nt(a, b):
    return lax.dot_general(a, b, (((1,), (1,)), ((), ())), preferred_element_type=F32)


def _mod_spec(arr):
    d = arr.shape[-1]
    if arr.shape[0] == 1:
        return pl.BlockSpec((1, 1, d), lambda b, *_: (0, 0, 0))
    return pl.BlockSpec((1, 1, d), lambda b, *_: (b, 0, 0))


def _ada_kernel(c_ref, w_ref, b_ref, o_ref):
    c = c_ref[...]
    a = c * jax.nn.sigmoid(c)
    o_ref[0] = jnp.dot(a, w_ref[0], preferred_element_type=F32,
                       precision=lax.Precision.HIGHEST) + b_ref[0]


def _ada_mods(cvecs, ada_w, ada_b):
    depth, d, nd = ada_w.shape
    rows = cvecs.shape[0]
    tn = 1152 if nd % 1152 == 0 else nd
    return pl.pallas_call(
        _ada_kernel,
        grid=(depth, nd // tn),
        in_specs=[pl.BlockSpec((rows, d), lambda l, j: (0, 0)),
                  pl.BlockSpec((1, d, tn), lambda l, j: (l, 0, j)),
                  pl.BlockSpec((1, 1, tn), lambda l, j: (l, 0, j))],
        out_specs=pl.BlockSpec((1, rows, tn), lambda l, j: (l, 0, j)),
        out_shape=jax.ShapeDtypeStruct((depth, rows, nd), F32),
        compiler_params=_cparams(("parallel", "parallel")),
        name="ada_mods",
    )(cvecs, ada_w, ada_b.reshape(depth, 1, nd))


def _ffn_kernel(x_ref, sh_ref, sc_ref, gt_ref, gpre_ref, gpost_ref, win_ref, wout_ref, o_ref,
                *, ffn_dim, chunk):
    x = x_ref[0]
    xm = _norm_mod(x, gpre_ref[...], sh_ref[0], sc_ref[0]).astype(BF16)
    acc = jnp.zeros(x.shape, F32)
    for c in range(ffn_dim // chunk):
        a = _dot(xm, win_ref[:, c * chunk:(c + 1) * chunk])
        b = _dot(xm, win_ref[:, ffn_dim + c * chunk:ffn_dim + (c + 1) * chunk])
        h = (a * jax.nn.sigmoid(a) * b).astype(BF16)
        acc = acc + _dot(h, wout_ref[c * chunk:(c + 1) * chunk, :])
    o_ref[0] = x + FFN_RES * gt_ref[0] * _rms(acc, gpost_ref[...])


def _ffn(x, shift, scale, gate, g_pre, g_post, w_in, w_out):
    bsz, n, d = x.shape
    ffn_dim = w_out.shape[0]
    tm = min(512, n)
    tok = pl.BlockSpec((1, tm, d), lambda b, i: (b, i, 0))
    gain = pl.BlockSpec((1, d), lambda b, i: (0, 0))
    return pl.pallas_call(
        functools.partial(_ffn_kernel, ffn_dim=ffn_dim, chunk=256),
        grid=(bsz, n // tm),
        in_specs=[tok, _mod_spec(shift), _mod_spec(scale), _mod_spec(gate), gain, gain,
                  _resident(w_in.shape), _resident(w_out.shape)],
        out_specs=tok,
        out_shape=jax.ShapeDtypeStruct(x.shape, F32),
        compiler_params=_cparams(("parallel", "parallel")),
        name="ffn",
    )(x, shift, scale, gate, g_pre, g_post, w_in, w_out)


def _rope_masks(tm):
    lane = lax.broadcasted_iota(jnp.int32, (tm, LANES), 1)
    even = ((lane // (ROPE_DIM // 4)) & 1) == 0
    lo = lane < (LANES // 2)
    return even, lo


def _rope(blk, cos, sin, even):
    q = ROPE_DIM // 4
    partner = jnp.where(even, pltpu.roll(blk, LANES - q, 1), pltpu.roll(blk, q, 1))
    return blk * cos + partner * sin


def _store_vt(vt_ref, h, v_blk):
    tm = v_blk.shape[0]
    vt_ref[0, h, 0, 0:LANES, :] = v_blk.T.astype(BF16)
    row = lax.broadcasted_iota(jnp.int32, (V_ROWS - LANES, tm), 0)
    vt_ref[0, h, 0, LANES:V_ROWS, :] = jnp.where(row == 0, 1.0, 0.0).astype(BF16)


def _kv_specs(bsz, n, tm, heads, dk):
    chunk = min(KV_CHUNK, n)
    per = chunk // tm
    specs = [pl.BlockSpec((1, heads, tm, dk), lambda b, i: (b, 0, i, 0)),
             pl.BlockSpec((1, heads, 1, V_ROWS, tm), lambda b, i: (b, 0, i // per, 0, i % per))]
    shapes = [jax.ShapeDtypeStruct((bsz, heads, n, dk), BF16),
              jax.ShapeDtypeStruct((bsz, heads, n // chunk, V_ROWS, chunk), BF16)]
    return specs, shapes


def _proj_diff_kernel(x_ref, sh_ref, sc_ref, g_ref, cos_ref, sin_ref, w_ref,
                      qt_ref, k_ref, vt_ref):
    hm = _norm_mod(x_ref[0], g_ref[...], sh_ref[0], sc_ref[0]).astype(BF16)
    tm = hm.shape[0]
    cos = cos_ref[...]
    sin = sin_ref[...]
    even, _ = _rope_masks(tm)
    hw = DIFF_HEADS * 2 * DIFF_D
    q = _dot(hm, w_ref[:, 0:hw])
    k = _dot(hm, w_ref[:, hw:2 * hw])
    v = _dot(hm, w_ref[:, 2 * hw:3 * hw])
    scale = LOG2E / math.sqrt(DIFF_D)
    top = lax.broadcasted_iota(jnp.int32, (LANES, tm), 0) < DIFF_D
    for h in range(DIFF_HEADS):
        hs = slice(h * LANES, (h + 1) * LANES)
        qbt = (_rope(q[:, hs], cos, sin, even) * scale).T
        qt_ref[0, h, 0, :, 0:tm] = jnp.where(top, qbt, 0.0).astype(BF16)
        qt_ref[0, h, 0, :, tm:2 * tm] = jnp.where(top, 0.0, qbt).astype(BF16)
        k_ref[0, h] = _rope(k[:, hs], cos, sin, even).astype(BF16)
        _store_vt(vt_ref, h, v[:, hs])


def _proj_diff(x, shift, scale, g, cos, sin, w):
    bsz, n, d = x.shape
    tm = min(FLASH_TQ, n)
    tok = pl.BlockSpec((1, tm, d), lambda b, i: (b, i, 0))
    tab = pl.BlockSpec((tm, LANES), lambda b, i: (i, 0))
    kv_specs, kv_shapes = _kv_specs(bsz, n, tm, DIFF_HEADS, LANES)
    return pl.pallas_call(
        _proj_diff_kernel,
        grid=(bsz, n // tm),
        in_specs=[tok, _mod_spec(shift), _mod_spec(scale),
                  pl.BlockSpec((1, d), lambda b, i: (0, 0)), tab, tab, _resident(w.shape)],
        out_specs=[pl.BlockSpec((1, DIFF_HEADS, 1, LANES, 2 * tm), lambda b, i: (b, 0, i, 0, 0))]
        + kv_specs,
        out_shape=[jax.ShapeDtypeStruct((bsz, DIFF_HEADS, n // tm, LANES, 2 * tm), BF16)] + kv_shapes,
        compiler_params=_cparams(("parallel", "parallel")),
        name="proj_diff",
    )(x, shift, scale, g, cos, sin, w)


_WQ0, _WK0, _WV0 = 0, 1024, 1280
_CQ0, _CKV0, _KR0, _PB_COLS = 1536, 1920, 2176, 2304


def _proj_wm_kernel(x_ref, sh_ref, sc_ref, g_ref, cos_ref, sin_ref, w_ref, wuq_ref, wukv_ref,
                    qn_ref, kvn_ref, wqt_ref, wk_ref, wvt_ref, mqt_ref, mk_ref, mvt_ref):
    hm = _norm_mod(x_ref[0], g_ref[...], sh_ref[0], sc_ref[0]).astype(BF16)
    tm = hm.shape[0]
    cos = cos_ref[...]
    sin = sin_ref[...]
    even, lo = _rope_masks(tm)
    half = LANES // 2
    p = _dot(hm, w_ref[...])

    def two_heads(blk):
        return jnp.where(lo, blk, 0.0), jnp.where(lo, pltpu.roll(blk, half, 1), 0.0)

    for j in range(WIN_Q_HEADS // 2):
        blk = _rope(p[:, _WQ0 + j * LANES:_WQ0 + (j + 1) * LANES], cos, sin, even) * (WIN_SCALE * LOG2E)
        for sub, padded in enumerate(two_heads(blk)):
            kvh, grp = divmod(2 * j + sub, WIN_GROUP)
            qt = padded.T.astype(BF16)
            for qb in range(tm // QBLK):
                wqt_ref[0, kvh, qb, :, grp * QBLK:(grp + 1) * QBLK] = qt[:, qb * QBLK:(qb + 1) * QBLK]
    for j in range(WIN_KV_HEADS // 2):
        kblk = two_heads(_rope(p[:, _WK0 + j * LANES:_WK0 + (j + 1) * LANES], cos, sin, even))
        vblk = two_heads(p[:, _WV0 + j * LANES:_WV0 + (j + 1) * LANES])
        for sub in range(2):
            kvh = 2 * j + sub
            wk_ref[0, :, kvh * LANES:(kvh + 1) * LANES] = kblk[sub].astype(BF16)
            wvt_ref[0, kvh, 0:LANES, :] = vblk[sub].T.astype(BF16)
            row = lax.broadcasted_iota(jnp.int32, (V_ROWS - LANES, tm), 0)
            wvt_ref[0, kvh, LANES:V_ROWS, :] = jnp.where(row == 0, 1.0, 0.0).astype(BF16)

    cq = _rms(p[:, _CQ0:_CQ0 + MLA_Q_RANK], qn_ref[...]).astype(BF16)
    q2 = _dot(cq, wuq_ref[...])
    ckv = _rms(p[:, _CKV0:_CKV0 + MLA_KV_RANK], kvn_ref[...]).astype(BF16)
    kv = _dot(ckv, wukv_ref[...])
    kr = _rope(p[:, _KR0:_KR0 + LANES], cos, sin, even).astype(BF16)
    for h in range(MLA_HEADS):
        c0 = 2 * h * LANES
        qn = q2[:, c0:c0 + LANES] * (MLA_SCALE * LOG2E)
        qr = _rope(q2[:, c0 + LANES:c0 + 2 * LANES], cos, sin, even) * (MLA_SCALE * LOG2E)
        mqt_ref[0, h, 0, 0:LANES, :] = qn.T.astype(BF16)
        mqt_ref[0, h, 0, LANES:2 * LANES, :] = qr.T.astype(BF16)
        mk_ref[0, h, :, 0:LANES] = kv[:, c0:c0 + LANES].astype(BF16)
        mk_ref[0, h, :, LANES:2 * LANES] = kr
        _store_vt(mvt_ref, h, kv[:, c0 + LANES:c0 + 2 * LANES])


def _proj_wm(x, shift, scale, g, cos, sin, w, wuq, wukv, qn, kvn):
    bsz, n, d = x.shape
    tm = min(FLASH_TQ, n)
    tok = lambda c: pl.BlockSpec((1, tm, c), lambda b, i: (b, i, 0))
    tab = pl.BlockSpec((tm, LANES), lambda b, i: (i, 0))
    row = lambda c: pl.BlockSpec((1, c), lambda b, i: (0, 0))
    kv_specs, kv_shapes = _kv_specs(bsz, n, tm, MLA_HEADS, 2 * LANES)
    qrow = WIN_GROUP * QBLK
    win_specs = [pl.BlockSpec((1, WIN_KV_HEADS, tm // QBLK, LANES, qrow), lambda b, i: (b, 0, i, 0, 0)),
                 tok(WIN_KV_HEADS * LANES),
                 pl.BlockSpec((1, WIN_KV_HEADS, V_ROWS, tm), lambda b, i: (b, 0, 0, i))]
    win_shapes = [jax.ShapeDtypeStruct((bsz, WIN_KV_HEADS, n // QBLK, LANES, qrow), BF16),
                  jax.ShapeDtypeStruct((bsz, n, WIN_KV_HEADS * LANES), BF16),
                  jax.ShapeDtypeStruct((bsz, WIN_KV_HEADS, V_ROWS, n), BF16)]
    return pl.pallas_call(
        _proj_wm_kernel,
        grid=(bsz, n // tm),
        in_specs=[tok(d), _mod_spec(shift), _mod_spec(scale), row(d), tab, tab,
                  _resident(w.shape), _resident(wuq.shape), _resident(wukv.shape),
                  row(MLA_Q_RANK), row(MLA_KV_RANK)],
        out_specs=win_specs
        + [pl.BlockSpec((1, MLA_HEADS, 1, 2 * LANES, tm), lambda b, i: (b, 0, i, 0, 0))] + kv_specs,
        out_shape=win_shapes
        + [jax.ShapeDtypeStruct((bsz, MLA_HEADS, n // tm, 2 * LANES, tm), BF16)] + kv_shapes,
        compiler_params=_cparams(("parallel", "parallel")),
        name="proj_wm",
    )(x, shift, scale, g, cos, sin, w, wuq, wukv, qn, kvn)


def _segments(seg_refs):
    segs = []
    for k_ref, vt_ref in zip(seg_refs[0::2], seg_refs[1::2]):
        segs.append((k_ref, vt_ref, vt_ref.shape[2], vt_ref.shape[4]))
    return segs


def _flash_loop_exact(qt, segs, m_sc, acc_sc):
    m_sc[...] = jnp.full(m_sc.shape, NEG_INF, F32)
    acc_sc[...] = jnp.zeros(acc_sc.shape, F32)
    for k_ref, vt_ref, nc, tk in segs:
        def body(j, carry, k_ref=k_ref, vt_ref=vt_ref, tk=tk):
            start = pl.multiple_of(j * tk, tk)
            s = _dot(k_ref[0, 0, pl.ds(start, tk), :], qt)
            m_prev = m_sc[...]
            m_new = jnp.maximum(m_prev, jnp.max(s, axis=0, keepdims=True))
            p = jnp.exp2(s - m_new)
            acc_sc[...] = jnp.exp2(m_prev - m_new) * acc_sc[...] + _dot(vt_ref[0, 0, j], p.astype(BF16))
            m_sc[...] = m_new
            return carry
        lax.fori_loop(0, nc, body, 0)


def _flash_loop_spec(qt, segs, m_sc, acc_sc, jump_sc):
    probe = _dot(segs[0][0][0, 0, 0:SPEC_PROBE_ROWS, :], qt)
    m_sc[...] = jnp.max(probe, axis=0, keepdims=True)
    acc_sc[...] = jnp.zeros(acc_sc.shape, F32)
    jump_sc[...] = jnp.zeros(jump_sc.shape, F32)
    chunks = [(k_ref, vt_ref, j, tk) for k_ref, vt_ref, nc, tk in segs for j in range(nc)]

    def scores(chunk):
        k_ref, _, j, tk = chunk
        return _dot(k_ref[0, 0, j * tk:(j + 1) * tk, :], qt)

    s = scores(chunks[0])
    for idx, (_, vt_ref, j, _) in enumerate(chunks):
        m_prev = m_sc[...]
        cmax = jnp.max(s, axis=0, keepdims=True)
        p = jnp.exp2(s - m_prev).astype(BF16)
        if idx + 1 < len(chunks):
            s = scores(chunks[idx + 1])
        pv = _dot(vt_ref[0, 0, j], p)
        m_new = jnp.maximum(m_prev, cmax)
        acc_sc[...] = (acc_sc[...] + pv) * jnp.exp2(m_prev - m_new)
        jump_sc[...] = jnp.maximum(jump_sc[...], cmax - m_prev)
        m_sc[...] = m_new


def _flash_tile(qt, segs, m_sc, acc_sc, jump_sc):
    _flash_loop_spec(qt, segs, m_sc, acc_sc, jump_sc)
    worst = jnp.max(jump_sc[...])

    @pl.when(jnp.logical_not(worst <= SPEC_JUMP_LIMIT))
    def _redo():
        _flash_loop_exact(qt, segs, m_sc, acc_sc)


def _diff_flash_kernel(*refs, nseg, lambda_init):
    qt_ref, seg_refs = refs[0], refs[1:1 + 2 * nseg]
    dl_ref, g_ref, o_ref, m_sc, acc_sc, jump_sc = refs[1 + 2 * nseg:]
    _flash_tile(qt_ref[0, 0, 0], _segments(seg_refs), m_sc, acc_sc, jump_sc)
    tq = o_ref.shape[1]
    o = acc_sc[0:LANES, :] / acc_sc[LANES:LANES + 1, :]
    dl = dl_ref[...]
    lam = (jnp.exp(jnp.sum(dl[0:1] * dl[1:2], axis=1, keepdims=True))
           - jnp.exp(jnp.sum(dl[2:3] * dl[3:4], axis=1, keepdims=True)) + lambda_init)
    y = (o[:, :tq] - lam * o[:, tq:]).T
    o_ref[0] = (_rms(y, g_ref[...]) * (1.0 - lambda_init)).astype(BF16)


def _mla_flash_kernel(*refs, nseg):
    qt_ref, seg_refs = refs[0], refs[1:1 + 2 * nseg]
    o_ref, m_sc, acc_sc, jump_sc = refs[1 + 2 * nseg:]
    _flash_tile(qt_ref[0, 0, 0], _segments(seg_refs), m_sc, acc_sc, jump_sc)
    o_ref[0] = (acc_sc[0:LANES, :] / acc_sc[LANES:LANES + 1, :]).T.astype(BF16)


def _flash(mode, qt, kv_segs, extra=(), lambda_init=0.0):
    bsz, heads, nq, dk, w = qt.shape
    maps = 2 if mode == "diff" else 1
    tq = w // maps
    in_specs = [pl.BlockSpec((1, 1, 1, dk, w), lambda b, h, i: (b, h, i, 0, 0))]
    args = [qt]
    for k, vt in kv_segs:
        in_specs += [pl.BlockSpec((1, 1) + k.shape[2:], lambda b, h, i: (b, h, 0, 0)),
                     pl.BlockSpec((1, 1) + vt.shape[2:], lambda b, h, i: (b, h, 0, 0, 0))]
        args += [k, vt]
    if mode == "diff":
        kern = functools.partial(_diff_flash_kernel, nseg=len(kv_segs), lambda_init=lambda_init)
        in_specs += [pl.BlockSpec(e.shape, lambda b, h, i: (0, 0)) for e in extra]
    else:
        kern = functools.partial(_mla_flash_kernel, nseg=len(kv_segs))
    return pl.pallas_call(
        kern,
        grid=(bsz, heads, nq),
        in_specs=in_specs,
        out_specs=pl.BlockSpec((1, tq, LANES), lambda b, h, i: (b, i, h)),
        out_shape=jax.ShapeDtypeStruct((bsz, nq * tq, heads * LANES), BF16),
        scratch_shapes=[pltpu.VMEM((1, w), F32), pltpu.VMEM((V_ROWS, w), F32),
                        pltpu.VMEM((1, w), F32)],
        compiler_params=_cparams(("parallel", "parallel", "arbitrary")),
        name=f"{mode}_flash",
    )(*args, *extra)


def _win_kernel(*refs, band, nblk):
    if band:
        (q_ref, kp_ref, ko_ref, kn_ref, vp_ref, vo_ref, vn_ref, kc_ref, vc_ref,
         sink_ref, o_ref) = refs
    else:
        q_ref, kc_ref, vc_ref, sink_ref, o_ref = refs
    t = QBLK
    w = WIN_GROUP * t
    if band:
        i = pl.program_id(1)
        kk = lax.broadcasted_iota(jnp.int32, (t, w), 0)
        qq = lax.broadcasted_iota(jnp.int32, (t, w), 1) % t
        ok_prev = kk >= qq + jnp.where(i > 0, 0, t)
        ok_next = kk <= qq - jnp.where(i < nblk - 1, 0, t)
    parts, sinks, maxes = [], [], []
    for kvh in range(WIN_KV_HEADS):
        ksl = slice(kvh * LANES, (kvh + 1) * LANES)
        h0 = kvh * WIN_GROUP
        qt = q_ref[0, kvh, 0]
        sink = jnp.concatenate([sink_ref[h0 + g:h0 + g + 1, :] for g in range(WIN_GROUP)],
                               axis=1) * LOG2E
        s_all = [_dot(kc_ref[0, :, ksl], qt)]
        if band:
            s_all = [jnp.where(ok_prev, _dot(kp_ref[0, :, ksl], qt), NEG_INF),
                     _dot(ko_ref[0, :, ksl], qt),
                     jnp.where(ok_next, _dot(kn_ref[0, :, ksl], qt), NEG_INF)] + s_all
        m = sink
        for part in s_all:
            m = jnp.maximum(m, jnp.max(part, axis=0, keepdims=True))
        parts.append(s_all)
        sinks.append(sink)
        maxes.append(m)
    outs = []
    for kvh in range(WIN_KV_HEADS):
        m, sink = maxes[kvh], sinks[kvh]
        p = [jnp.exp2(part - m).astype(BF16) for part in parts[kvh]]
        if band:
            p = jnp.concatenate(p, axis=0)
            vt = jnp.concatenate([vp_ref[0, kvh], vo_ref[0, kvh], vn_ref[0, kvh], vc_ref[0, kvh]],
                                 axis=1)
        else:
            p, vt = p[0], vc_ref[0, kvh]
        pv = _dot(vt, p)
        outs.append(pv[0:LANES, :] / (pv[LANES:LANES + 1, :] + jnp.exp2(sink - m)))
    for kvh in range(WIN_KV_HEADS):
        o = outs[kvh]
        for pair in range(WIN_GROUP // 2):
            a = o[:, (2 * pair) * t:(2 * pair + 1) * t].T
            b = o[:, (2 * pair + 1) * t:(2 * pair + 2) * t].T
            blk = kvh * (WIN_GROUP // 2) + pair
            o_ref[0, :, blk * LANES:(blk + 1) * LANES] = (a + pltpu.roll(b, LANES // 2, 1)).astype(BF16)


def _win_attn(qt, k, vt, kc, vtc, sink, band):
    bsz, _, nblk, _, qrow = qt.shape
    kw = kc.shape[2]
    nctx = kc.shape[1]
    t = QBLK
    out_w = WIN_Q_HEADS * WIN_HEAD_DIM
    own = lambda b, i: (b, i, 0)
    in_specs = [pl.BlockSpec((1, WIN_KV_HEADS, 1, LANES, qrow), lambda b, i: (b, 0, i, 0, 0))]
    args = [qt]
    if band:
        prev = lambda b, i: (b, jnp.maximum(i - 1, 0), 0)
        nxt = lambda b, i: (b, jnp.minimum(i + 1, nblk - 1), 0)
        kspec = lambda f: pl.BlockSpec((1, t, kw), f)
        vspec = lambda f: pl.BlockSpec((1, WIN_KV_HEADS, V_ROWS, t),
                                       lambda b, i, f=f: (b, 0, 0, f(b, i)[1]))
        in_specs += [kspec(prev), kspec(own), kspec(nxt), vspec(prev), vspec(own), vspec(nxt)]
        args += [k, k, k, vt, vt, vt]
    in_specs += [pl.BlockSpec((1, nctx, kw), lambda b, i: (b, 0, 0)),
                 pl.BlockSpec((1, WIN_KV_HEADS, V_ROWS, nctx), lambda b, i: (b, 0, 0, 0)),
                 pl.BlockSpec(sink.shape, lambda b, i: (0, 0))]
    args += [kc, vtc, sink]
    return pl.pallas_call(
        functools.partial(_win_kernel, band=band, nblk=nblk),
        grid=(bsz, nblk),
        in_specs=in_specs,
        out_specs=pl.BlockSpec((1, t, out_w), own),
        out_shape=jax.ShapeDtypeStruct((bsz, nblk * t, out_w), BF16),
        compiler_params=_cparams(("parallel", "parallel")),
        name="win_attn" if band else "win_attn_ctx",
    )(*args)


def _merge_kernel(x_ref, ya_ref, yb_ref, yc_ref, sh_ref, sc_ref, gt_ref, g2_ref, g3_ref,
                  wg_ref, bw_ref, wo_ref, o_ref):
    x = x_ref[0]
    d = x.shape[1]
    hm = _norm_mod(x, g2_ref[...], sh_ref[0], sc_ref[0]).astype(BF16)
    merged = jnp.zeros(x.shape, F32)
    for i, y_ref in enumerate((ya_ref, yb_ref, yc_ref)):
        gate = jax.nn.sigmoid(_dot(hm, wg_ref[:, i * d:(i + 1) * d]))
        merged = merged + gate * _dot(y_ref[0], bw_ref[i])
    y = _dot(merged.astype(BF16), wo_ref[...])
    o_ref[0] = x + gt_ref[0] * _rms(y, g3_ref[...])


def _merge(x, ya, yb, yc, shift, scale, gate, g2, g3, wg, bw, wo):
    bsz, n, d = x.shape
    tm = min(512, n)
    tok = pl.BlockSpec((1, tm, d), lambda b, i: (b, i, 0))
    gain = pl.BlockSpec((1, d), lambda b, i: (0, 0))
    return pl.pallas_call(
        _merge_kernel,
        grid=(bsz, n // tm),
        in_specs=[tok, tok, tok, tok, _mod_spec(shift), _mod_spec(scale), _mod_spec(gate),
                  gain, gain, _resident(wg.shape), _resident(bw.shape), _resident(wo.shape)],
        out_specs=tok,
        out_shape=jax.ShapeDtypeStruct(x.shape, F32),
        compiler_params=_cparams(("parallel", "parallel")),
        name="merge",
    )(x, ya, yb, yc, shift, scale, gate, g2, g3, wg, bw, wo)


def _rope_tables(n):
    quarter = ROPE_DIM // 4
    pos = jnp.arange(n)
    row = (pos // GRID_W).astype(F32)
    col = (pos % GRID_W).astype(F32)
    inv_freq = 1.0 / (ROPE_BASE ** (jnp.arange(quarter, dtype=F32) / quarter))
    ang_r = row[:, None] * inv_freq
    ang_c = col[:, None] * inv_freq
    cos = jnp.concatenate([jnp.cos(ang_r)] * 2 + [jnp.cos(ang_c)] * 2, axis=1)
    sin = jnp.concatenate([-jnp.sin(ang_r), jnp.sin(ang_r), -jnp.sin(ang_c), jnp.sin(ang_c)], axis=1)
    return jnp.tile(cos, (1, 2)), jnp.tile(sin, (1, 2))


def _layer_weights(l, ffn_w_in, ffn_w_out, mix_w_in, mla_w_uq, mla_w_ukv, branch_w, mix_w_out):
    d = mix_w_in.shape[1]
    w = mix_w_in[l]
    hw = DIFF_HEADS * 2 * DIFF_D
    wq_n = WIN_Q_HEADS * WIN_HEAD_DIM
    wk_n = WIN_KV_HEADS * WIN_HEAD_DIM
    c = 3 * hw
    w_diff = w[:, :c]
    rest_n = wq_n + 2 * wk_n + MLA_Q_RANK + MLA_KV_RANK + MLA_ROPE
    w_wm = jnp.concatenate([w[:, c:c + rest_n], jnp.zeros((d, _PB_COLS - rest_n), w.dtype)], axis=1)
    w_gate = w[:, c + rest_n:]
    wuq = mla_w_uq[l].reshape(MLA_Q_RANK, MLA_HEADS, MLA_NOPE + MLA_ROPE)
    wuq = jnp.pad(wuq, ((0, 0), (0, 0), (0, 2 * LANES - MLA_NOPE - MLA_ROPE)))
    wuq = wuq.reshape(MLA_Q_RANK, MLA_HEADS * 2 * LANES)
    return dict(
        ffn_in=[ffn_w_in[l, i].astype(BF16) for i in range(2)],
        ffn_out=[ffn_w_out[l, i].astype(BF16) for i in range(2)],
        w_diff=w_diff.astype(BF16), w_wm=w_wm.astype(BF16), w_gate=w_gate.astype(BF16),
        wuq=wuq.astype(BF16), wukv=mla_w_ukv[l].astype(BF16),
        bw=branch_w[l].astype(BF16), wo=mix_w_out[l].astype(BF16))


def kernel(x, c, ctx, c_ctx, ada_w, ada_b, norm_g, ffn_w_in, ffn_w_out, mix_w_in, diff_lambda,
           diff_subln_g, win_sink, mla_q_norm_g, mla_kv_norm_g, mla_w_uq, mla_w_ukv, branch_w,
           mix_w_out):
    bsz, s, d = x.shape
    nctx = ctx.shape[1]
    depth = ada_w.shape[0]

    cvecs = jnp.zeros((8, d), F32).at[:bsz].set(c).at[bsz].set(c_ctx)
    mods = _ada_mods(cvecs, ada_w, ada_b).reshape(depth, 8, N_MOD, d)

    cos_x, sin_x = _rope_tables(s)
    cos_c = jnp.ones((nctx, LANES), F32)
    sin_c = jnp.zeros((nctx, LANES), F32)

    h = ctx
    for l in range(depth):
        last = l == depth - 1
        lambda_init = 0.8 - 0.6 * math.exp(-0.3 * l)
        wts = _layer_weights(l, ffn_w_in, ffn_w_out, mix_w_in, mla_w_uq, mla_w_ukv, branch_w,
                             mix_w_out)
        mx = [mods[l, :bsz, k][:, None, :] for k in range(N_MOD)]
        mc = [mods[l, bsz:bsz + 1, k][:, None, :] for k in range(N_MOD)]
        g = [norm_g[l, k][None, :] for k in range(6)]
        qn = mla_q_norm_g[l][None, :]
        kvn = mla_kv_norm_g[l][None, :]
        dl = diff_lambda[l].astype(F32)
        subln = diff_subln_g[l][None, :]
        sink = jnp.broadcast_to(win_sink[l].astype(F32)[:, None], (WIN_Q_HEADS, LANES))

        x = _ffn(x, mx[0], mx[1], mx[2], g[0], g[1], wts["ffn_in"][0], wts["ffn_out"][0])
        h = _ffn(h, mc[0], mc[1], mc[2], g[0], g[1], wts["ffn_in"][0], wts["ffn_out"][0])

        dqt_x, dk_x, dvt_x = _proj_diff(x, mx[3], mx[4], g[2], cos_x, sin_x, wts["w_diff"])
        dqt_c, dk_c, dvt_c = _proj_diff(h, mc[3], mc[4], g[2], cos_c, sin_c, wts["w_diff"])
        wqt_x, wk_x, wvt_x, mqt_x, mk_x, mvt_x = _proj_wm(
            x, mx[3], mx[4], g[2], cos_x, sin_x, wts["w_wm"], wts["wuq"], wts["wukv"], qn, kvn)
        wqt_c, wk_c, wvt_c, mqt_c, mk_c, mvt_c = _proj_wm(
            h, mc[3], mc[4], g[2], cos_c, sin_c, wts["w_wm"], wts["wuq"], wts["wukv"], qn, kvn)

        ya = _flash("diff", dqt_x, [(dk_x, dvt_x), (dk_c, dvt_c)], (dl, subln), lambda_init)
        yb = _win_attn(wqt_x, wk_x, wvt_x, wk_c, wvt_c, sink, band=True)
        yc = _flash("mla", mqt_x, [(mk_x, mvt_x), (mk_c, mvt_c)])
        x_new = _merge(x, ya, yb, yc, mx[3], mx[4], mx[5], g[2], g[3],
                       wts["w_gate"], wts["bw"], wts["wo"])
        if not last:
            ca = _flash("diff", dqt_c, [(dk_c, dvt_c)], (dl, subln), lambda_init)
            cb = _win_attn(wqt_c, None, None, wk_c, wvt_c, sink, band=False)
            cc = _flash("mla", mqt_c, [(mk_c, mvt_c)])
            h = _merge(h, ca, cb, cc, mc[3], mc[4], mc[5], g[2], g[3],
                       wts["w_gate"], wts["bw"], wts["wo"])
            h = _ffn(h, mc[6], mc[7], mc[8], g[4], g[5], wts["ffn_in"][1], wts["ffn_out"][1])
        x = _ffn(x_new, mx[6], mx[7], mx[8], g[4], g[5], wts["ffn_in"][1], wts["ffn_out"][1])
    return x
```

```python
import functools
import math

import jax
import jax.numpy as jnp
from jax import lax
from jax.experimental import pallas as pl
from jax.experimental.pallas import tpu as pltpu

F32 = jnp.float32
BF16 = jnp.bfloat16

GRID_W = 64
QBLK = 128
EPS = 1e-6
NEG_INF = -1e30
ROPE_DIM = 64
ROPE_BASE = 10000.0
N_MOD = 9
FFN_RES = 0.5
DIFF_HEADS = 8
DIFF_D = 64
WIN_Q_HEADS = 16
WIN_KV_HEADS = 4
WIN_GROUP = 4
WIN_HEAD_DIM = 64
WIN_SCALE = WIN_HEAD_DIM ** -0.5
MLA_HEADS = 8
MLA_Q_RANK = 384
MLA_KV_RANK = 256
MLA_NOPE = 128
MLA_ROPE = 64
MLA_V = 128
MLA_SCALE = (MLA_NOPE + MLA_ROPE) ** -0.5
LOG2E = math.log2(math.e)

LANES = 128
VMEM_LIMIT = 56 * 1024 * 1024

FLASH_TQ = 512
MLA_TILES_PER_STEP = 2
KV_CHUNK = 2048
V_ROWS = LANES + 16
SPEC_JUMP_LIMIT = 64.0
SPEC_PROBE_ROWS = 128


def _cparams(sem):
    return pltpu.CompilerParams(dimension_semantics=sem, vmem_limit_bytes=VMEM_LIMIT)


def _resident(shape):
    nd = len(shape)
    return pl.BlockSpec(shape, lambda *_: (0,) * nd, pipeline_mode=pl.Buffered(1))


def _rms(x, g):
    return x * lax.rsqrt(jnp.mean(x * x, axis=-1, keepdims=True) + EPS) * g


def _norm_mod(x, g, shift, scale):
    return _rms(x, g) * (1.0 + scale) + shift


def _dot(a, b):
    return jnp.dot(a, b, preferred_element_type=F32)


def _dot_nt(a, b):
    return lax.dot_general(a, b, (((1,), (1,)), ((), ())), preferred_element_type=F32)


def _mod_spec(arr):
    d = arr.shape[-1]
    if arr.shape[0] == 1:
        return pl.BlockSpec((1, 1, d), lambda b, *_: (0, 0, 0))
    return pl.BlockSpec((1, 1, d), lambda b, *_: (b, 0, 0))


def _ada_kernel(c_ref, w_ref, b_ref, o_ref):
    c = c_ref[...]
    a = c * jax.nn.sigmoid(c)
    o_ref[0] = jnp.dot(a, w_ref[0], preferred_element_type=F32,
                       precision=lax.Precision.HIGHEST) + b_ref[0]


def _ada_mods(cvecs, ada_w, ada_b):
    depth, d, nd = ada_w.shape
    rows = cvecs.shape[0]
    tn = 1152 if nd % 1152 == 0 else nd
    return pl.pallas_call(
        _ada_kernel,
        grid=(depth, nd // tn),
        in_specs=[pl.BlockSpec((rows, d), lambda l, j: (0, 0)),
                  pl.BlockSpec((1, d, tn), lambda l, j: (l, 0, j)),
                  pl.BlockSpec((1, 1, tn), lambda l, j: (l, 0, j))],
        out_specs=pl.BlockSpec((1, rows, tn), lambda l, j: (l, 0, j)),
        out_shape=jax.ShapeDtypeStruct((depth, rows, nd), F32),
        compiler_params=_cparams(("parallel", "parallel")),
        name="ada_mods",
    )(cvecs, ada_w, ada_b.reshape(depth, 1, nd))


def _ffn_kernel(x_ref, sh_ref, sc_ref, gt_ref, gpre_ref, gpost_ref, win_ref, wout_ref, o_ref,
                *, ffn_dim, chunk):
    x = x_ref[0]
    xm = _norm_mod(x, gpre_ref[...], sh_ref[0], sc_ref[0]).astype(BF16)
    acc = jnp.zeros(x.shape, F32)
    for c in range(ffn_dim // chunk):
        a = _dot(xm, win_ref[:, c * chunk:(c + 1) * chunk])
        b = _dot(xm, win_ref[:, ffn_dim + c * chunk:ffn_dim + (c + 1) * chunk])
        h = (a * jax.nn.sigmoid(a) * b).astype(BF16)
        acc = acc + _dot(h, wout_ref[c * chunk:(c + 1) * chunk, :])
    o_ref[0] = x + FFN_RES * gt_ref[0] * _rms(acc, gpost_ref[...])


def _ffn(x, shift, scale, gate, g_pre, g_post, w_in, w_out):
    bsz, n, d = x.shape
    ffn_dim = w_out.shape[0]
    tm = min(512, n)
    tok = pl.BlockSpec((1, tm, d), lambda b, i: (b, i, 0))
    gain = pl.BlockSpec((1, d), lambda b, i: (0, 0))
    return pl.pallas_call(
        functools.partial(_ffn_kernel, ffn_dim=ffn_dim, chunk=256),
        grid=(bsz, n // tm),
        in_specs=[tok, _mod_spec(shift), _mod_spec(scale), _mod_spec(gate), gain, gain,
                  _resident(w_in.shape), _resident(w_out.shape)],
        out_specs=tok,
        out_shape=jax.ShapeDtypeStruct(x.shape, F32),
        compiler_params=_cparams(("parallel", "parallel")),
        name="ffn",
    )(x, shift, scale, gate, g_pre, g_post, w_in, w_out)


def _rope_masks(tm):
    lane = lax.broadcasted_iota(jnp.int32, (tm, LANES), 1)
    even = ((lane // (ROPE_DIM // 4)) & 1) == 0
    lo = lane < (LANES // 2)
    return even, lo


def _rope(blk, cos, sin, even):
    q = ROPE_DIM // 4
    partner = jnp.where(even, pltpu.roll(blk, LANES - q, 1), pltpu.roll(blk, q, 1))
    return blk * cos + partner * sin


def _store_vt(vt_ref, h, v_blk):
    tm = v_blk.shape[0]
    vt_ref[0, h, 0, 0:LANES, :] = v_blk.T.astype(BF16)
    row = lax.broadcasted_iota(jnp.int32, (V_ROWS - LANES, tm), 0)
    vt_ref[0, h, 0, LANES:V_ROWS, :] = jnp.where(row == 0, 1.0, 0.0).astype(BF16)


def _kv_specs(bsz, n, tm, heads, dk):
    chunk = min(KV_CHUNK, n)
    per = chunk // tm
    specs = [pl.BlockSpec((1, heads, tm, dk), lambda b, i: (b, 0, i, 0)),
             pl.BlockSpec((1, heads, 1, V_ROWS, tm), lambda b, i: (b, 0, i // per, 0, i % per))]
    shapes = [jax.ShapeDtypeStruct((bsz, heads, n, dk), BF16),
              jax.ShapeDtypeStruct((bsz, heads, n // chunk, V_ROWS, chunk), BF16)]
    return specs, shapes


def _proj_diff_kernel(x_ref, sh_ref, sc_ref, g_ref, cos_ref, sin_ref, w_ref,
                      qt_ref, k_ref, vt_ref):
    hm = _norm_mod(x_ref[0], g_ref[...], sh_ref[0], sc_ref[0]).astype(BF16)
    tm = hm.shape[0]
    cos = cos_ref[...]
    sin = sin_ref[...]
    even, _ = _rope_masks(tm)
    hw = DIFF_HEADS * 2 * DIFF_D
    q = _dot(hm, w_ref[:, 0:hw])
    k = _dot(hm, w_ref[:, hw:2 * hw])
    v = _dot(hm, w_ref[:, 2 * hw:3 * hw])
    scale = LOG2E / math.sqrt(DIFF_D)
    top = lax.broadcasted_iota(jnp.int32, (LANES, tm), 0) < DIFF_D
    for h in range(DIFF_HEADS):
        hs = slice(h * LANES, (h + 1) * LANES)
        qbt = (_rope(q[:, hs], cos, sin, even) * scale).T
        qt_ref[0, h, 0, :, 0:tm] = jnp.where(top, qbt, 0.0).astype(BF16)
        qt_ref[0, h, 0, :, tm:2 * tm] = jnp.where(top, 0.0, qbt).astype(BF16)
        k_ref[0, h] = _rope(k[:, hs], cos, sin, even).astype(BF16)
        _store_vt(vt_ref, h, v[:, hs])


def _proj_diff(x, shift, scale, g, cos, sin, w):
    bsz, n, d = x.shape
    tm = min(FLASH_TQ, n)
    tok = pl.BlockSpec((1, tm, d), lambda b, i: (b, i, 0))
    tab = pl.BlockSpec((tm, LANES), lambda b, i: (i, 0))
    kv_specs, kv_shapes = _kv_specs(bsz, n, tm, DIFF_HEADS, LANES)
    return pl.pallas_call(
        _proj_diff_kernel,
        grid=(bsz, n // tm),
        in_specs=[tok, _mod_spec(shift), _mod_spec(scale),
                  pl.BlockSpec((1, d), lambda b, i: (0, 0)), tab, tab, _resident(w.shape)],
        out_specs=[pl.BlockSpec((1, DIFF_HEADS, 1, LANES, 2 * tm), lambda b, i: (b, 0, i, 0, 0))]
        + kv_specs,
        out_shape=[jax.ShapeDtypeStruct((bsz, DIFF_HEADS, n // tm, LANES, 2 * tm), BF16)] + kv_shapes,
        compiler_params=_cparams(("parallel", "parallel")),
        name="proj_diff",
    )(x, shift, scale, g, cos, sin, w)


_WQ0, _WK0, _WV0 = 0, 1024, 1280
_CQ0, _CKV0, _KR0, _PB_COLS = 1536, 1920, 2176, 2304


def _proj_wm_kernel(x_ref, sh_ref, sc_ref, g_ref, cos_ref, sin_ref, w_ref, wuq_ref, wukv_ref,
                    qn_ref, kvn_ref, wqt_ref, wk_ref, wvt_ref, mqt_ref, mk_ref, mvt_ref):
    hm = _norm_mod(x_ref[0], g_ref[...], sh_ref[0], sc_ref[0]).astype(BF16)
    tm = hm.shape[0]
    cos = cos_ref[...]
    sin = sin_ref[...]
    even, lo = _rope_masks(tm)
    half = LANES // 2
    p = _dot(hm, w_ref[...])

    def two_heads(blk):
        return jnp.where(lo, blk, 0.0), jnp.where(lo, pltpu.roll(blk, half, 1), 0.0)

    for j in range(WIN_Q_HEADS // 2):
        blk = _rope(p[:, _WQ0 + j * LANES:_WQ0 + (j + 1) * LANES], cos, sin, even) * (WIN_SCALE * LOG2E)
        for sub, padded in enumerate(two_heads(blk)):
            kvh, grp = divmod(2 * j + sub, WIN_GROUP)
            qt = padded.T.astype(BF16)
            for qb in range(tm // QBLK):
                wqt_ref[0, kvh, qb, :, grp * QBLK:(grp + 1) * QBLK] = qt[:, qb * QBLK:(qb + 1) * QBLK]
    for j in range(WIN_KV_HEADS // 2):
        kblk = two_heads(_rope(p[:, _WK0 + j * LANES:_WK0 + (j + 1) * LANES], cos, sin, even))
        vblk = two_heads(p[:, _WV0 + j * LANES:_WV0 + (j + 1) * LANES])
        for sub in range(2):
            kvh = 2 * j + sub
            wk_ref[0, :, kvh * LANES:(kvh + 1) * LANES] = kblk[sub].astype(BF16)
            wvt_ref[0, kvh, 0:LANES, :] = vblk[sub].T.astype(BF16)
            row = lax.broadcasted_iota(jnp.int32, (V_ROWS - LANES, tm), 0)
            wvt_ref[0, kvh, LANES:V_ROWS, :] = jnp.where(row == 0, 1.0, 0.0).astype(BF16)

    cq = _rms(p[:, _CQ0:_CQ0 + MLA_Q_RANK], qn_ref[...]).astype(BF16)
    q2 = _dot(cq, wuq_ref[...])
    ckv = _rms(p[:, _CKV0:_CKV0 + MLA_KV_RANK], kvn_ref[...]).astype(BF16)
    kv = _dot(ckv, wukv_ref[...])
    kr = _rope(p[:, _KR0:_KR0 + LANES], cos, sin, even).astype(BF16)
    for h in range(MLA_HEADS):
        c0 = 2 * h * LANES
        qn = q2[:, c0:c0 + LANES] * (MLA_SCALE * LOG2E)
        qr = _rope(q2[:, c0 + LANES:c0 + 2 * LANES], cos, sin, even) * (MLA_SCALE * LOG2E)
        mqt_ref[0, h, 0, 0:LANES, :] = qn.T.astype(BF16)
        mqt_ref[0, h, 0, LANES:2 * LANES, :] = qr.T.astype(BF16)
        mk_ref[0, h, :, 0:LANES] = kv[:, c0:c0 + LANES].astype(BF16)
        mk_ref[0, h, :, LANES:2 * LANES] = kr
        _store_vt(mvt_ref, h, kv[:, c0 + LANES:c0 + 2 * LANES])


def _proj_wm(x, shift, scale, g, cos, sin, w, wuq, wukv, qn, kvn):
    bsz, n, d = x.shape
    tm = min(FLASH_TQ, n)
    tok = lambda c: pl.BlockSpec((1, tm, c), lambda b, i: (b, i, 0))
    tab = pl.BlockSpec((tm, LANES), lambda b, i: (i, 0))
    row = lambda c: pl.BlockSpec((1, c), lambda b, i: (0, 0))
    kv_specs, kv_shapes = _kv_specs(bsz, n, tm, MLA_HEADS, 2 * LANES)
    qrow = WIN_GROUP * QBLK
    win_specs = [pl.BlockSpec((1, WIN_KV_HEADS, tm // QBLK, LANES, qrow), lambda b, i: (b, 0, i, 0, 0)),
                 tok(WIN_KV_HEADS * LANES),
                 pl.BlockSpec((1, WIN_KV_HEADS, V_ROWS, tm), lambda b, i: (b, 0, 0, i))]
    win_shapes = [jax.ShapeDtypeStruct((bsz, WIN_KV_HEADS, n // QBLK, LANES, qrow), BF16),
                  jax.ShapeDtypeStruct((bsz, n, WIN_KV_HEADS * LANES), BF16),
                  jax.ShapeDtypeStruct((bsz, WIN_KV_HEADS, V_ROWS, n), BF16)]
    return pl.pallas_call(
        _proj_wm_kernel,
        grid=(bsz, n // tm),
        in_specs=[tok(d), _mod_spec(shift), _mod_spec(scale), row(d), tab, tab,
                  _resident(w.shape), _resident(wuq.shape), _resident(wukv.shape),
                  row(MLA_Q_RANK), row(MLA_KV_RANK)],
        out_specs=win_specs
        + [pl.BlockSpec((1, MLA_HEADS, 1, 2 * LANES, tm), lambda b, i: (b, 0, i, 0, 0))] + kv_specs,
        out_shape=win_shapes
        + [jax.ShapeDtypeStruct((bsz, MLA_HEADS, n // tm, 2 * LANES, tm), BF16)] + kv_shapes,
        compiler_params=_cparams(("parallel", "parallel")),
        name="proj_wm",
    )(x, shift, scale, g, cos, sin, w, wuq, wukv, qn, kvn)


def _segments(seg_refs):
    segs = []
    for k_ref, vt_ref in zip(seg_refs[0::2], seg_refs[1::2]):
        segs.append((k_ref, vt_ref, vt_ref.shape[2], vt_ref.shape[4]))
    return segs


def _flash_loop_exact(qt, segs, m_sc, acc_sc):
    m_sc[...] = jnp.full(m_sc.shape, NEG_INF, F32)
    acc_sc[...] = jnp.zeros(acc_sc.shape, F32)
    for k_ref, vt_ref, nc, tk in segs:
        def body(j, carry, k_ref=k_ref, vt_ref=vt_ref, tk=tk):
            start = pl.multiple_of(j * tk, tk)
            s = _dot(k_ref[0, 0, pl.ds(start, tk), :], qt)
            m_prev = m_sc[...]
            m_new = jnp.maximum(m_prev, jnp.max(s, axis=0, keepdims=True))
            p = jnp.exp2(s - m_new)
            acc_sc[...] = jnp.exp2(m_prev - m_new) * acc_sc[...] + _dot(vt_ref[0, 0, j], p.astype(BF16))
            m_sc[...] = m_new
            return carry
        lax.fori_loop(0, nc, body, 0)


def _flash_loop_spec(qt, segs, m_sc, acc_sc, jump_sc):
    probe = _dot(segs[0][0][0, 0, 0:SPEC_PROBE_ROWS, :], qt)
    m_sc[...] = jnp.max(probe, axis=0, keepdims=True)
    acc_sc[...] = jnp.zeros(acc_sc.shape, F32)
    jump_sc[...] = jnp.zeros(jump_sc.shape, F32)
    chunks = [(k_ref, vt_ref, j, tk) for k_ref, vt_ref, nc, tk in segs for j in range(nc)]

    def scores(chunk):
        k_ref, _, j, tk = chunk
        return _dot(k_ref[0, 0, j * tk:(j + 1) * tk, :], qt)

    s = scores(chunks[0])
    for idx, (_, vt_ref, j, _) in enumerate(chunks):
        m_prev = m_sc[...]
        cmax = jnp.max(s, axis=0, keepdims=True)
        p = jnp.exp2(s - m_prev).astype(BF16)
        if idx + 1 < len(chunks):
            s = scores(chunks[idx + 1])
        pv = _dot(vt_ref[0, 0, j], p)
        m_new = jnp.maximum(m_prev, cmax)
        acc_sc[...] = (acc_sc[...] + pv) * jnp.exp2(m_prev - m_new)
        jump_sc[...] = jnp.maximum(jump_sc[...], cmax - m_prev)
        m_sc[...] = m_new


def _flash_tile(qt, segs, m_sc, acc_sc, jump_sc):
    _flash_loop_spec(qt, segs, m_sc, acc_sc, jump_sc)
    worst = jnp.max(jump_sc[...])

    @pl.when(jnp.logical_not(worst <= SPEC_JUMP_LIMIT))
    def _redo():
        _flash_loop_exact(qt, segs, m_sc, acc_sc)


def _diff_flash_kernel(*refs, nseg, lambda_init):
    qt_ref, seg_refs = refs[0], refs[1:1 + 2 * nseg]
    dl_ref, g_ref, o_ref, m_sc, acc_sc, jump_sc = refs[1 + 2 * nseg:]
    _flash_tile(qt_ref[0, 0, 0], _segments(seg_refs), m_sc, acc_sc, jump_sc)
    tq = o_ref.shape[1]
    o = acc_sc[0:LANES, :] / acc_sc[LANES:LANES + 1, :]
    dl = dl_ref[...]
    lam = (jnp.exp(jnp.sum(dl[0:1] * dl[1:2], axis=1, keepdims=True))
           - jnp.exp(jnp.sum(dl[2:3] * dl[3:4], axis=1, keepdims=True)) + lambda_init)
    y = (o[:, :tq] - lam * o[:, tq:]).T
    o_ref[0] = (_rms(y, g_ref[...]) * (1.0 - lambda_init)).astype(BF16)


def _mla_flash_kernel(*refs, nseg):
    qt_ref, seg_refs = refs[0], refs[1:1 + 2 * nseg]
    o_ref, m_sc, acc_sc, jump_sc = refs[1 + 2 * nseg:]
    qt = jnp.concatenate([qt_ref[0, 0, t] for t in range(qt_ref.shape[2])], axis=1)
    _flash_tile(qt, _segments(seg_refs), m_sc, acc_sc, jump_sc)
    o_ref[0] = (acc_sc[0:LANES, :] / acc_sc[LANES:LANES + 1, :]).T.astype(BF16)


def _flash(mode, qt, kv_segs, extra=(), lambda_init=0.0):
    bsz, heads, nq, dk, w = qt.shape
    maps = 2 if mode == "diff" else 1
    tps = MLA_TILES_PER_STEP if (mode == "mla" and nq % MLA_TILES_PER_STEP == 0) else 1
    tq = tps * w // maps
    nq //= tps
    in_specs = [pl.BlockSpec((1, 1, tps, dk, w), lambda b, h, i: (b, h, i, 0, 0))]
    w *= tps
    args = [qt]
    for k, vt in kv_segs:
        in_specs += [pl.BlockSpec((1, 1) + k.shape[2:], lambda b, h, i: (b, h, 0, 0)),
                     pl.BlockSpec((1, 1) + vt.shape[2:], lambda b, h, i: (b, h, 0, 0, 0))]
        args += [k, vt]
    if mode == "diff":
        kern = functools.partial(_diff_flash_kernel, nseg=len(kv_segs), lambda_init=lambda_init)
        in_specs += [pl.BlockSpec(e.shape, lambda b, h, i: (0, 0)) for e in extra]
    else:
        kern = functools.partial(_mla_flash_kernel, nseg=len(kv_segs))
    return pl.pallas_call(
        kern,
        grid=(bsz, heads, nq),
        in_specs=in_specs,
        out_specs=pl.BlockSpec((1, tq, LANES), lambda b, h, i: (b, i, h)),
        out_shape=jax.ShapeDtypeStruct((bsz, nq * tq, heads * LANES), BF16),
        scratch_shapes=[pltpu.VMEM((1, w), F32), pltpu.VMEM((V_ROWS, w), F32),
                        pltpu.VMEM((1, w), F32)],
        compiler_params=_cparams(("parallel", "parallel", "arbitrary")),
        name=f"{mode}_flash",
    )(*args, *extra)


def _win_kernel(*refs, band, nblk):
    if band:
        (q_ref, kp_ref, ko_ref, kn_ref, vp_ref, vo_ref, vn_ref, kc_ref, vc_ref,
         sink_ref, o_ref) = refs
    else:
        q_ref, kc_ref, vc_ref, sink_ref, o_ref = refs
    t = QBLK
    w = WIN_GROUP * t
    if band:
        i = pl.program_id(1)
        kk = lax.broadcasted_iota(jnp.int32, (t, w), 0)
        qq = lax.broadcasted_iota(jnp.int32, (t, w), 1) % t
        ok_prev = kk >= qq + jnp.where(i > 0, 0, t)
        ok_next = kk <= qq - jnp.where(i < nblk - 1, 0, t)
    parts, sinks, maxes = [], [], []
    for kvh in range(WIN_KV_HEADS):
        ksl = slice(kvh * LANES, (kvh + 1) * LANES)
        h0 = kvh * WIN_GROUP
        qt = q_ref[0, kvh, 0]
        sink = jnp.concatenate([sink_ref[h0 + g:h0 + g + 1, :] for g in range(WIN_GROUP)],
                               axis=1) * LOG2E
        s_all = [_dot(kc_ref[0, :, ksl], qt)]
        if band:
            s_all = [jnp.where(ok_prev, _dot(kp_ref[0, :, ksl], qt), NEG_INF),
                     _dot(ko_ref[0, :, ksl], qt),
                     jnp.where(ok_next, _dot(kn_ref[0, :, ksl], qt), NEG_INF)] + s_all
        m = sink
        for part in s_all:
            m = jnp.maximum(m, jnp.max(part, axis=0, keepdims=True))
        parts.append(s_all)
        sinks.append(sink)
        maxes.append(m)
    outs = []
    for kvh in range(WIN_KV_HEADS):
        m, sink = maxes[kvh], sinks[kvh]
        p = [jnp.exp2(part - m).astype(BF16) for part in parts[kvh]]
        if band:
            p = jnp.concatenate(p, axis=0)
            vt = jnp.concatenate([vp_ref[0, kvh], vo_ref[0, kvh], vn_ref[0, kvh], vc_ref[0, kvh]],
                                 axis=1)
        else:
            p, vt = p[0], vc_ref[0, kvh]
        pv = _dot(vt, p)
        outs.append(pv[0:LANES, :] / (pv[LANES:LANES + 1, :] + jnp.exp2(sink - m)))
    for kvh in range(WIN_KV_HEADS):
        o = outs[kvh]
        for pair in range(WIN_GROUP // 2):
            a = o[0:WIN_HEAD_DIM, (2 * pair) * t:(2 * pair + 1) * t]
            b = o[0:WIN_HEAD_DIM, (2 * pair + 1) * t:(2 * pair + 2) * t]
            blk = kvh * (WIN_GROUP // 2) + pair
            o_ref[0, :, blk * LANES:(blk + 1) * LANES] = jnp.concatenate([a, b], axis=0).T.astype(BF16)


def _win_attn(qt, k, vt, kc, vtc, sink, band):
    bsz, _, nblk, _, qrow = qt.shape
    kw = kc.shape[2]
    nctx = kc.shape[1]
    t = QBLK
    out_w = WIN_Q_HEADS * WIN_HEAD_DIM
    own = lambda b, i: (b, i, 0)
    in_specs = [pl.BlockSpec((1, WIN_KV_HEADS, 1, LANES, qrow), lambda b, i: (b, 0, i, 0, 0))]
    args = [qt]
    if band:
        prev = lambda b, i: (b, jnp.maximum(i - 1, 0), 0)
        nxt = lambda b, i: (b, jnp.minimum(i + 1, nblk - 1), 0)
        kspec = lambda f: pl.BlockSpec((1, t, kw), f)
        vspec = lambda f: pl.BlockSpec((1, WIN_KV_HEADS, V_ROWS, t),
                                       lambda b, i, f=f: (b, 0, 0, f(b, i)[1]))
        in_specs += [kspec(prev), kspec(own), kspec(nxt), vspec(prev), vspec(own), vspec(nxt)]
        args += [k, k, k, vt, vt, vt]
    in_specs += [pl.BlockSpec((1, nctx, kw), lambda b, i: (b, 0, 0)),
                 pl.BlockSpec((1, WIN_KV_HEADS, V_ROWS, nctx), lambda b, i: (b, 0, 0, 0)),
                 pl.BlockSpec(sink.shape, lambda b, i: (0, 0))]
    args += [kc, vtc, sink]
    return pl.pallas_call(
        functools.partial(_win_kernel, band=band, nblk=nblk),
        grid=(bsz, nblk),
        in_specs=in_specs,
        out_specs=pl.BlockSpec((1, t, out_w), own),
        out_shape=jax.ShapeDtypeStruct((bsz, nblk * t, out_w), BF16),
        compiler_params=_cparams(("parallel", "parallel")),
        name="win_attn" if band else "win_attn_ctx",
    )(*args)


def _merge_kernel(x_ref, ya_ref, yb_ref, yc_ref, sh_ref, sc_ref, gt_ref, g2_ref, g3_ref,
                  wg_ref, bw_ref, wo_ref, o_ref):
    x = x_ref[0]
    d = x.shape[1]
    hm = _norm_mod(x, g2_ref[...], sh_ref[0], sc_ref[0]).astype(BF16)
    merged = jnp.zeros(x.shape, F32)
    for i, y_ref in enumerate((ya_ref, yb_ref, yc_ref)):
        gate = jax.nn.sigmoid(_dot(hm, wg_ref[:, i * d:(i + 1) * d]))
        merged = merged + gate * _dot(y_ref[0], bw_ref[i])
    y = _dot(merged.astype(BF16), wo_ref[...])
    o_ref[0] = x + gt_ref[0] * _rms(y, g3_ref[...])


def _merge(x, ya, yb, yc, shift, scale, gate, g2, g3, wg, bw, wo):
    bsz, n, d = x.shape
    tm = min(512, n)
    tok = pl.BlockSpec((1, tm, d), lambda b, i: (b, i, 0))
    gain = pl.BlockSpec((1, d), lambda b, i: (0, 0))
    return pl.pallas_call(
        _merge_kernel,
        grid=(bsz, n // tm),
        in_specs=[tok, tok, tok, tok, _mod_spec(shift), _mod_spec(scale), _mod_spec(gate),
                  gain, gain, _resident(wg.shape), _resident(bw.shape), _resident(wo.shape)],
        out_specs=tok,
        out_shape=jax.ShapeDtypeStruct(x.shape, F32),
        compiler_params=_cparams(("parallel", "parallel")),
        name="merge",
    )(x, ya, yb, yc, shift, scale, gate, g2, g3, wg, bw, wo)


def _rope_tables(n):
    quarter = ROPE_DIM // 4
    pos = jnp.arange(n)
    row = (pos // GRID_W).astype(F32)
    col = (pos % GRID_W).astype(F32)
    inv_freq = 1.0 / (ROPE_BASE ** (jnp.arange(quarter, dtype=F32) / quarter))
    ang_r = row[:, None] * inv_freq
    ang_c = col[:, None] * inv_freq
    cos = jnp.concatenate([jnp.cos(ang_r)] * 2 + [jnp.cos(ang_c)] * 2, axis=1)
    sin = jnp.concatenate([-jnp.sin(ang_r), jnp.sin(ang_r), -jnp.sin(ang_c), jnp.sin(ang_c)], axis=1)
    return jnp.tile(cos, (1, 2)), jnp.tile(sin, (1, 2))


def _layer_weights(l, ffn_w_in, ffn_w_out, mix_w_in, mla_w_uq, mla_w_ukv, branch_w, mix_w_out):
    d = mix_w_in.shape[1]
    w = mix_w_in[l]
    hw = DIFF_HEADS * 2 * DIFF_D
    wq_n = WIN_Q_HEADS * WIN_HEAD_DIM
    wk_n = WIN_KV_HEADS * WIN_HEAD_DIM
    c = 3 * hw
    w_diff = w[:, :c]
    rest_n = wq_n + 2 * wk_n + MLA_Q_RANK + MLA_KV_RANK + MLA_ROPE
    w_wm = jnp.concatenate([w[:, c:c + rest_n], jnp.zeros((d, _PB_COLS - rest_n), w.dtype)], axis=1)
    w_gate = w[:, c + rest_n:]
    wuq = mla_w_uq[l].reshape(MLA_Q_RANK, MLA_HEADS, MLA_NOPE + MLA_ROPE)
    wuq = jnp.pad(wuq, ((0, 0), (0, 0), (0, 2 * LANES - MLA_NOPE - MLA_ROPE)))
    wuq = wuq.reshape(MLA_Q_RANK, MLA_HEADS * 2 * LANES)
    return dict(
        ffn_in=[ffn_w_in[l, i].astype(BF16) for i in range(2)],
        ffn_out=[ffn_w_out[l, i].astype(BF16) for i in range(2)],
        w_diff=w_diff.astype(BF16), w_wm=w_wm.astype(BF16), w_gate=w_gate.astype(BF16),
        wuq=wuq.astype(BF16), wukv=mla_w_ukv[l].astype(BF16),
        bw=branch_w[l].astype(BF16), wo=mix_w_out[l].astype(BF16))


def kernel(x, c, ctx, c_ctx, ada_w, ada_b, norm_g, ffn_w_in, ffn_w_out, mix_w_in, diff_lambda,
           diff_subln_g, win_sink, mla_q_norm_g, mla_kv_norm_g, mla_w_uq, mla_w_ukv, branch_w,
           mix_w_out):
    bsz, s, d = x.shape
    nctx = ctx.shape[1]
    depth = ada_w.shape[0]

    cvecs = jnp.zeros((8, d), F32).at[:bsz].set(c).at[bsz].set(c_ctx)
    mods = _ada_mods(cvecs, ada_w, ada_b).reshape(depth, 8, N_MOD, d)

    cos_x, sin_x = _rope_tables(s)
    cos_c = jnp.ones((nctx, LANES), F32)
    sin_c = jnp.zeros((nctx, LANES), F32)

    h = ctx
    for l in range(depth):
        last = l == depth - 1
        lambda_init = 0.8 - 0.6 * math.exp(-0.3 * l)
        wts = _layer_weights(l, ffn_w_in, ffn_w_out, mix_w_in, mla_w_uq, mla_w_ukv, branch_w,
                             mix_w_out)
        mx = [mods[l, :bsz, k][:, None, :] for k in range(N_MOD)]
        mc = [mods[l, bsz:bsz + 1, k][:, None, :] for k in range(N_MOD)]
        g = [norm_g[l, k][None, :] for k in range(6)]
        qn = mla_q_norm_g[l][None, :]
        kvn = mla_kv_norm_g[l][None, :]
        dl = diff_lambda[l].astype(F32)
        subln = diff_subln_g[l][None, :]
        sink = jnp.broadcast_to(win_sink[l].astype(F32)[:, None], (WIN_Q_HEADS, LANES))

        x = _ffn(x, mx[0], mx[1], mx[2], g[0], g[1], wts["ffn_in"][0], wts["ffn_out"][0])
        h = _ffn(h, mc[0], mc[1], mc[2], g[0], g[1], wts["ffn_in"][0], wts["ffn_out"][0])

        dqt_x, dk_x, dvt_x = _proj_diff(x, mx[3], mx[4], g[2], cos_x, sin_x, wts["w_diff"])
        dqt_c, dk_c, dvt_c = _proj_diff(h, mc[3], mc[4], g[2], cos_c, sin_c, wts["w_diff"])
        wqt_x, wk_x, wvt_x, mqt_x, mk_x, mvt_x = _proj_wm(
            x, mx[3], mx[4], g[2], cos_x, sin_x, wts["w_wm"], wts["wuq"], wts["wukv"], qn, kvn)
        wqt_c, wk_c, wvt_c, mqt_c, mk_c, mvt_c = _proj_wm(
            h, mc[3], mc[4], g[2], cos_c, sin_c, wts["w_wm"], wts["wuq"], wts["wukv"], qn, kvn)

        ya = _flash("diff", dqt_x, [(dk_x, dvt_x), (dk_c, dvt_c)], (dl, subln), lambda_init)
        yb = _win_attn(wqt_x, wk_x, wvt_x, wk_c, wvt_c, sink, band=True)
        yc = _flash("mla", mqt_x, [(mk_x, mvt_x), (mk_c, mvt_c)])
        x_new = _merge(x, ya, yb, yc, mx[3], mx[4], mx[5], g[2], g[3],
                       wts["w_gate"], wts["bw"], wts["wo"])
        if not last:
            ca = _flash("diff", dqt_c, [(dk_c, dvt_c)], (dl, subln), lambda_init)
            cb = _win_attn(wqt_c, None, None, wk_c, wvt_c, sink, band=False)
            cc = _flash("mla", mqt_c, [(mk_c, mvt_c)])
            h = _merge(h, ca, cb, cc, mc[3], mc[4], mc[5], g[2], g[3],
                       wts["w_gate"], wts["bw"], wts["wo"])
            h = _ffn(h, mc[6], mc[7], mc[8], g[4], g[5], wts["ffn_in"][1], wts["ffn_out"][1])
        x = _ffn(x_new, mx[6], mx[7], mx[8], g[4], g[5], wts["ffn_in"][1], wts["ffn_out"][1])
    return x
```

```python
import functools
import math

import jax
import jax.numpy as jnp
from jax import lax
from jax.experimental import pallas as pl
from jax.experimental.pallas import tpu as pltpu

F32 = jnp.float32
BF16 = jnp.bfloat16

GRID_W = 64
QBLK = 128
EPS = 1e-6
NEG_INF = -1e30
ROPE_DIM = 64
ROPE_BASE = 10000.0
N_MOD = 9
FFN_RES = 0.5
DIFF_HEADS = 8
DIFF_D = 64
WIN_Q_HEADS = 16
WIN_KV_HEADS = 4
WIN_GROUP = 4
WIN_HEAD_DIM = 64
WIN_SCALE = WIN_HEAD_DIM ** -0.5
MLA_HEADS = 8
MLA_Q_RANK = 384
MLA_KV_RANK = 256
MLA_NOPE = 128
MLA_ROPE = 64
MLA_V = 128
MLA_SCALE = (MLA_NOPE + MLA_ROPE) ** -0.5
LOG2E = math.log2(math.e)

LANES = 128
VMEM_LIMIT = 56 * 1024 * 1024

FLASH_TQ = 512
MLA_TILES_PER_STEP = 2
KV_CHUNK = 2048
V_ROWS = LANES + 16
SPEC_FINITE_LIMIT = 3.0e38
SPEC_PROBE_ROWS = 128


def _cparams(sem):
    return pltpu.CompilerParams(dimension_semantics=sem, vmem_limit_bytes=VMEM_LIMIT)


def _resident(shape):
    nd = len(shape)
    return pl.BlockSpec(shape, lambda *_: (0,) * nd, pipeline_mode=pl.Buffered(1))


def _rms(x, g):
    return x * lax.rsqrt(jnp.mean(x * x, axis=-1, keepdims=True) + EPS) * g


def _norm_mod(x, g, shift, scale):
    return _rms(x, g) * (1.0 + scale) + shift


def _dot(a, b):
    return jnp.dot(a, b, preferred_element_type=F32)


def _dot_nt(a, b):
    return lax.dot_general(a, b, (((1,), (1,)), ((), ())), preferred_element_type=F32)


def _mod_spec(arr):
    d = arr.shape[-1]
    if arr.shape[0] == 1:
        return pl.BlockSpec((1, 1, d), lambda b, *_: (0, 0, 0))
    return pl.BlockSpec((1, 1, d), lambda b, *_: (b, 0, 0))


def _ada_kernel(c_ref, w_ref, b_ref, o_ref):
    c = c_ref[...]
    a = c * jax.nn.sigmoid(c)
    o_ref[0] = jnp.dot(a, w_ref[0], preferred_element_type=F32,
                       precision=lax.Precision.HIGHEST) + b_ref[0]


def _ada_mods(cvecs, ada_w, ada_b):
    depth, d, nd = ada_w.shape
    rows = cvecs.shape[0]
    tn = 1152 if nd % 1152 == 0 else nd
    return pl.pallas_call(
        _ada_kernel,
        grid=(depth, nd // tn),
        in_specs=[pl.BlockSpec((rows, d), lambda l, j: (0, 0)),
                  pl.BlockSpec((1, d, tn), lambda l, j: (l, 0, j)),
                  pl.BlockSpec((1, 1, tn), lambda l, j: (l, 0, j))],
        out_specs=pl.BlockSpec((1, rows, tn), lambda l, j: (l, 0, j)),
        out_shape=jax.ShapeDtypeStruct((depth, rows, nd), F32),
        compiler_params=_cparams(("parallel", "parallel")),
        name="ada_mods",
    )(cvecs, ada_w, ada_b.reshape(depth, 1, nd))


def _ffn_kernel(x_ref, sh_ref, sc_ref, gt_ref, gpre_ref, gpost_ref, win_ref, wout_ref, o_ref,
                *, ffn_dim, chunk):
    x = x_ref[0]
    xm = _norm_mod(x, gpre_ref[...], sh_ref[0], sc_ref[0]).astype(BF16)
    acc = jnp.zeros(x.shape, F32)
    for c in range(ffn_dim // chunk):
        a = _dot(xm, win_ref[:, c * chunk:(c + 1) * chunk])
        b = _dot(xm, win_ref[:, ffn_dim + c * chunk:ffn_dim + (c + 1) * chunk])
        h = (a * jax.nn.sigmoid(a) * b).astype(BF16)
        acc = acc + _dot(h, wout_ref[c * chunk:(c + 1) * chunk, :])
    o_ref[0] = x + FFN_RES * gt_ref[0] * _rms(acc, gpost_ref[...])


def _ffn(x, shift, scale, gate, g_pre, g_post, w_in, w_out):
    bsz, n, d = x.shape
    ffn_dim = w_out.shape[0]
    tm = min(512, n)
    tok = pl.BlockSpec((1, tm, d), lambda b, i: (b, i, 0))
    gain = pl.BlockSpec((1, d), lambda b, i: (0, 0))
    return pl.pallas_call(
        functools.partial(_ffn_kernel, ffn_dim=ffn_dim, chunk=256),
        grid=(bsz, n // tm),
        in_specs=[tok, _mod_spec(shift), _mod_spec(scale), _mod_spec(gate), gain, gain,
                  _resident(w_in.shape), _resident(w_out.shape)],
        out_specs=tok,
        out_shape=jax.ShapeDtypeStruct(x.shape, F32),
        compiler_params=_cparams(("parallel", "parallel")),
        name="ffn",
    )(x, shift, scale, gate, g_pre, g_post, w_in, w_out)


def _rope_masks(tm):
    lane = lax.broadcasted_iota(jnp.int32, (tm, LANES), 1)
    even = ((lane // (ROPE_DIM // 4)) & 1) == 0
    lo = lane < (LANES // 2)
    return even, lo


def _rope(blk, cos, sin, even):
    q = ROPE_DIM // 4
    partner = jnp.where(even, pltpu.roll(blk, LANES - q, 1), pltpu.roll(blk, q, 1))
    return blk * cos + partner * sin


def _store_vt(vt_ref, h, v_blk):
    tm = v_blk.shape[0]
    vt_ref[0, h, 0, 0:LANES, :] = v_blk.T.astype(BF16)
    row = lax.broadcasted_iota(jnp.int32, (V_ROWS - LANES, tm), 0)
    vt_ref[0, h, 0, LANES:V_ROWS, :] = jnp.where(row == 0, 1.0, 0.0).astype(BF16)


def _kv_specs(bsz, n, tm, heads, dk):
    chunk = min(KV_CHUNK, n)
    per = chunk // tm
    specs = [pl.BlockSpec((1, heads, tm, dk), lambda b, i: (b, 0, i, 0)),
             pl.BlockSpec((1, heads, 1, V_ROWS, tm), lambda b, i: (b, 0, i // per, 0, i % per))]
    shapes = [jax.ShapeDtypeStruct((bsz, heads, n, dk), BF16),
              jax.ShapeDtypeStruct((bsz, heads, n // chunk, V_ROWS, chunk), BF16)]
    return specs, shapes


def _proj_diff_kernel(x_ref, sh_ref, sc_ref, g_ref, cos_ref, sin_ref, w_ref,
                      qt_ref, k_ref, vt_ref):
    hm = _norm_mod(x_ref[0], g_ref[...], sh_ref[0], sc_ref[0]).astype(BF16)
    tm = hm.shape[0]
    cos = cos_ref[...]
    sin = sin_ref[...]
    even, _ = _rope_masks(tm)
    hw = DIFF_HEADS * 2 * DIFF_D
    q = _dot(hm, w_ref[:, 0:hw])
    k = _dot(hm, w_ref[:, hw:2 * hw])
    v = _dot(hm, w_ref[:, 2 * hw:3 * hw])
    scale = LOG2E / math.sqrt(DIFF_D)
    top = lax.broadcasted_iota(jnp.int32, (LANES, tm), 0) < DIFF_D
    for h in range(DIFF_HEADS):
        hs = slice(h * LANES, (h + 1) * LANES)
        qbt = (_rope(q[:, hs], cos, sin, even) * scale).T
        qt_ref[0, h, 0, :, 0:tm] = jnp.where(top, qbt, 0.0).astype(BF16)
        qt_ref[0, h, 0, :, tm:2 * tm] = jnp.where(top, 0.0, qbt).astype(BF16)
        k_ref[0, h] = _rope(k[:, hs], cos, sin, even).astype(BF16)
        _store_vt(vt_ref, h, v[:, hs])


def _proj_diff(x, shift, scale, g, cos, sin, w):
    bsz, n, d = x.shape
    tm = min(FLASH_TQ, n)
    tok = pl.BlockSpec((1, tm, d), lambda b, i: (b, i, 0))
    tab = pl.BlockSpec((tm, LANES), lambda b, i: (i, 0))
    kv_specs, kv_shapes = _kv_specs(bsz, n, tm, DIFF_HEADS, LANES)
    return pl.pallas_call(
        _proj_diff_kernel,
        grid=(bsz, n // tm),
        in_specs=[tok, _mod_spec(shift), _mod_spec(scale),
                  pl.BlockSpec((1, d), lambda b, i: (0, 0)), tab, tab, _resident(w.shape)],
        out_specs=[pl.BlockSpec((1, DIFF_HEADS, 1, LANES, 2 * tm), lambda b, i: (b, 0, i, 0, 0))]
        + kv_specs,
        out_shape=[jax.ShapeDtypeStruct((bsz, DIFF_HEADS, n // tm, LANES, 2 * tm), BF16)] + kv_shapes,
        compiler_params=_cparams(("parallel", "parallel")),
        name="proj_diff",
    )(x, shift, scale, g, cos, sin, w)


_WQ0, _WK0, _WV0 = 0, 1024, 1280
_CQ0, _CKV0, _KR0, _PB_COLS = 1536, 1920, 2176, 2304


def _proj_wm_kernel(x_ref, sh_ref, sc_ref, g_ref, cos_ref, sin_ref, w_ref, wuq_ref, wukv_ref,
                    qn_ref, kvn_ref, wqt_ref, wk_ref, wvt_ref, mqt_ref, mk_ref, mvt_ref):
    hm = _norm_mod(x_ref[0], g_ref[...], sh_ref[0], sc_ref[0]).astype(BF16)
    tm = hm.shape[0]
    cos = cos_ref[...]
    sin = sin_ref[...]
    even, lo = _rope_masks(tm)
    half = LANES // 2
    p = _dot(hm, w_ref[...])

    def two_heads(blk):
        return jnp.where(lo, blk, 0.0), jnp.where(lo, pltpu.roll(blk, half, 1), 0.0)

    for j in range(WIN_Q_HEADS // 2):
        blk = _rope(p[:, _WQ0 + j * LANES:_WQ0 + (j + 1) * LANES], cos, sin, even) * (WIN_SCALE * LOG2E)
        for sub, padded in enumerate(two_heads(blk)):
            kvh, grp = divmod(2 * j + sub, WIN_GROUP)
            qt = padded.T.astype(BF16)
            for qb in range(tm // QBLK):
                wqt_ref[0, kvh, qb, :, grp * QBLK:(grp + 1) * QBLK] = qt[:, qb * QBLK:(qb + 1) * QBLK]
    for j in range(WIN_KV_HEADS // 2):
        kblk = two_heads(_rope(p[:, _WK0 + j * LANES:_WK0 + (j + 1) * LANES], cos, sin, even))
        vblk = two_heads(p[:, _WV0 + j * LANES:_WV0 + (j + 1) * LANES])
        for sub in range(2):
            kvh = 2 * j + sub
            wk_ref[0, :, kvh * LANES:(kvh + 1) * LANES] = kblk[sub].astype(BF16)
            wvt_ref[0, kvh, 0:LANES, :] = vblk[sub].T.astype(BF16)
            row = lax.broadcasted_iota(jnp.int32, (V_ROWS - LANES, tm), 0)
            wvt_ref[0, kvh, LANES:V_ROWS, :] = jnp.where(row == 0, 1.0, 0.0).astype(BF16)

    cq = _rms(p[:, _CQ0:_CQ0 + MLA_Q_RANK], qn_ref[...]).astype(BF16)
    q2 = _dot(cq, wuq_ref[...])
    ckv = _rms(p[:, _CKV0:_CKV0 + MLA_KV_RANK], kvn_ref[...]).astype(BF16)
    kv = _dot(ckv, wukv_ref[...])
    kr = _rope(p[:, _KR0:_KR0 + LANES], cos, sin, even).astype(BF16)
    for h in range(MLA_HEADS):
        c0 = 2 * h * LANES
        qn = q2[:, c0:c0 + LANES] * (MLA_SCALE * LOG2E)
        qr = _rope(q2[:, c0 + LANES:c0 + 2 * LANES], cos, sin, even) * (MLA_SCALE * LOG2E)
        mqt_ref[0, h, 0, 0:LANES, :] = qn.T.astype(BF16)
        mqt_ref[0, h, 0, LANES:2 * LANES, :] = qr.T.astype(BF16)
        mk_ref[0, h, :, 0:LANES] = kv[:, c0:c0 + LANES].astype(BF16)
        mk_ref[0, h, :, LANES:2 * LANES] = kr
        _store_vt(mvt_ref, h, kv[:, c0 + LANES:c0 + 2 * LANES])


def _proj_wm(x, shift, scale, g, cos, sin, w, wuq, wukv, qn, kvn):
    bsz, n, d = x.shape
    tm = min(FLASH_TQ, n)
    tok = lambda c: pl.BlockSpec((1, tm, c), lambda b, i: (b, i, 0))
    tab = pl.BlockSpec((tm, LANES), lambda b, i: (i, 0))
    row = lambda c: pl.BlockSpec((1, c), lambda b, i: (0, 0))
    kv_specs, kv_shapes = _kv_specs(bsz, n, tm, MLA_HEADS, 2 * LANES)
    qrow = WIN_GROUP * QBLK
    win_specs = [pl.BlockSpec((1, WIN_KV_HEADS, tm // QBLK, LANES, qrow), lambda b, i: (b, 0, i, 0, 0)),
                 tok(WIN_KV_HEADS * LANES),
                 pl.BlockSpec((1, WIN_KV_HEADS, V_ROWS, tm), lambda b, i: (b, 0, 0, i))]
    win_shapes = [jax.ShapeDtypeStruct((bsz, WIN_KV_HEADS, n // QBLK, LANES, qrow), BF16),
                  jax.ShapeDtypeStruct((bsz, n, WIN_KV_HEADS * LANES), BF16),
                  jax.ShapeDtypeStruct((bsz, WIN_KV_HEADS, V_ROWS, n), BF16)]
    return pl.pallas_call(
        _proj_wm_kernel,
        grid=(bsz, n // tm),
        in_specs=[tok(d), _mod_spec(shift), _mod_spec(scale), row(d), tab, tab,
                  _resident(w.shape), _resident(wuq.shape), _resident(wukv.shape),
                  row(MLA_Q_RANK), row(MLA_KV_RANK)],
        out_specs=win_specs
        + [pl.BlockSpec((1, MLA_HEADS, 1, 2 * LANES, tm), lambda b, i: (b, 0, i, 0, 0))] + kv_specs,
        out_shape=win_shapes
        + [jax.ShapeDtypeStruct((bsz, MLA_HEADS, n // tm, 2 * LANES, tm), BF16)] + kv_shapes,
        compiler_params=_cparams(("parallel", "parallel")),
        name="proj_wm",
    )(x, shift, scale, g, cos, sin, w, wuq, wukv, qn, kvn)


def _segments(seg_refs):
    segs = []
    for k_ref, vt_ref in zip(seg_refs[0::2], seg_refs[1::2]):
        segs.append((k_ref, vt_ref, vt_ref.shape[2], vt_ref.shape[4]))
    return segs


def _flash_loop_exact(qt, segs, m_sc, acc_sc):
    m_sc[...] = jnp.full(m_sc.shape, NEG_INF, F32)
    acc_sc[...] = jnp.zeros(acc_sc.shape, F32)
    for k_ref, vt_ref, nc, tk in segs:
        def body(j, carry, k_ref=k_ref, vt_ref=vt_ref, tk=tk):
            start = pl.multiple_of(j * tk, tk)
            s = _dot(k_ref[0, 0, pl.ds(start, tk), :], qt)
            m_prev = m_sc[...]
            m_new = jnp.maximum(m_prev, jnp.max(s, axis=0, keepdims=True))
            p = jnp.exp2(s - m_new)
            acc_sc[...] = jnp.exp2(m_prev - m_new) * acc_sc[...] + _dot(vt_ref[0, 0, j], p.astype(BF16))
            m_sc[...] = m_new
            return carry
        lax.fori_loop(0, nc, body, 0)


def _flash_loop_spec(qt, segs, acc_sc):
    probe = _dot(segs[0][0][0, 0, 0:SPEC_PROBE_ROWS, :], qt)
    m0 = jnp.max(probe, axis=0, keepdims=True)
    acc_sc[...] = jnp.zeros(acc_sc.shape, F32)
    chunks = [(k_ref, vt_ref, j, tk) for k_ref, vt_ref, nc, tk in segs for j in range(nc)]

    def scores(chunk):
        k_ref, _, j, tk = chunk
        return _dot(k_ref[0, 0, j * tk:(j + 1) * tk, :], qt)

    s = scores(chunks[0])
    for idx, (_, vt_ref, j, _) in enumerate(chunks):
        pf = jnp.exp2(s - m0)
        lsum = jnp.sum(pf, axis=0, keepdims=True)
        p = pf.astype(BF16)
        if idx + 1 < len(chunks):
            s = scores(chunks[idx + 1])
        acc_sc[0:LANES, :] += _dot(vt_ref[0, 0, j][0:LANES], p)
        acc_sc[LANES:LANES + 1, :] += lsum


def _flash_tile(qt, segs, m_sc, acc_sc):
    _flash_loop_spec(qt, segs, acc_sc)
    finite = jnp.where(jnp.abs(acc_sc[0:LANES + 1, :]) <= SPEC_FINITE_LIMIT, 1.0, 0.0)

    @pl.when(jnp.min(finite) < 0.5)
    def _redo():
        _flash_loop_exact(qt, segs, m_sc, acc_sc)


def _diff_flash_kernel(*refs, nseg, lambda_init):
    qt_ref, seg_refs = refs[0], refs[1:1 + 2 * nseg]
    dl_ref, g_ref, o_ref, m_sc, acc_sc = refs[1 + 2 * nseg:]
    _flash_tile(qt_ref[0, 0, 0], _segments(seg_refs), m_sc, acc_sc)
    tq = o_ref.shape[1]
    o = acc_sc[0:LANES, :] / acc_sc[LANES:LANES + 1, :]
    dl = dl_ref[...]
    lam = (jnp.exp(jnp.sum(dl[0:1] * dl[1:2], axis=1, keepdims=True))
           - jnp.exp(jnp.sum(dl[2:3] * dl[3:4], axis=1, keepdims=True)) + lambda_init)
    y = (o[:, :tq] - lam * o[:, tq:]).T
    o_ref[0] = (_rms(y, g_ref[...]) * (1.0 - lambda_init)).astype(BF16)


def _mla_flash_kernel(*refs, nseg):
    qt_ref, seg_refs = refs[0], refs[1:1 + 2 * nseg]
    o_ref, m_sc, acc_sc = refs[1 + 2 * nseg:]
    qt = jnp.concatenate([qt_ref[0, 0, t] for t in range(qt_ref.shape[2])], axis=1)
    _flash_tile(qt, _segments(seg_refs), m_sc, acc_sc)
    o_ref[0] = (acc_sc[0:LANES, :] / acc_sc[LANES:LANES + 1, :]).T.astype(BF16)


def _flash(mode, qt, kv_segs, extra=(), lambda_init=0.0):
    bsz, heads, nq, dk, w = qt.shape
    maps = 2 if mode == "diff" else 1
    tps = MLA_TILES_PER_STEP if (mode == "mla" and nq % MLA_TILES_PER_STEP == 0) else 1
    tq = tps * w // maps
    nq //= tps
    in_specs = [pl.BlockSpec((1, 1, tps, dk, w), lambda b, h, i: (b, h, i, 0, 0))]
    w *= tps
    args = [qt]
    for k, vt in kv_segs:
        in_specs += [pl.BlockSpec((1, 1) + k.shape[2:], lambda b, h, i: (b, h, 0, 0)),
                     pl.BlockSpec((1, 1) + vt.shape[2:], lambda b, h, i: (b, h, 0, 0, 0))]
        args += [k, vt]
    if mode == "diff":
        kern = functools.partial(_diff_flash_kernel, nseg=len(kv_segs), lambda_init=lambda_init)
        in_specs += [pl.BlockSpec(e.shape, lambda b, h, i: (0, 0)) for e in extra]
    else:
        kern = functools.partial(_mla_flash_kernel, nseg=len(kv_segs))
    return pl.pallas_call(
        kern,
        grid=(bsz, heads, nq),
        in_specs=in_specs,
        out_specs=pl.BlockSpec((1, tq, LANES), lambda b, h, i: (b, i, h)),
        out_shape=jax.ShapeDtypeStruct((bsz, nq * tq, heads * LANES), BF16),
        scratch_shapes=[pltpu.VMEM((1, w), F32), pltpu.VMEM((V_ROWS, w), F32)],
        compiler_params=_cparams(("parallel", "parallel", "arbitrary")),
        name=f"{mode}_flash",
    )(*args, *extra)


def _win_kernel(*refs, band, nblk):
    if band:
        (q_ref, kp_ref, ko_ref, kn_ref, vp_ref, vo_ref, vn_ref, kc_ref, vc_ref,
         sink_ref, o_ref) = refs
    else:
        q_ref, kc_ref, vc_ref, sink_ref, o_ref = refs
    t = QBLK
    w = WIN_GROUP * t
    if band:
        i = pl.program_id(1)
        kk = lax.broadcasted_iota(jnp.int32, (t, w), 0)
        qq = lax.broadcasted_iota(jnp.int32, (t, w), 1) % t
        ok_prev = kk >= qq + jnp.where(i > 0, 0, t)
        ok_next = kk <= qq - jnp.where(i < nblk - 1, 0, t)
    parts, sinks, maxes = [], [], []
    for kvh in range(WIN_KV_HEADS):
        ksl = slice(kvh * LANES, (kvh + 1) * LANES)
        h0 = kvh * WIN_GROUP
        qt = q_ref[0, kvh, 0]
        sink = jnp.concatenate([sink_ref[h0 + g:h0 + g + 1, :] for g in range(WIN_GROUP)],
                               axis=1) * LOG2E
        s_all = [_dot(kc_ref[0, :, ksl], qt)]
        if band:
            s_all = [jnp.where(ok_prev, _dot(kp_ref[0, :, ksl], qt), NEG_INF),
                     _dot(ko_ref[0, :, ksl], qt),
                     jnp.where(ok_next, _dot(kn_ref[0, :, ksl], qt), NEG_INF)] + s_all
        m = sink
        for part in s_all:
            m = jnp.maximum(m, jnp.max(part, axis=0, keepdims=True))
        parts.append(s_all)
        sinks.append(sink)
        maxes.append(m)
    outs = []
    for kvh in range(WIN_KV_HEADS):
        m, sink = maxes[kvh], sinks[kvh]
        p = [jnp.exp2(part - m).astype(BF16) for part in parts[kvh]]
        if band:
            p = jnp.concatenate(p, axis=0)
            vt = jnp.concatenate([vp_ref[0, kvh], vo_ref[0, kvh], vn_ref[0, kvh], vc_ref[0, kvh]],
                                 axis=1)
        else:
            p, vt = p[0], vc_ref[0, kvh]
        pv = _dot(vt, p)
        outs.append(pv[0:LANES, :] / (pv[LANES:LANES + 1, :] + jnp.exp2(sink - m)))
    for kvh in range(WIN_KV_HEADS):
        o = outs[kvh]
        for pair in range(WIN_GROUP // 2):
            a = o[0:WIN_HEAD_DIM, (2 * pair) * t:(2 * pair + 1) * t]
            b = o[0:WIN_HEAD_DIM, (2 * pair + 1) * t:(2 * pair + 2) * t]
            blk = kvh * (WIN_GROUP // 2) + pair
            o_ref[0, :, blk * LANES:(blk + 1) * LANES] = jnp.concatenate([a, b], axis=0).T.astype(BF16)


def _win_attn(qt, k, vt, kc, vtc, sink, band):
    bsz, _, nblk, _, qrow = qt.shape
    kw = kc.shape[2]
    nctx = kc.shape[1]
    t = QBLK
    out_w = WIN_Q_HEADS * WIN_HEAD_DIM
    own = lambda b, i: (b, i, 0)
    in_specs = [pl.BlockSpec((1, WIN_KV_HEADS, 1, LANES, qrow), lambda b, i: (b, 0, i, 0, 0))]
    args = [qt]
    if band:
        prev = lambda b, i: (b, jnp.maximum(i - 1, 0), 0)
        nxt = lambda b, i: (b, jnp.minimum(i + 1, nblk - 1), 0)
        kspec = lambda f: pl.BlockSpec((1, t, kw), f)
        vspec = lambda f: pl.BlockSpec((1, WIN_KV_HEADS, V_ROWS, t),
                                       lambda b, i, f=f: (b, 0, 0, f(b, i)[1]))
        in_specs += [kspec(prev), kspec(own), kspec(nxt), vspec(prev), vspec(own), vspec(nxt)]
        args += [k, k, k, vt, vt, vt]
    in_specs += [pl.BlockSpec((1, nctx, kw), lambda b, i: (b, 0, 0)),
                 pl.BlockSpec((1, WIN_KV_HEADS, V_ROWS, nctx), lambda b, i: (b, 0, 0, 0)),
                 pl.BlockSpec(sink.shape, lambda b, i: (0, 0))]
    args += [kc, vtc, sink]
    return pl.pallas_call(
        functools.partial(_win_kernel, band=band, nblk=nblk),
        grid=(bsz, nblk),
        in_specs=in_specs,
        out_specs=pl.BlockSpec((1, t, out_w), own),
        out_shape=jax.ShapeDtypeStruct((bsz, nblk * t, out_w), BF16),
        compiler_params=_cparams(("parallel", "parallel")),
        name="win_attn" if band else "win_attn_ctx",
    )(*args)


def _merge_kernel(x_ref, ya_ref, yb_ref, yc_ref, sh_ref, sc_ref, gt_ref, g2_ref, g3_ref,
                  wg_ref, bw_ref, wo_ref, o_ref):
    x = x_ref[0]
    d = x.shape[1]
    hm = _norm_mod(x, g2_ref[...], sh_ref[0], sc_ref[0]).astype(BF16)
    merged = jnp.zeros(x.shape, F32)
    for i, y_ref in enumerate((ya_ref, yb_ref, yc_ref)):
        gate = jax.nn.sigmoid(_dot(hm, wg_ref[:, i * d:(i + 1) * d]))
        merged = merged + gate * _dot(y_ref[0], bw_ref[i])
    y = _dot(merged.astype(BF16), wo_ref[...])
    o_ref[0] = x + gt_ref[0] * _rms(y, g3_ref[...])


def _merge(x, ya, yb, yc, shift, scale, gate, g2, g3, wg, bw, wo):
    bsz, n, d = x.shape
    tm = min(512, n)
    tok = pl.BlockSpec((1, tm, d), lambda b, i: (b, i, 0))
    gain = pl.BlockSpec((1, d), lambda b, i: (0, 0))
    return pl.pallas_call(
        _merge_kernel,
        grid=(bsz, n // tm),
        in_specs=[tok, tok, tok, tok, _mod_spec(shift), _mod_spec(scale), _mod_spec(gate),
                  gain, gain, _resident(wg.shape), _resident(bw.shape), _resident(wo.shape)],
        out_specs=tok,
        out_shape=jax.ShapeDtypeStruct(x.shape, F32),
        compiler_params=_cparams(("parallel", "parallel")),
        name="merge",
    )(x, ya, yb, yc, shift, scale, gate, g2, g3, wg, bw, wo)


def _rope_tables(n):
    quarter = ROPE_DIM // 4
    pos = jnp.arange(n)
    row = (pos // GRID_W).astype(F32)
    col = (pos % GRID_W).astype(F32)
    inv_freq = 1.0 / (ROPE_BASE ** (jnp.arange(quarter, dtype=F32) / quarter))
    ang_r = row[:, None] * inv_freq
    ang_c = col[:, None] * inv_freq
    cos = jnp.concatenate([jnp.cos(ang_r)] * 2 + [jnp.cos(ang_c)] * 2, axis=1)
    sin = jnp.concatenate([-jnp.sin(ang_r), jnp.sin(ang_r), -jnp.sin(ang_c), jnp.sin(ang_c)], axis=1)
    return jnp.tile(cos, (1, 2)), jnp.tile(sin, (1, 2))


def _layer_weights(l, ffn_w_in, ffn_w_out, mix_w_in, mla_w_uq, mla_w_ukv, branch_w, mix_w_out):
    d = mix_w_in.shape[1]
    w = mix_w_in[l]
    hw = DIFF_HEADS * 2 * DIFF_D
    wq_n = WIN_Q_HEADS * WIN_HEAD_DIM
    wk_n = WIN_KV_HEADS * WIN_HEAD_DIM
    c = 3 * hw
    w_diff = w[:, :c]
    rest_n = wq_n + 2 * wk_n + MLA_Q_RANK + MLA_KV_RANK + MLA_ROPE
    w_wm = jnp.concatenate([w[:, c:c + rest_n], jnp.zeros((d, _PB_COLS - rest_n), w.dtype)], axis=1)
    w_gate = w[:, c + rest_n:]
    wuq = mla_w_uq[l].reshape(MLA_Q_RANK, MLA_HEADS, MLA_NOPE + MLA_ROPE)
    wuq = jnp.pad(wuq, ((0, 0), (0, 0), (0, 2 * LANES - MLA_NOPE - MLA_ROPE)))
    wuq = wuq.reshape(MLA_Q_RANK, MLA_HEADS * 2 * LANES)
    return dict(
        ffn_in=[ffn_w_in[l, i].astype(BF16) for i in range(2)],
        ffn_out=[ffn_w_out[l, i].astype(BF16) for i in range(2)],
        w_diff=w_diff.astype(BF16), w_wm=w_wm.astype(BF16), w_gate=w_gate.astype(BF16),
        wuq=wuq.astype(BF16), wukv=mla_w_ukv[l].astype(BF16),
        bw=branch_w[l].astype(BF16), wo=mix_w_out[l].astype(BF16))


def kernel(x, c, ctx, c_ctx, ada_w, ada_b, norm_g, ffn_w_in, ffn_w_out, mix_w_in, diff_lambda,
           diff_subln_g, win_sink, mla_q_norm_g, mla_kv_norm_g, mla_w_uq, mla_w_ukv, branch_w,
           mix_w_out):
    bsz, s, d = x.shape
    nctx = ctx.shape[1]
    depth = ada_w.shape[0]

    cvecs = jnp.zeros((8, d), F32).at[:bsz].set(c).at[bsz].set(c_ctx)
    mods = _ada_mods(cvecs, ada_w, ada_b).reshape(depth, 8, N_MOD, d)

    cos_x, sin_x = _rope_tables(s)
    cos_c = jnp.ones((nctx, LANES), F32)
    sin_c = jnp.zeros((nctx, LANES), F32)

    h = ctx
    for l in range(depth):
        last = l == depth - 1
        lambda_init = 0.8 - 0.6 * math.exp(-0.3 * l)
        wts = _layer_weights(l, ffn_w_in, ffn_w_out, mix_w_in, mla_w_uq, mla_w_ukv, branch_w,
                             mix_w_out)
        mx = [mods[l, :bsz, k][:, None, :] for k in range(N_MOD)]
        mc = [mods[l, bsz:bsz + 1, k][:, None, :] for k in range(N_MOD)]
        g = [norm_g[l, k][None, :] for k in range(6)]
        qn = mla_q_norm_g[l][None, :]
        kvn = mla_kv_norm_g[l][None, :]
        dl = diff_lambda[l].astype(F32)
        subln = diff_subln_g[l][None, :]
        sink = jnp.broadcast_to(win_sink[l].astype(F32)[:, None], (WIN_Q_HEADS, LANES))

        x = _ffn(x, mx[0], mx[1], mx[2], g[0], g[1], wts["ffn_in"][0], wts["ffn_out"][0])
        h = _ffn(h, mc[0], mc[1], mc[2], g[0], g[1], wts["ffn_in"][0], wts["ffn_out"][0])

        dqt_x, dk_x, dvt_x = _proj_diff(x, mx[3], mx[4], g[2], cos_x, sin_x, wts["w_diff"])
        dqt_c, dk_c, dvt_c = _proj_diff(h, mc[3], mc[4], g[2], cos_c, sin_c, wts["w_diff"])
        wqt_x, wk_x, wvt_x, mqt_x, mk_x, mvt_x = _proj_wm(
            x, mx[3], mx[4], g[2], cos_x, sin_x, wts["w_wm"], wts["wuq"], wts["wukv"], qn, kvn)
        wqt_c, wk_c, wvt_c, mqt_c, mk_c, mvt_c = _proj_wm(
            h, mc[3], mc[4], g[2], cos_c, sin_c, wts["w_wm"], wts["wuq"], wts["wukv"], qn, kvn)

        ya = _flash("diff", dqt_x, [(dk_x, dvt_x), (dk_c, dvt_c)], (dl, subln), lambda_init)
        yb = _win_attn(wqt_x, wk_x, wvt_x, wk_c, wvt_c, sink, band=True)
        yc = _flash("mla", mqt_x, [(mk_x, mvt_x), (mk_c, mvt_c)])
        x_new = _merge(x, ya, yb, yc, mx[3], mx[4], mx[5], g[2], g[3],
                       wts["w_gate"], wts["bw"], wts["wo"])
        if not last:
            ca = _flash("diff", dqt_c, [(dk_c, dvt_c)], (dl, subln), lambda_init)
            cb = _win_attn(wqt_c, None, None, wk_c, wvt_c, sink, band=False)
            cc = _flash("mla", mqt_c, [(mk_c, mvt_c)])
            h = _merge(h, ca, cb, cc, mc[3], mc[4], mc[5], g[2], g[3],
                       wts["w_gate"], wts["bw"], wts["wo"])
            h = _ffn(h, mc[6], mc[7], mc[8], g[4], g[5], wts["ffn_in"][1], wts["ffn_out"][1])
        x = _ffn(x_new, mx[6], mx[7], mx[8], g[4], g[5], wts["ffn_in"][1], wts["ffn_out"][1])
    return x
```

```python
import functools
import math

import jax
import jax.numpy as jnp
from jax import lax
from jax.experimental import pallas as pl
from jax.experimental.pallas import tpu as pltpu

F32 = jnp.float32
BF16 = jnp.bfloat16

GRID_W = 64
QBLK = 128
EPS = 1e-6
NEG_INF = -1e30
ROPE_DIM = 64
ROPE_BASE = 10000.0
N_MOD = 9
FFN_RES = 0.5
DIFF_HEADS = 8
DIFF_D = 64
WIN_Q_HEADS = 16
WIN_KV_HEADS = 4
WIN_GROUP = 4
WIN_HEAD_DIM = 64
WIN_SCALE = WIN_HEAD_DIM ** -0.5
MLA_HEADS = 8
MLA_Q_RANK = 384
MLA_KV_RANK = 256
MLA_NOPE = 128
MLA_ROPE = 64
MLA_V = 128
MLA_SCALE = (MLA_NOPE + MLA_ROPE) ** -0.5
LOG2E = math.log2(math.e)

LANES = 128
VMEM_LIMIT = 56 * 1024 * 1024

FLASH_TQ = 512
FLASH_TILES_PER_STEP = 2
FLASH_ITEM_ELEMS = 2048 * 1024
KV_CHUNK = 2048
V_ROWS = LANES + 16
SPEC_FINITE_LIMIT = 3.0e38
SPEC_PROBE_ROWS = 128


def _cparams(sem):
    return pltpu.CompilerParams(dimension_semantics=sem, vmem_limit_bytes=VMEM_LIMIT)


def _resident(shape):
    nd = len(shape)
    return pl.BlockSpec(shape, lambda *_: (0,) * nd, pipeline_mode=pl.Buffered(1))


def _rms(x, g):
    return x * lax.rsqrt(jnp.mean(x * x, axis=-1, keepdims=True) + EPS) * g


def _norm_mod(x, g, shift, scale):
    return _rms(x, g) * (1.0 + scale) + shift


def _dot(a, b):
    return jnp.dot(a, b, preferred_element_type=F32)


def _dot_nt(a, b):
    return lax.dot_general(a, b, (((1,), (1,)), ((), ())), preferred_element_type=F32)


def _mod_spec(arr):
    d = arr.shape[-1]
    if arr.shape[0] == 1:
        return pl.BlockSpec((1, 1, d), lambda b, *_: (0, 0, 0))
    return pl.BlockSpec((1, 1, d), lambda b, *_: (b, 0, 0))


def _ada_kernel(c_ref, w_ref, b_ref, o_ref):
    c = c_ref[...]
    a = c * jax.nn.sigmoid(c)
    o_ref[0] = jnp.dot(a, w_ref[0], preferred_element_type=F32,
                       precision=lax.Precision.HIGHEST) + b_ref[0]


def _ada_mods(cvecs, ada_w, ada_b):
    depth, d, nd = ada_w.shape
    rows = cvecs.shape[0]
    tn = 1152 if nd % 1152 == 0 else nd
    return pl.pallas_call(
        _ada_kernel,
        grid=(depth, nd // tn),
        in_specs=[pl.BlockSpec((rows, d), lambda l, j: (0, 0)),
                  pl.BlockSpec((1, d, tn), lambda l, j: (l, 0, j)),
                  pl.BlockSpec((1, 1, tn), lambda l, j: (l, 0, j))],
        out_specs=pl.BlockSpec((1, rows, tn), lambda l, j: (l, 0, j)),
        out_shape=jax.ShapeDtypeStruct((depth, rows, nd), F32),
        compiler_params=_cparams(("parallel", "parallel")),
        name="ada_mods",
    )(cvecs, ada_w, ada_b.reshape(depth, 1, nd))


def _ffn_kernel(x_ref, sh_ref, sc_ref, gt_ref, gpre_ref, gpost_ref, win_ref, wout_ref, o_ref,
                *, ffn_dim, chunk):
    x = x_ref[0]
    xm = _norm_mod(x, gpre_ref[...], sh_ref[0], sc_ref[0]).astype(BF16)
    acc = jnp.zeros(x.shape, F32)
    for c in range(ffn_dim // chunk):
        a = _dot(xm, win_ref[:, c * chunk:(c + 1) * chunk])
        b = _dot(xm, win_ref[:, ffn_dim + c * chunk:ffn_dim + (c + 1) * chunk])
        h = (a * jax.nn.sigmoid(a) * b).astype(BF16)
        acc = acc + _dot(h, wout_ref[c * chunk:(c + 1) * chunk, :])
    o_ref[0] = x + FFN_RES * gt_ref[0] * _rms(acc, gpost_ref[...])


def _ffn(x, shift, scale, gate, g_pre, g_post, w_in, w_out):
    bsz, n, d = x.shape
    ffn_dim = w_out.shape[0]
    tm = min(512, n)
    tok = pl.BlockSpec((1, tm, d), lambda b, i: (b, i, 0))
    gain = pl.BlockSpec((1, d), lambda b, i: (0, 0))
    return pl.pallas_call(
        functools.partial(_ffn_kernel, ffn_dim=ffn_dim, chunk=256),
        grid=(bsz, n // tm),
        in_specs=[tok, _mod_spec(shift), _mod_spec(scale), _mod_spec(gate), gain, gain,
                  _resident(w_in.shape), _resident(w_out.shape)],
        out_specs=tok,
        out_shape=jax.ShapeDtypeStruct(x.shape, F32),
        compiler_params=_cparams(("parallel", "parallel")),
        name="ffn",
    )(x, shift, scale, gate, g_pre, g_post, w_in, w_out)


def _rope_masks(tm):
    lane = lax.broadcasted_iota(jnp.int32, (tm, LANES), 1)
    even = ((lane // (ROPE_DIM // 4)) & 1) == 0
    lo = lane < (LANES // 2)
    return even, lo


def _rope(blk, cos, sin, even):
    q = ROPE_DIM // 4
    partner = jnp.where(even, pltpu.roll(blk, LANES - q, 1), pltpu.roll(blk, q, 1))
    return blk * cos + partner * sin


def _store_vt(vt_ref, h, v_blk):
    tm = v_blk.shape[0]
    vt_ref[0, h, 0, 0:LANES, :] = v_blk.T.astype(BF16)
    row = lax.broadcasted_iota(jnp.int32, (V_ROWS - LANES, tm), 0)
    vt_ref[0, h, 0, LANES:V_ROWS, :] = jnp.where(row == 0, 1.0, 0.0).astype(BF16)


def _kv_specs(bsz, n, tm, heads, dk):
    chunk = min(KV_CHUNK, n)
    per = chunk // tm
    specs = [pl.BlockSpec((1, heads, tm, dk), lambda b, i: (b, 0, i, 0)),
             pl.BlockSpec((1, heads, 1, V_ROWS, tm), lambda b, i: (b, 0, i // per, 0, i % per))]
    shapes = [jax.ShapeDtypeStruct((bsz, heads, n, dk), BF16),
              jax.ShapeDtypeStruct((bsz, heads, n // chunk, V_ROWS, chunk), BF16)]
    return specs, shapes


def _proj_diff_kernel(x_ref, sh_ref, sc_ref, g_ref, cos_ref, sin_ref, w_ref,
                      qt_ref, k_ref, vt_ref):
    hm = _norm_mod(x_ref[0], g_ref[...], sh_ref[0], sc_ref[0]).astype(BF16)
    tm = hm.shape[0]
    cos = cos_ref[...]
    sin = sin_ref[...]
    even, _ = _rope_masks(tm)
    hw = DIFF_HEADS * 2 * DIFF_D
    q = _dot(hm, w_ref[:, 0:hw])
    k = _dot(hm, w_ref[:, hw:2 * hw])
    v = _dot(hm, w_ref[:, 2 * hw:3 * hw])
    scale = LOG2E / math.sqrt(DIFF_D)
    top = lax.broadcasted_iota(jnp.int32, (LANES, tm), 0) < DIFF_D
    for h in range(DIFF_HEADS):
        hs = slice(h * LANES, (h + 1) * LANES)
        qbt = (_rope(q[:, hs], cos, sin, even) * scale).T
        qt_ref[0, h, 0, :, 0:tm] = jnp.where(top, qbt, 0.0).astype(BF16)
        qt_ref[0, h, 0, :, tm:2 * tm] = jnp.where(top, 0.0, qbt).astype(BF16)
        k_ref[0, h] = _rope(k[:, hs], cos, sin, even).astype(BF16)
        _store_vt(vt_ref, h, v[:, hs])


def _proj_diff(x, shift, scale, g, cos, sin, w):
    bsz, n, d = x.shape
    tm = min(FLASH_TQ, n)
    tok = pl.BlockSpec((1, tm, d), lambda b, i: (b, i, 0))
    tab = pl.BlockSpec((tm, LANES), lambda b, i: (i, 0))
    kv_specs, kv_shapes = _kv_specs(bsz, n, tm, DIFF_HEADS, LANES)
    return pl.pallas_call(
        _proj_diff_kernel,
        grid=(bsz, n // tm),
        in_specs=[tok, _mod_spec(shift), _mod_spec(scale),
                  pl.BlockSpec((1, d), lambda b, i: (0, 0)), tab, tab, _resident(w.shape)],
        out_specs=[pl.BlockSpec((1, DIFF_HEADS, 1, LANES, 2 * tm), lambda b, i: (b, 0, i, 0, 0))]
        + kv_specs,
        out_shape=[jax.ShapeDtypeStruct((bsz, DIFF_HEADS, n // tm, LANES, 2 * tm), BF16)] + kv_shapes,
        compiler_params=_cparams(("parallel", "parallel")),
        name="proj_diff",
    )(x, shift, scale, g, cos, sin, w)


_WQ0, _WK0, _WV0 = 0, 1024, 1280
_CQ0, _CKV0, _KR0, _PB_COLS = 1536, 1920, 2176, 2304


def _proj_wm_kernel(x_ref, sh_ref, sc_ref, g_ref, cos_ref, sin_ref, w_ref, wuq_ref, wukv_ref,
                    qn_ref, kvn_ref, wqt_ref, wk_ref, wvt_ref, mqt_ref, mk_ref, mvt_ref):
    hm = _norm_mod(x_ref[0], g_ref[...], sh_ref[0], sc_ref[0]).astype(BF16)
    tm = hm.shape[0]
    cos = cos_ref[...]
    sin = sin_ref[...]
    even, lo = _rope_masks(tm)
    half = LANES // 2
    p = _dot(hm, w_ref[...])

    def two_heads(blk):
        return jnp.where(lo, blk, 0.0), jnp.where(lo, pltpu.roll(blk, half, 1), 0.0)

    for j in range(WIN_Q_HEADS // 2):
        blk = _rope(p[:, _WQ0 + j * LANES:_WQ0 + (j + 1) * LANES], cos, sin, even) * (WIN_SCALE * LOG2E)
        for sub, padded in enumerate(two_heads(blk)):
            kvh, grp = divmod(2 * j + sub, WIN_GROUP)
            qt = padded.T.astype(BF16)
            for qb in range(tm // QBLK):
                wqt_ref[0, kvh, qb, :, grp * QBLK:(grp + 1) * QBLK] = qt[:, qb * QBLK:(qb + 1) * QBLK]
    for j in range(WIN_KV_HEADS // 2):
        kblk = two_heads(_rope(p[:, _WK0 + j * LANES:_WK0 + (j + 1) * LANES], cos, sin, even))
        vblk = two_heads(p[:, _WV0 + j * LANES:_WV0 + (j + 1) * LANES])
        for sub in range(2):
            kvh = 2 * j + sub
            wk_ref[0, :, kvh * LANES:(kvh + 1) * LANES] = kblk[sub].astype(BF16)
            wvt_ref[0, kvh, 0:LANES, :] = vblk[sub].T.astype(BF16)
            row = lax.broadcasted_iota(jnp.int32, (V_ROWS - LANES, tm), 0)
            wvt_ref[0, kvh, LANES:V_ROWS, :] = jnp.where(row == 0, 1.0, 0.0).astype(BF16)

    cq = _rms(p[:, _CQ0:_CQ0 + MLA_Q_RANK], qn_ref[...]).astype(BF16)
    q2 = _dot(cq, wuq_ref[...])
    ckv = _rms(p[:, _CKV0:_CKV0 + MLA_KV_RANK], kvn_ref[...]).astype(BF16)
    kv = _dot(ckv, wukv_ref[...])
    kr = _rope(p[:, _KR0:_KR0 + LANES], cos, sin, even).astype(BF16)
    for h in range(MLA_HEADS):
        c0 = 2 * h * LANES
        qn = q2[:, c0:c0 + LANES] * (MLA_SCALE * LOG2E)
        qr = _rope(q2[:, c0 + LANES:c0 + 2 * LANES], cos, sin, even) * (MLA_SCALE * LOG2E)
        mqt_ref[0, h, 0, 0:LANES, :] = qn.T.astype(BF16)
        mqt_ref[0, h, 0, LANES:2 * LANES, :] = qr.T.astype(BF16)
        mk_ref[0, h, :, 0:LANES] = kv[:, c0:c0 + LANES].astype(BF16)
        mk_ref[0, h, :, LANES:2 * LANES] = kr
        _store_vt(mvt_ref, h, kv[:, c0 + LANES:c0 + 2 * LANES])


def _proj_wm(x, shift, scale, g, cos, sin, w, wuq, wukv, qn, kvn):
    bsz, n, d = x.shape
    tm = min(FLASH_TQ, n)
    tok = lambda c: pl.BlockSpec((1, tm, c), lambda b, i: (b, i, 0))
    tab = pl.BlockSpec((tm, LANES), lambda b, i: (i, 0))
    row = lambda c: pl.BlockSpec((1, c), lambda b, i: (0, 0))
    kv_specs, kv_shapes = _kv_specs(bsz, n, tm, MLA_HEADS, 2 * LANES)
    qrow = WIN_GROUP * QBLK
    win_specs = [pl.BlockSpec((1, WIN_KV_HEADS, tm // QBLK, LANES, qrow), lambda b, i: (b, 0, i, 0, 0)),
                 tok(WIN_KV_HEADS * LANES),
                 pl.BlockSpec((1, WIN_KV_HEADS, V_ROWS, tm), lambda b, i: (b, 0, 0, i))]
    win_shapes = [jax.ShapeDtypeStruct((bsz, WIN_KV_HEADS, n // QBLK, LANES, qrow), BF16),
                  jax.ShapeDtypeStruct((bsz, n, WIN_KV_HEADS * LANES), BF16),
                  jax.ShapeDtypeStruct((bsz, WIN_KV_HEADS, V_ROWS, n), BF16)]
    return pl.pallas_call(
        _proj_wm_kernel,
        grid=(bsz, n // tm),
        in_specs=[tok(d), _mod_spec(shift), _mod_spec(scale), row(d), tab, tab,
                  _resident(w.shape), _resident(wuq.shape), _resident(wukv.shape),
                  row(MLA_Q_RANK), row(MLA_KV_RANK)],
        out_specs=win_specs
        + [pl.BlockSpec((1, MLA_HEADS, 1, 2 * LANES, tm), lambda b, i: (b, 0, i, 0, 0))] + kv_specs,
        out_shape=win_shapes
        + [jax.ShapeDtypeStruct((bsz, MLA_HEADS, n // tm, 2 * LANES, tm), BF16)] + kv_shapes,
        compiler_params=_cparams(("parallel", "parallel")),
        name="proj_wm",
    )(x, shift, scale, g, cos, sin, w, wuq, wukv, qn, kvn)


def _segments(seg_refs):
    segs = []
    for k_ref, vt_ref in zip(seg_refs[0::2], seg_refs[1::2]):
        segs.append((k_ref, vt_ref, vt_ref.shape[2], vt_ref.shape[4]))
    return segs


def _item_rows(tk, w):
    return min(tk, max(FLASH_ITEM_ELEMS // w, 2 * LANES))


def _flash_loop_exact(qt, segs, m_sc, acc_sc):
    m_sc[...] = jnp.full(m_sc.shape, NEG_INF, F32)
    acc_sc[...] = jnp.zeros(acc_sc.shape, F32)
    for k_ref, vt_ref, nc, tk in segs:
        rows = _item_rows(tk, qt.shape[1])

        def body(j, carry, k_ref=k_ref, vt_ref=vt_ref, tk=tk, rows=rows):
            for r0 in range(0, tk, rows):
                start = pl.multiple_of(j * tk + r0, rows)
                s = _dot(k_ref[0, 0, pl.ds(start, rows), :], qt)
                m_prev = m_sc[...]
                m_new = jnp.maximum(m_prev, jnp.max(s, axis=0, keepdims=True))
                p = jnp.exp2(s - m_new).astype(BF16)
                acc_sc[...] = (jnp.exp2(m_prev - m_new) * acc_sc[...]
                               + _dot(vt_ref[0, 0, j][:, r0:r0 + rows], p))
                m_sc[...] = m_new
            return carry
        lax.fori_loop(0, nc, body, 0)


def _flash_loop_spec(qt, segs, acc_sc):
    probe = _dot(segs[0][0][0, 0, 0:SPEC_PROBE_ROWS, :], qt)
    m0 = jnp.max(probe, axis=0, keepdims=True)
    acc_sc[...] = jnp.zeros(acc_sc.shape, F32)
    items = []
    for k_ref, vt_ref, nc, tk in segs:
        rows = _item_rows(tk, qt.shape[1])
        items += [(k_ref, vt_ref, j, j * tk + r0, r0, rows) for j in range(nc) for r0 in range(0, tk, rows)]

    def scores(item):
        k_ref, _, _, row0, _, rows = item
        return _dot(k_ref[0, 0, row0:row0 + rows, :], qt)

    s = scores(items[0])
    for idx, (_, vt_ref, j, _, r0, rows) in enumerate(items):
        pf = jnp.exp2(s - m0)
        lsum = jnp.sum(pf, axis=0, keepdims=True)
        p = pf.astype(BF16)
        if idx + 1 < len(items):
            s = scores(items[idx + 1])
        acc_sc[0:LANES, :] += _dot(vt_ref[0, 0, j][0:LANES, r0:r0 + rows], p)
        acc_sc[LANES:LANES + 1, :] += lsum


def _flash_tile(qt, segs, m_sc, acc_sc):
    _flash_loop_spec(qt, segs, acc_sc)
    finite = jnp.where(jnp.abs(acc_sc[0:LANES + 1, :]) <= SPEC_FINITE_LIMIT, 1.0, 0.0)

    @pl.when(jnp.min(finite) < 0.5)
    def _redo():
        _flash_loop_exact(qt, segs, m_sc, acc_sc)


def _diff_flash_kernel(*refs, nseg, lambda_init):
    qt_ref, seg_refs = refs[0], refs[1:1 + 2 * nseg]
    dl_ref, g_ref, o_ref, m_sc, acc_sc = refs[1 + 2 * nseg:]
    tiles = qt_ref.shape[2]
    qt = jnp.concatenate([qt_ref[0, 0, t] for t in range(tiles)], axis=1)
    _flash_tile(qt, _segments(seg_refs), m_sc, acc_sc)
    tq = o_ref.shape[1] // tiles
    dl = dl_ref[...]
    lam = (jnp.exp(jnp.sum(dl[0:1] * dl[1:2], axis=1, keepdims=True))
           - jnp.exp(jnp.sum(dl[2:3] * dl[3:4], axis=1, keepdims=True)) + lambda_init)
    for t in range(tiles):
        lanes = slice(2 * t * tq, 2 * (t + 1) * tq)
        o = acc_sc[0:LANES, lanes] / acc_sc[LANES:LANES + 1, lanes]
        y = (o[:, :tq] - lam * o[:, tq:]).T
        o_ref[0, t * tq:(t + 1) * tq, :] = (_rms(y, g_ref[...]) * (1.0 - lambda_init)).astype(BF16)


def _mla_flash_kernel(*refs, nseg):
    qt_ref, seg_refs = refs[0], refs[1:1 + 2 * nseg]
    o_ref, m_sc, acc_sc = refs[1 + 2 * nseg:]
    qt = jnp.concatenate([qt_ref[0, 0, t] for t in range(qt_ref.shape[2])], axis=1)
    _flash_tile(qt, _segments(seg_refs), m_sc, acc_sc)
    o_ref[0] = (acc_sc[0:LANES, :] / acc_sc[LANES:LANES + 1, :]).T.astype(BF16)


def _flash(mode, qt, kv_segs, extra=(), lambda_init=0.0):
    bsz, heads, nq, dk, w = qt.shape
    maps = 2 if mode == "diff" else 1
    tps = FLASH_TILES_PER_STEP if nq % FLASH_TILES_PER_STEP == 0 else 1
    tq = tps * w // maps
    nq //= tps
    in_specs = [pl.BlockSpec((1, 1, tps, dk, w), lambda b, h, i: (b, h, i, 0, 0))]
    w *= tps
    args = [qt]
    for k, vt in kv_segs:
        in_specs += [pl.BlockSpec((1, 1) + k.shape[2:], lambda b, h, i: (b, h, 0, 0)),
                     pl.BlockSpec((1, 1) + vt.shape[2:], lambda b, h, i: (b, h, 0, 0, 0))]
        args += [k, vt]
    if mode == "diff":
        kern = functools.partial(_diff_flash_kernel, nseg=len(kv_segs), lambda_init=lambda_init)
        in_specs += [pl.BlockSpec(e.shape, lambda b, h, i: (0, 0)) for e in extra]
    else:
        kern = functools.partial(_mla_flash_kernel, nseg=len(kv_segs))
    return pl.pallas_call(
        kern,
        grid=(bsz, heads, nq),
        in_specs=in_specs,
        out_specs=pl.BlockSpec((1, tq, LANES), lambda b, h, i: (b, i, h)),
        out_shape=jax.ShapeDtypeStruct((bsz, nq * tq, heads * LANES), BF16),
        scratch_shapes=[pltpu.VMEM((1, w), F32), pltpu.VMEM((V_ROWS, w), F32)],
        compiler_params=_cparams(("parallel", "parallel", "arbitrary")),
        name=f"{mode}_flash",
    )(*args, *extra)


def _win_kernel(*refs, band, nblk):
    if band:
        (q_ref, kp_ref, ko_ref, kn_ref, vp_ref, vo_ref, vn_ref, kc_ref, vc_ref,
         sink_ref, o_ref) = refs
    else:
        q_ref, kc_ref, vc_ref, sink_ref, o_ref = refs
    t = QBLK
    w = WIN_GROUP * t
    if band:
        i = pl.program_id(1)
        kk = lax.broadcasted_iota(jnp.int32, (t, w), 0)
        qq = lax.broadcasted_iota(jnp.int32, (t, w), 1) % t
        ok_prev = kk >= qq + jnp.where(i > 0, 0, t)
        ok_next = kk <= qq - jnp.where(i < nblk - 1, 0, t)
    parts, sinks, maxes = [], [], []
    for kvh in range(WIN_KV_HEADS):
        ksl = slice(kvh * LANES, (kvh + 1) * LANES)
        h0 = kvh * WIN_GROUP
        qt = q_ref[0, kvh, 0]
        sink = jnp.concatenate([sink_ref[h0 + g:h0 + g + 1, :] for g in range(WIN_GROUP)],
                               axis=1) * LOG2E
        s_all = [_dot(kc_ref[0, :, ksl], qt)]
        if band:
            s_all = [jnp.where(ok_prev, _dot(kp_ref[0, :, ksl], qt), NEG_INF),
                     _dot(ko_ref[0, :, ksl], qt),
                     jnp.where(ok_next, _dot(kn_ref[0, :, ksl], qt), NEG_INF)] + s_all
        m = sink
        for part in s_all:
            m = jnp.maximum(m, jnp.max(part, axis=0, keepdims=True))
        parts.append(s_all)
        sinks.append(sink)
        maxes.append(m)
    outs = []
    for kvh in range(WIN_KV_HEADS):
        m, sink = maxes[kvh], sinks[kvh]
        p = [jnp.exp2(part - m).astype(BF16) for part in parts[kvh]]
        if band:
            p = jnp.concatenate(p, axis=0)
            vt = jnp.concatenate([vp_ref[0, kvh], vo_ref[0, kvh], vn_ref[0, kvh], vc_ref[0, kvh]],
                                 axis=1)
        else:
            p, vt = p[0], vc_ref[0, kvh]
        pv = _dot(vt, p)
        outs.append(pv[0:LANES, :] / (pv[LANES:LANES + 1, :] + jnp.exp2(sink - m)))
    for kvh in range(WIN_KV_HEADS):
        o = outs[kvh]
        for pair in range(WIN_GROUP // 2):
            a = o[0:WIN_HEAD_DIM, (2 * pair) * t:(2 * pair + 1) * t]
            b = o[0:WIN_HEAD_DIM, (2 * pair + 1) * t:(2 * pair + 2) * t]
            blk = kvh * (WIN_GROUP // 2) + pair
            o_ref[0, :, blk * LANES:(blk + 1) * LANES] = jnp.concatenate([a, b], axis=0).T.astype(BF16)


def _win_attn(qt, k, vt, kc, vtc, sink, band):
    bsz, _, nblk, _, qrow = qt.shape
    kw = kc.shape[2]
    nctx = kc.shape[1]
    t = QBLK
    out_w = WIN_Q_HEADS * WIN_HEAD_DIM
    own = lambda b, i: (b, i, 0)
    in_specs = [pl.BlockSpec((1, WIN_KV_HEADS, 1, LANES, qrow), lambda b, i: (b, 0, i, 0, 0))]
    args = [qt]
    if band:
        prev = lambda b, i: (b, jnp.maximum(i - 1, 0), 0)
        nxt = lambda b, i: (b, jnp.minimum(i + 1, nblk - 1), 0)
        kspec = lambda f: pl.BlockSpec((1, t, kw), f)
        vspec = lambda f: pl.BlockSpec((1, WIN_KV_HEADS, V_ROWS, t),
                                       lambda b, i, f=f: (b, 0, 0, f(b, i)[1]))
        in_specs += [kspec(prev), kspec(own), kspec(nxt), vspec(prev), vspec(own), vspec(nxt)]
        args += [k, k, k, vt, vt, vt]
    in_specs += [pl.BlockSpec((1, nctx, kw), lambda b, i: (b, 0, 0)),
                 pl.BlockSpec((1, WIN_KV_HEADS, V_ROWS, nctx), lambda b, i: (b, 0, 0, 0)),
                 pl.BlockSpec(sink.shape, lambda b, i: (0, 0))]
    args += [kc, vtc, sink]
    return pl.pallas_call(
        functools.partial(_win_kernel, band=band, nblk=nblk),
        grid=(bsz, nblk),
        in_specs=in_specs,
        out_specs=pl.BlockSpec((1, t, out_w), own),
        out_shape=jax.ShapeDtypeStruct((bsz, nblk * t, out_w), BF16),
        compiler_params=_cparams(("parallel", "parallel")),
        name="win_attn" if band else "win_attn_ctx",
    )(*args)


def _merge_kernel(x_ref, ya_ref, yb_ref, yc_ref, sh_ref, sc_ref, gt_ref, g2_ref, g3_ref,
                  wg_ref, bw_ref, wo_ref, o_ref):
    x = x_ref[0]
    d = x.shape[1]
    hm = _norm_mod(x, g2_ref[...], sh_ref[0], sc_ref[0]).astype(BF16)
    merged = jnp.zeros(x.shape, F32)
    for i, y_ref in enumerate((ya_ref, yb_ref, yc_ref)):
        gate = jax.nn.sigmoid(_dot(hm, wg_ref[:, i * d:(i + 1) * d]))
        merged = merged + gate * _dot(y_ref[0], bw_ref[i])
    y = _dot(merged.astype(BF16), wo_ref[...])
    o_ref[0] = x + gt_ref[0] * _rms(y, g3_ref[...])


def _merge(x, ya, yb, yc, shift, scale, gate, g2, g3, wg, bw, wo):
    bsz, n, d = x.shape
    tm = min(512, n)
    tok = pl.BlockSpec((1, tm, d), lambda b, i: (b, i, 0))
    gain = pl.BlockSpec((1, d), lambda b, i: (0, 0))
    return pl.pallas_call(
        _merge_kernel,
        grid=(bsz, n // tm),
        in_specs=[tok, tok, tok, tok, _mod_spec(shift), _mod_spec(scale), _mod_spec(gate),
                  gain, gain, _resident(wg.shape), _resident(bw.shape), _resident(wo.shape)],
        out_specs=tok,
        out_shape=jax.ShapeDtypeStruct(x.shape, F32),
        compiler_params=_cparams(("parallel", "parallel")),
        name="merge",
    )(x, ya, yb, yc, shift, scale, gate, g2, g3, wg, bw, wo)


def _rope_tables(n):
    quarter = ROPE_DIM // 4
    pos = jnp.arange(n)
    row = (pos // GRID_W).astype(F32)
    col = (pos % GRID_W).astype(F32)
    inv_freq = 1.0 / (ROPE_BASE ** (jnp.arange(quarter, dtype=F32) / quarter))
    ang_r = row[:, None] * inv_freq
    ang_c = col[:, None] * inv_freq
    cos = jnp.concatenate([jnp.cos(ang_r)] * 2 + [jnp.cos(ang_c)] * 2, axis=1)
    sin = jnp.concatenate([-jnp.sin(ang_r), jnp.sin(ang_r), -jnp.sin(ang_c), jnp.sin(ang_c)], axis=1)
    return jnp.tile(cos, (1, 2)), jnp.tile(sin, (1, 2))


def _layer_weights(l, ffn_w_in, ffn_w_out, mix_w_in, mla_w_uq, mla_w_ukv, branch_w, mix_w_out):
    d = mix_w_in.shape[1]
    w = mix_w_in[l]
    hw = DIFF_HEADS * 2 * DIFF_D
    wq_n = WIN_Q_HEADS * WIN_HEAD_DIM
    wk_n = WIN_KV_HEADS * WIN_HEAD_DIM
    c = 3 * hw
    w_diff = w[:, :c]
    rest_n = wq_n + 2 * wk_n + MLA_Q_RANK + MLA_KV_RANK + MLA_ROPE
    w_wm = jnp.concatenate([w[:, c:c + rest_n], jnp.zeros((d, _PB_COLS - rest_n), w.dtype)], axis=1)
    w_gate = w[:, c + rest_n:]
    wuq = mla_w_uq[l].reshape(MLA_Q_RANK, MLA_HEADS, MLA_NOPE + MLA_ROPE)
    wuq = jnp.pad(wuq, ((0, 0), (0, 0), (0, 2 * LANES - MLA_NOPE - MLA_ROPE)))
    wuq = wuq.reshape(MLA_Q_RANK, MLA_HEADS * 2 * LANES)
    return dict(
        ffn_in=[ffn_w_in[l, i].astype(BF16) for i in range(2)],
        ffn_out=[ffn_w_out[l, i].astype(BF16) for i in range(2)],
        w_diff=w_diff.astype(BF16), w_wm=w_wm.astype(BF16), w_gate=w_gate.astype(BF16),
        wuq=wuq.astype(BF16), wukv=mla_w_ukv[l].astype(BF16),
        bw=branch_w[l].astype(BF16), wo=mix_w_out[l].astype(BF16))


def kernel(x, c, ctx, c_ctx, ada_w, ada_b, norm_g, ffn_w_in, ffn_w_out, mix_w_in, diff_lambda,
           diff_subln_g, win_sink, mla_q_norm_g, mla_kv_norm_g, mla_w_uq, mla_w_ukv, branch_w,
           mix_w_out):
    bsz, s, d = x.shape
    nctx = ctx.shape[1]
    depth = ada_w.shape[0]

    cvecs = jnp.zeros((8, d), F32).at[:bsz].set(c).at[bsz].set(c_ctx)
    mods = _ada_mods(cvecs, ada_w, ada_b).reshape(depth, 8, N_MOD, d)

    cos_x, sin_x = _rope_tables(s)
    cos_c = jnp.ones((nctx, LANES), F32)
    sin_c = jnp.zeros((nctx, LANES), F32)

    h = ctx
    for l in range(depth):
        last = l == depth - 1
        lambda_init = 0.8 - 0.6 * math.exp(-0.3 * l)
        wts = _layer_weights(l, ffn_w_in, ffn_w_out, mix_w_in, mla_w_uq, mla_w_ukv, branch_w,
                             mix_w_out)
        mx = [mods[l, :bsz, k][:, None, :] for k in range(N_MOD)]
        mc = [mods[l, bsz:bsz + 1, k][:, None, :] for k in range(N_MOD)]
        g = [norm_g[l, k][None, :] for k in range(6)]
        qn = mla_q_norm_g[l][None, :]
        kvn = mla_kv_norm_g[l][None, :]
        dl = diff_lambda[l].astype(F32)
        subln = diff_subln_g[l][None, :]
        sink = jnp.broadcast_to(win_sink[l].astype(F32)[:, None], (WIN_Q_HEADS, LANES))

        x = _ffn(x, mx[0], mx[1], mx[2], g[0], g[1], wts["ffn_in"][0], wts["ffn_out"][0])
        h = _ffn(h, mc[0], mc[1], mc[2], g[0], g[1], wts["ffn_in"][0], wts["ffn_out"][0])

        dqt_x, dk_x, dvt_x = _proj_diff(x, mx[3], mx[4], g[2], cos_x, sin_x, wts["w_diff"])
        dqt_c, dk_c, dvt_c = _proj_diff(h, mc[3], mc[4], g[2], cos_c, sin_c, wts["w_diff"])
        wqt_x, wk_x, wvt_x, mqt_x, mk_x, mvt_x = _proj_wm(
            x, mx[3], mx[4], g[2], cos_x, sin_x, wts["w_wm"], wts["wuq"], wts["wukv"], qn, kvn)
        wqt_c, wk_c, wvt_c, mqt_c, mk_c, mvt_c = _proj_wm(
            h, mc[3], mc[4], g[2], cos_c, sin_c, wts["w_wm"], wts["wuq"], wts["wukv"], qn, kvn)

        ya = _flash("diff", dqt_x, [(dk_x, dvt_x), (dk_c, dvt_c)], (dl, subln), lambda_init)
        yb = _win_attn(wqt_x, wk_x, wvt_x, wk_c, wvt_c, sink, band=True)
        yc = _flash("mla", mqt_x, [(mk_x, mvt_x), (mk_c, mvt_c)])
        x_new = _merge(x, ya, yb, yc, mx[3], mx[4], mx[5], g[2], g[3],
                       wts["w_gate"], wts["bw"], wts["wo"])
        if not last:
            ca = _flash("diff", dqt_c, [(dk_c, dvt_c)], (dl, subln), lambda_init)
            cb = _win_attn(wqt_c, None, None, wk_c, wvt_c, sink, band=False)
            cc = _flash("mla", mqt_c, [(mk_c, mvt_c)])
            h = _merge(h, ca, cb, cc, mc[3], mc[4], mc[5], g[2], g[3],
                       wts["w_gate"], wts["bw"], wts["wo"])
            h = _ffn(h, mc[6], mc[7], mc[8], g[4], g[5], wts["ffn_in"][1], wts["ffn_out"][1])
        x = _ffn(x_new, mx[6], mx[7], mx[8], g[4], g[5], wts["ffn_in"][1], wts["ffn_out"][1])
    return x
```

```python
import functools
import math

import jax
import jax.numpy as jnp
from jax import lax
from jax.experimental import pallas as pl
from jax.experimental.pallas import tpu as pltpu

F32 = jnp.float32
BF16 = jnp.bfloat16

GRID_W = 64
QBLK = 128
EPS = 1e-6
NEG_INF = -1e30
ROPE_DIM = 64
ROPE_BASE = 10000.0
N_MOD = 9
FFN_RES = 0.5
DIFF_HEADS = 8
DIFF_D = 64
WIN_Q_HEADS = 16
WIN_KV_HEADS = 4
WIN_GROUP = 4
WIN_HEAD_DIM = 64
WIN_SCALE = WIN_HEAD_DIM ** -0.5
MLA_HEADS = 8
MLA_Q_RANK = 384
MLA_KV_RANK = 256
MLA_NOPE = 128
MLA_ROPE = 64
MLA_V = 128
MLA_SCALE = (MLA_NOPE + MLA_ROPE) ** -0.5
LOG2E = math.log2(math.e)

LANES = 128
VMEM_LIMIT = 56 * 1024 * 1024

FLASH_TQ = 512
FLASH_TILES_PER_STEP = 4
FLASH_ITEM_ELEMS = 2048 * 1024
KV_CHUNK = 2048
V_ROWS = LANES + 16
SPEC_FINITE_LIMIT = 3.0e38
SPEC_PROBE_ROWS = 128


def _cparams(sem):
    return pltpu.CompilerParams(dimension_semantics=sem, vmem_limit_bytes=VMEM_LIMIT)


def _resident(shape):
    nd = len(shape)
    return pl.BlockSpec(shape, lambda *_: (0,) * nd, pipeline_mode=pl.Buffered(1))


def _rms(x, g):
    return x * lax.rsqrt(jnp.mean(x * x, axis=-1, keepdims=True) + EPS) * g


def _norm_mod(x, g, shift, scale):
    return _rms(x, g) * (1.0 + scale) + shift


def _dot(a, b):
    return jnp.dot(a, b, preferred_element_type=F32)


def _dot_nt(a, b):
    return lax.dot_general(a, b, (((1,), (1,)), ((), ())), preferred_element_type=F32)


def _mod_spec(arr):
    d = arr.shape[-1]
    if arr.shape[0] == 1:
        return pl.BlockSpec((1, 1, d), lambda b, *_: (0, 0, 0))
    return pl.BlockSpec((1, 1, d), lambda b, *_: (b, 0, 0))


def _ada_kernel(c_ref, w_ref, b_ref, o_ref):
    c = c_ref[...]
    a = c * jax.nn.sigmoid(c)
    o_ref[0] = jnp.dot(a, w_ref[0], preferred_element_type=F32,
                       precision=lax.Precision.HIGHEST) + b_ref[0]


def _ada_mods(cvecs, ada_w, ada_b):
    depth, d, nd = ada_w.shape
    rows = cvecs.shape[0]
    tn = 1152 if nd % 1152 == 0 else nd
    return pl.pallas_call(
        _ada_kernel,
        grid=(depth, nd // tn),
        in_specs=[pl.BlockSpec((rows, d), lambda l, j: (0, 0)),
                  pl.BlockSpec((1, d, tn), lambda l, j: (l, 0, j)),
                  pl.BlockSpec((1, 1, tn), lambda l, j: (l, 0, j))],
        out_specs=pl.BlockSpec((1, rows, tn), lambda l, j: (l, 0, j)),
        out_shape=jax.ShapeDtypeStruct((depth, rows, nd), F32),
        compiler_params=_cparams(("parallel", "parallel")),
        name="ada_mods",
    )(cvecs, ada_w, ada_b.reshape(depth, 1, nd))


def _ffn_kernel(x_ref, sh_ref, sc_ref, gt_ref, gpre_ref, gpost_ref, win_ref, wout_ref, o_ref,
                *, ffn_dim, chunk):
    x = x_ref[0]
    xm = _norm_mod(x, gpre_ref[...], sh_ref[0], sc_ref[0]).astype(BF16)
    acc = jnp.zeros(x.shape, F32)
    for c in range(ffn_dim // chunk):
        a = _dot(xm, win_ref[:, c * chunk:(c + 1) * chunk])
        b = _dot(xm, win_ref[:, ffn_dim + c * chunk:ffn_dim + (c + 1) * chunk])
        h = (a * jax.nn.sigmoid(a) * b).astype(BF16)
        acc = acc + _dot(h, wout_ref[c * chunk:(c + 1) * chunk, :])
    o_ref[0] = x + FFN_RES * gt_ref[0] * _rms(acc, gpost_ref[...])


def _ffn(x, shift, scale, gate, g_pre, g_post, w_in, w_out):
    bsz, n, d = x.shape
    ffn_dim = w_out.shape[0]
    tm = min(512, n)
    tok = pl.BlockSpec((1, tm, d), lambda b, i: (b, i, 0))
    gain = pl.BlockSpec((1, d), lambda b, i: (0, 0))
    return pl.pallas_call(
        functools.partial(_ffn_kernel, ffn_dim=ffn_dim, chunk=256),
        grid=(bsz, n // tm),
        in_specs=[tok, _mod_spec(shift), _mod_spec(scale), _mod_spec(gate), gain, gain,
                  _resident(w_in.shape), _resident(w_out.shape)],
        out_specs=tok,
        out_shape=jax.ShapeDtypeStruct(x.shape, F32),
        compiler_params=_cparams(("parallel", "parallel")),
        name="ffn",
    )(x, shift, scale, gate, g_pre, g_post, w_in, w_out)


def _rope_masks(tm):
    lane = lax.broadcasted_iota(jnp.int32, (tm, LANES), 1)
    even = ((lane // (ROPE_DIM // 4)) & 1) == 0
    lo = lane < (LANES // 2)
    return even, lo


def _rope(blk, cos, sin, even):
    q = ROPE_DIM // 4
    partner = jnp.where(even, pltpu.roll(blk, LANES - q, 1), pltpu.roll(blk, q, 1))
    return blk * cos + partner * sin


def _store_vt(vt_ref, h, v_blk):
    tm = v_blk.shape[0]
    vt_ref[0, h, 0, 0:LANES, :] = v_blk.T.astype(BF16)
    row = lax.broadcasted_iota(jnp.int32, (V_ROWS - LANES, tm), 0)
    vt_ref[0, h, 0, LANES:V_ROWS, :] = jnp.where(row == 0, 1.0, 0.0).astype(BF16)


def _kv_specs(bsz, n, tm, heads, dk):
    chunk = min(KV_CHUNK, n)
    per = chunk // tm
    specs = [pl.BlockSpec((1, heads, tm, dk), lambda b, i: (b, 0, i, 0)),
             pl.BlockSpec((1, heads, 1, V_ROWS, tm), lambda b, i: (b, 0, i // per, 0, i % per))]
    shapes = [jax.ShapeDtypeStruct((bsz, heads, n, dk), BF16),
              jax.ShapeDtypeStruct((bsz, heads, n // chunk, V_ROWS, chunk), BF16)]
    return specs, shapes


def _proj_diff_kernel(x_ref, sh_ref, sc_ref, g_ref, cos_ref, sin_ref, w_ref,
                      qt_ref, k_ref, vt_ref):
    hm = _norm_mod(x_ref[0], g_ref[...], sh_ref[0], sc_ref[0]).astype(BF16)
    tm = hm.shape[0]
    cos = cos_ref[...]
    sin = sin_ref[...]
    even, _ = _rope_masks(tm)
    hw = DIFF_HEADS * 2 * DIFF_D
    q = _dot(hm, w_ref[:, 0:hw])
    k = _dot(hm, w_ref[:, hw:2 * hw])
    v = _dot(hm, w_ref[:, 2 * hw:3 * hw])
    scale = LOG2E / math.sqrt(DIFF_D)
    top = lax.broadcasted_iota(jnp.int32, (LANES, tm), 0) < DIFF_D
    for h in range(DIFF_HEADS):
        hs = slice(h * LANES, (h + 1) * LANES)
        qbt = (_rope(q[:, hs], cos, sin, even) * scale).T
        qt_ref[0, h, 0, :, 0:tm] = jnp.where(top, qbt, 0.0).astype(BF16)
        qt_ref[0, h, 0, :, tm:2 * tm] = jnp.where(top, 0.0, qbt).astype(BF16)
        k_ref[0, h] = _rope(k[:, hs], cos, sin, even).astype(BF16)
        _store_vt(vt_ref, h, v[:, hs])


def _proj_diff(x, shift, scale, g, cos, sin, w):
    bsz, n, d = x.shape
    tm = min(FLASH_TQ, n)
    tok = pl.BlockSpec((1, tm, d), lambda b, i: (b, i, 0))
    tab = pl.BlockSpec((tm, LANES), lambda b, i: (i, 0))
    kv_specs, kv_shapes = _kv_specs(bsz, n, tm, DIFF_HEADS, LANES)
    return pl.pallas_call(
        _proj_diff_kernel,
        grid=(bsz, n // tm),
        in_specs=[tok, _mod_spec(shift), _mod_spec(scale),
                  pl.BlockSpec((1, d), lambda b, i: (0, 0)), tab, tab, _resident(w.shape)],
        out_specs=[pl.BlockSpec((1, DIFF_HEADS, 1, LANES, 2 * tm), lambda b, i: (b, 0, i, 0, 0))]
        + kv_specs,
        out_shape=[jax.ShapeDtypeStruct((bsz, DIFF_HEADS, n // tm, LANES, 2 * tm), BF16)] + kv_shapes,
        compiler_params=_cparams(("parallel", "parallel")),
        name="proj_diff",
    )(x, shift, scale, g, cos, sin, w)


_WQ0, _WK0, _WV0 = 0, 1024, 1280
_CQ0, _CKV0, _KR0, _PB_COLS = 1536, 1920, 2176, 2304


def _proj_wm_kernel(x_ref, sh_ref, sc_ref, g_ref, cos_ref, sin_ref, w_ref, wuq_ref, wukv_ref,
                    qn_ref, kvn_ref, wqt_ref, wk_ref, wvt_ref, mqt_ref, mk_ref, mvt_ref):
    hm = _norm_mod(x_ref[0], g_ref[...], sh_ref[0], sc_ref[0]).astype(BF16)
    tm = hm.shape[0]
    cos = cos_ref[...]
    sin = sin_ref[...]
    even, lo = _rope_masks(tm)
    half = LANES // 2
    p = _dot(hm, w_ref[...])

    def two_heads(blk):
        return jnp.where(lo, blk, 0.0), jnp.where(lo, pltpu.roll(blk, half, 1), 0.0)

    for j in range(WIN_Q_HEADS // 2):
        blk = _rope(p[:, _WQ0 + j * LANES:_WQ0 + (j + 1) * LANES], cos, sin, even) * (WIN_SCALE * LOG2E)
        for sub, padded in enumerate(two_heads(blk)):
            kvh, grp = divmod(2 * j + sub, WIN_GROUP)
            qt = padded.T.astype(BF16)
            for qb in range(tm // QBLK):
                wqt_ref[0, kvh, qb, :, grp * QBLK:(grp + 1) * QBLK] = qt[:, qb * QBLK:(qb + 1) * QBLK]
    for j in range(WIN_KV_HEADS // 2):
        kblk = two_heads(_rope(p[:, _WK0 + j * LANES:_WK0 + (j + 1) * LANES], cos, sin, even))
        vblk = two_heads(p[:, _WV0 + j * LANES:_WV0 + (j + 1) * LANES])
        for sub in range(2):
            kvh = 2 * j + sub
            wk_ref[0, :, kvh * LANES:(kvh + 1) * LANES] = kblk[sub].astype(BF16)
            wvt_ref[0, kvh, 0:LANES, :] = vblk[sub].T.astype(BF16)
            row = lax.broadcasted_iota(jnp.int32, (V_ROWS - LANES, tm), 0)
            wvt_ref[0, kvh, LANES:V_ROWS, :] = jnp.where(row == 0, 1.0, 0.0).astype(BF16)

    cq = _rms(p[:, _CQ0:_CQ0 + MLA_Q_RANK], qn_ref[...]).astype(BF16)
    q2 = _dot(cq, wuq_ref[...])
    ckv = _rms(p[:, _CKV0:_CKV0 + MLA_KV_RANK], kvn_ref[...]).astype(BF16)
    kv = _dot(ckv, wukv_ref[...])
    kr = _rope(p[:, _KR0:_KR0 + LANES], cos, sin, even).astype(BF16)
    for h in range(MLA_HEADS):
        c0 = 2 * h * LANES
        qn = q2[:, c0:c0 + LANES] * (MLA_SCALE * LOG2E)
        qr = _rope(q2[:, c0 + LANES:c0 + 2 * LANES], cos, sin, even) * (MLA_SCALE * LOG2E)
        mqt_ref[0, h, 0, 0:LANES, :] = qn.T.astype(BF16)
        mqt_ref[0, h, 0, LANES:2 * LANES, :] = qr.T.astype(BF16)
        mk_ref[0, h, :, 0:LANES] = kv[:, c0:c0 + LANES].astype(BF16)
        mk_ref[0, h, :, LANES:2 * LANES] = kr
        _store_vt(mvt_ref, h, kv[:, c0 + LANES:c0 + 2 * LANES])


def _proj_wm(x, shift, scale, g, cos, sin, w, wuq, wukv, qn, kvn):
    bsz, n, d = x.shape
    tm = min(FLASH_TQ, n)
    tok = lambda c: pl.BlockSpec((1, tm, c), lambda b, i: (b, i, 0))
    tab = pl.BlockSpec((tm, LANES), lambda b, i: (i, 0))
    row = lambda c: pl.BlockSpec((1, c), lambda b, i: (0, 0))
    kv_specs, kv_shapes = _kv_specs(bsz, n, tm, MLA_HEADS, 2 * LANES)
    qrow = WIN_GROUP * QBLK
    win_specs = [pl.BlockSpec((1, WIN_KV_HEADS, tm // QBLK, LANES, qrow), lambda b, i: (b, 0, i, 0, 0)),
                 tok(WIN_KV_HEADS * LANES),
                 pl.BlockSpec((1, WIN_KV_HEADS, V_ROWS, tm), lambda b, i: (b, 0, 0, i))]
    win_shapes = [jax.ShapeDtypeStruct((bsz, WIN_KV_HEADS, n // QBLK, LANES, qrow), BF16),
                  jax.ShapeDtypeStruct((bsz, n, WIN_KV_HEADS * LANES), BF16),
                  jax.ShapeDtypeStruct((bsz, WIN_KV_HEADS, V_ROWS, n), BF16)]
    return pl.pallas_call(
        _proj_wm_kernel,
        grid=(bsz, n // tm),
        in_specs=[tok(d), _mod_spec(shift), _mod_spec(scale), row(d), tab, tab,
                  _resident(w.shape), _resident(wuq.shape), _resident(wukv.shape),
                  row(MLA_Q_RANK), row(MLA_KV_RANK)],
        out_specs=win_specs
        + [pl.BlockSpec((1, MLA_HEADS, 1, 2 * LANES, tm), lambda b, i: (b, 0, i, 0, 0))] + kv_specs,
        out_shape=win_shapes
        + [jax.ShapeDtypeStruct((bsz, MLA_HEADS, n // tm, 2 * LANES, tm), BF16)] + kv_shapes,
        compiler_params=_cparams(("parallel", "parallel")),
        name="proj_wm",
    )(x, shift, scale, g, cos, sin, w, wuq, wukv, qn, kvn)


def _segments(seg_refs):
    segs = []
    for k_ref, vt_ref in zip(seg_refs[0::2], seg_refs[1::2]):
        segs.append((k_ref, vt_ref, vt_ref.shape[2], vt_ref.shape[4]))
    return segs


def _item_rows(tk, w):
    return min(tk, max(FLASH_ITEM_ELEMS // w, 2 * LANES))


def _flash_loop_exact(qt, segs, m_sc, acc_sc):
    m_sc[...] = jnp.full(m_sc.shape, NEG_INF, F32)
    acc_sc[...] = jnp.zeros(acc_sc.shape, F32)
    for k_ref, vt_ref, nc, tk in segs:
        rows = _item_rows(tk, qt.shape[1])

        def body(j, carry, k_ref=k_ref, vt_ref=vt_ref, tk=tk, rows=rows):
            for r0 in range(0, tk, rows):
                start = pl.multiple_of(j * tk + r0, rows)
                s = _dot(k_ref[0, 0, pl.ds(start, rows), :], qt)
                m_prev = m_sc[...]
                m_new = jnp.maximum(m_prev, jnp.max(s, axis=0, keepdims=True))
                p = jnp.exp2(s - m_new).astype(BF16)
                acc_sc[...] = (jnp.exp2(m_prev - m_new) * acc_sc[...]
                               + _dot(vt_ref[0, 0, j][:, r0:r0 + rows], p))
                m_sc[...] = m_new
            return carry
        lax.fori_loop(0, nc, body, 0)


def _flash_loop_spec(qt, segs, acc_sc):
    probe = _dot(segs[0][0][0, 0, 0:SPEC_PROBE_ROWS, :], qt)
    m0 = jnp.max(probe, axis=0, keepdims=True)
    acc_sc[...] = jnp.zeros(acc_sc.shape, F32)
    items = []
    for k_ref, vt_ref, nc, tk in segs:
        rows = _item_rows(tk, qt.shape[1])
        items += [(k_ref, vt_ref, j, j * tk + r0, r0, rows) for j in range(nc) for r0 in range(0, tk, rows)]

    def scores(item):
        k_ref, _, _, row0, _, rows = item
        return _dot(k_ref[0, 0, row0:row0 + rows, :], qt)

    s = scores(items[0])
    for idx, (_, vt_ref, j, _, r0, rows) in enumerate(items):
        pf = jnp.exp2(s - m0)
        lsum = jnp.sum(pf, axis=0, keepdims=True)
        p = pf.astype(BF16)
        if idx + 1 < len(items):
            s = scores(items[idx + 1])
        acc_sc[0:LANES, :] += _dot(vt_ref[0, 0, j][0:LANES, r0:r0 + rows], p)
        acc_sc[LANES:LANES + 1, :] += lsum


def _flash_tile(qt, segs, m_sc, acc_sc):
    _flash_loop_spec(qt, segs, acc_sc)
    finite = jnp.where(jnp.abs(acc_sc[0:LANES + 1, :]) <= SPEC_FINITE_LIMIT, 1.0, 0.0)

    @pl.when(jnp.min(finite) < 0.5)
    def _redo():
        _flash_loop_exact(qt, segs, m_sc, acc_sc)


def _diff_flash_kernel(*refs, nseg, lambda_init):
    qt_ref, seg_refs = refs[0], refs[1:1 + 2 * nseg]
    dl_ref, g_ref, o_ref, m_sc, acc_sc = refs[1 + 2 * nseg:]
    tiles = qt_ref.shape[2]
    qt = jnp.concatenate([qt_ref[0, 0, t] for t in range(tiles)], axis=1)
    _flash_tile(qt, _segments(seg_refs), m_sc, acc_sc)
    tq = o_ref.shape[1] // tiles
    dl = dl_ref[...]
    lam = (jnp.exp(jnp.sum(dl[0:1] * dl[1:2], axis=1, keepdims=True))
           - jnp.exp(jnp.sum(dl[2:3] * dl[3:4], axis=1, keepdims=True)) + lambda_init)
    for t in range(tiles):
        lanes = slice(2 * t * tq, 2 * (t + 1) * tq)
        o = acc_sc[0:LANES, lanes] / acc_sc[LANES:LANES + 1, lanes]
        y = (o[:, :tq] - lam * o[:, tq:]).T
        o_ref[0, t * tq:(t + 1) * tq, :] = (_rms(y, g_ref[...]) * (1.0 - lambda_init)).astype(BF16)


def _mla_flash_kernel(*refs, nseg):
    qt_ref, seg_refs = refs[0], refs[1:1 + 2 * nseg]
    o_ref, m_sc, acc_sc = refs[1 + 2 * nseg:]
    qt = jnp.concatenate([qt_ref[0, 0, t] for t in range(qt_ref.shape[2])], axis=1)
    _flash_tile(qt, _segments(seg_refs), m_sc, acc_sc)
    o_ref[0] = (acc_sc[0:LANES, :] / acc_sc[LANES:LANES + 1, :]).T.astype(BF16)


def _flash(mode, qt, kv_segs, extra=(), lambda_init=0.0):
    bsz, heads, nq, dk, w = qt.shape
    maps = 2 if mode == "diff" else 1
    tps = FLASH_TILES_PER_STEP if nq % FLASH_TILES_PER_STEP == 0 else 1
    tq = tps * w // maps
    nq //= tps
    in_specs = [pl.BlockSpec((1, 1, tps, dk, w), lambda b, h, i: (b, h, i, 0, 0))]
    w *= tps
    args = [qt]
    for k, vt in kv_segs:
        in_specs += [pl.BlockSpec((1, 1) + k.shape[2:], lambda b, h, i: (b, h, 0, 0)),
                     pl.BlockSpec((1, 1) + vt.shape[2:], lambda b, h, i: (b, h, 0, 0, 0))]
        args += [k, vt]
    if mode == "diff":
        kern = functools.partial(_diff_flash_kernel, nseg=len(kv_segs), lambda_init=lambda_init)
        in_specs += [pl.BlockSpec(e.shape, lambda b, h, i: (0, 0)) for e in extra]
    else:
        kern = functools.partial(_mla_flash_kernel, nseg=len(kv_segs))
    return pl.pallas_call(
        kern,
        grid=(bsz, heads, nq),
        in_specs=in_specs,
        out_specs=pl.BlockSpec((1, tq, LANES), lambda b, h, i: (b, i, h)),
        out_shape=jax.ShapeDtypeStruct((bsz, nq * tq, heads * LANES), BF16),
        scratch_shapes=[pltpu.VMEM((1, w), F32), pltpu.VMEM((V_ROWS, w), F32)],
        compiler_params=_cparams(("parallel", "parallel", "arbitrary")),
        name=f"{mode}_flash",
    )(*args, *extra)


def _win_kernel(*refs, band, nblk):
    if band:
        (q_ref, kp_ref, ko_ref, kn_ref, vp_ref, vo_ref, vn_ref, kc_ref, vc_ref,
         sink_ref, o_ref) = refs
    else:
        q_ref, kc_ref, vc_ref, sink_ref, o_ref = refs
    t = QBLK
    w = WIN_GROUP * t
    if band:
        i = pl.program_id(1)
        kk = lax.broadcasted_iota(jnp.int32, (t, w), 0)
        qq = lax.broadcasted_iota(jnp.int32, (t, w), 1) % t
        ok_prev = kk >= qq + jnp.where(i > 0, 0, t)
        ok_next = kk <= qq - jnp.where(i < nblk - 1, 0, t)
    parts, sinks, maxes = [], [], []
    for kvh in range(WIN_KV_HEADS):
        ksl = slice(kvh * LANES, (kvh + 1) * LANES)
        h0 = kvh * WIN_GROUP
        qt = q_ref[0, kvh, 0]
        sink = jnp.concatenate([sink_ref[h0 + g:h0 + g + 1, :] for g in range(WIN_GROUP)],
                               axis=1) * LOG2E
        s_all = [_dot(kc_ref[0, :, ksl], qt)]
        if band:
            s_all = [jnp.where(ok_prev, _dot(kp_ref[0, :, ksl], qt), NEG_INF),
                     _dot(ko_ref[0, :, ksl], qt),
                     jnp.where(ok_next, _dot(kn_ref[0, :, ksl], qt), NEG_INF)] + s_all
        m = sink
        for part in s_all:
            m = jnp.maximum(m, jnp.max(part, axis=0, keepdims=True))
        parts.append(s_all)
        sinks.append(sink)
        maxes.append(m)
    outs = []
    for kvh in range(WIN_KV_HEADS):
        m, sink = maxes[kvh], sinks[kvh]
        p = [jnp.exp2(part - m).astype(BF16) for part in parts[kvh]]
        if band:
            p = jnp.concatenate(p, axis=0)
            vt = jnp.concatenate([vp_ref[0, kvh], vo_ref[0, kvh], vn_ref[0, kvh], vc_ref[0, kvh]],
                                 axis=1)
        else:
            p, vt = p[0], vc_ref[0, kvh]
        pv = _dot(vt, p)
        outs.append(pv[0:LANES, :] / (pv[LANES:LANES + 1, :] + jnp.exp2(sink - m)))
    for kvh in range(WIN_KV_HEADS):
        o = outs[kvh]
        for pair in range(WIN_GROUP // 2):
            a = o[0:WIN_HEAD_DIM, (2 * pair) * t:(2 * pair + 1) * t]
            b = o[0:WIN_HEAD_DIM, (2 * pair + 1) * t:(2 * pair + 2) * t]
            blk = kvh * (WIN_GROUP // 2) + pair
            o_ref[0, :, blk * LANES:(blk + 1) * LANES] = jnp.concatenate([a, b], axis=0).T.astype(BF16)


def _win_attn(qt, k, vt, kc, vtc, sink, band):
    bsz, _, nblk, _, qrow = qt.shape
    kw = kc.shape[2]
    nctx = kc.shape[1]
    t = QBLK
    out_w = WIN_Q_HEADS * WIN_HEAD_DIM
    own = lambda b, i: (b, i, 0)
    in_specs = [pl.BlockSpec((1, WIN_KV_HEADS, 1, LANES, qrow), lambda b, i: (b, 0, i, 0, 0))]
    args = [qt]
    if band:
        prev = lambda b, i: (b, jnp.maximum(i - 1, 0), 0)
        nxt = lambda b, i: (b, jnp.minimum(i + 1, nblk - 1), 0)
        kspec = lambda f: pl.BlockSpec((1, t, kw), f)
        vspec = lambda f: pl.BlockSpec((1, WIN_KV_HEADS, V_ROWS, t),
                                       lambda b, i, f=f: (b, 0, 0, f(b, i)[1]))
        in_specs += [kspec(prev), kspec(own), kspec(nxt), vspec(prev), vspec(own), vspec(nxt)]
        args += [k, k, k, vt, vt, vt]
    in_specs += [pl.BlockSpec((1, nctx, kw), lambda b, i: (b, 0, 0)),
                 pl.BlockSpec((1, WIN_KV_HEADS, V_ROWS, nctx), lambda b, i: (b, 0, 0, 0)),
                 pl.BlockSpec(sink.shape, lambda b, i: (0, 0))]
    args += [kc, vtc, sink]
    return pl.pallas_call(
        functools.partial(_win_kernel, band=band, nblk=nblk),
        grid=(bsz, nblk),
        in_specs=in_specs,
        out_specs=pl.BlockSpec((1, t, out_w), own),
        out_shape=jax.ShapeDtypeStruct((bsz, nblk * t, out_w), BF16),
        compiler_params=_cparams(("parallel", "parallel")),
        name="win_attn" if band else "win_attn_ctx",
    )(*args)


def _merge_kernel(x_ref, ya_ref, yb_ref, yc_ref, sh_ref, sc_ref, gt_ref, g2_ref, g3_ref,
                  wg_ref, bw_ref, wo_ref, o_ref):
    x = x_ref[0]
    d = x.shape[1]
    hm = _norm_mod(x, g2_ref[...], sh_ref[0], sc_ref[0]).astype(BF16)
    merged = jnp.zeros(x.shape, F32)
    for i, y_ref in enumerate((ya_ref, yb_ref, yc_ref)):
        gate = jax.nn.sigmoid(_dot(hm, wg_ref[:, i * d:(i + 1) * d]))
        merged = merged + gate * _dot(y_ref[0], bw_ref[i])
    y = _dot(merged.astype(BF16), wo_ref[...])
    o_ref[0] = x + gt_ref[0] * _rms(y, g3_ref[...])


def _merge(x, ya, yb, yc, shift, scale, gate, g2, g3, wg, bw, wo):
    bsz, n, d = x.shape
    tm = min(512, n)
    tok = pl.BlockSpec((1, tm, d), lambda b, i: (b, i, 0))
    gain = pl.BlockSpec((1, d), lambda b, i: (0, 0))
    return pl.pallas_call(
        _merge_kernel,
        grid=(bsz, n // tm),
        in_specs=[tok, tok, tok, tok, _mod_spec(shift), _mod_spec(scale), _mod_spec(gate),
                  gain, gain, _resident(wg.shape), _resident(bw.shape), _resident(wo.shape)],
        out_specs=tok,
        out_shape=jax.ShapeDtypeStruct(x.shape, F32),
        compiler_params=_cparams(("parallel", "parallel")),
        name="merge",
    )(x, ya, yb, yc, shift, scale, gate, g2, g3, wg, bw, wo)


def _rope_tables(n):
    quarter = ROPE_DIM // 4
    pos = jnp.arange(n)
    row = (pos // GRID_W).astype(F32)
    col = (pos % GRID_W).astype(F32)
    inv_freq = 1.0 / (ROPE_BASE ** (jnp.arange(quarter, dtype=F32) / quarter))
    ang_r = row[:, None] * inv_freq
    ang_c = col[:, None] * inv_freq
    cos = jnp.concatenate([jnp.cos(ang_r)] * 2 + [jnp.cos(ang_c)] * 2, axis=1)
    sin = jnp.concatenate([-jnp.sin(ang_r), jnp.sin(ang_r), -jnp.sin(ang_c), jnp.sin(ang_c)], axis=1)
    return jnp.tile(cos, (1, 2)), jnp.tile(sin, (1, 2))


def _layer_weights(l, ffn_w_in, ffn_w_out, mix_w_in, mla_w_uq, mla_w_ukv, branch_w, mix_w_out):
    d = mix_w_in.shape[1]
    w = mix_w_in[l]
    hw = DIFF_HEADS * 2 * DIFF_D
    wq_n = WIN_Q_HEADS * WIN_HEAD_DIM
    wk_n = WIN_KV_HEADS * WIN_HEAD_DIM
    c = 3 * hw
    w_diff = w[:, :c]
    rest_n = wq_n + 2 * wk_n + MLA_Q_RANK + MLA_KV_RANK + MLA_ROPE
    w_wm = jnp.concatenate([w[:, c:c + rest_n], jnp.zeros((d, _PB_COLS - rest_n), w.dtype)], axis=1)
    w_gate = w[:, c + rest_n:]
    wuq = mla_w_uq[l].reshape(MLA_Q_RANK, MLA_HEADS, MLA_NOPE + MLA_ROPE)
    wuq = jnp.pad(wuq, ((0, 0), (0, 0), (0, 2 * LANES - MLA_NOPE - MLA_ROPE)))
    wuq = wuq.reshape(MLA_Q_RANK, MLA_HEADS * 2 * LANES)
    return dict(
        ffn_in=[ffn_w_in[l, i].astype(BF16) for i in range(2)],
        ffn_out=[ffn_w_out[l, i].astype(BF16) for i in range(2)],
        w_diff=w_diff.astype(BF16), w_wm=w_wm.astype(BF16), w_gate=w_gate.astype(BF16),
        wuq=wuq.astype(BF16), wukv=mla_w_ukv[l].astype(BF16),
        bw=branch_w[l].astype(BF16), wo=mix_w_out[l].astype(BF16))


def kernel(x, c, ctx, c_ctx, ada_w, ada_b, norm_g, ffn_w_in, ffn_w_out, mix_w_in, diff_lambda,
           diff_subln_g, win_sink, mla_q_norm_g, mla_kv_norm_g, mla_w_uq, mla_w_ukv, branch_w,
           mix_w_out):
    bsz, s, d = x.shape
    nctx = ctx.shape[1]
    depth = ada_w.shape[0]

    cvecs = jnp.zeros((8, d), F32).at[:bsz].set(c).at[bsz].set(c_ctx)
    mods = _ada_mods(cvecs, ada_w, ada_b).reshape(depth, 8, N_MOD, d)

    cos_x, sin_x = _rope_tables(s)
    cos_c = jnp.ones((nctx, LANES), F32)
    sin_c = jnp.zeros((nctx, LANES), F32)

    h = ctx
    for l in range(depth):
        last = l == depth - 1
        lambda_init = 0.8 - 0.6 * math.exp(-0.3 * l)
        wts = _layer_weights(l, ffn_w_in, ffn_w_out, mix_w_in, mla_w_uq, mla_w_ukv, branch_w,
                             mix_w_out)
        mx = [mods[l, :bsz, k][:, None, :] for k in range(N_MOD)]
        mc = [mods[l, bsz:bsz + 1, k][:, None, :] for k in range(N_MOD)]
        g = [norm_g[l, k][None, :] for k in range(6)]
        qn = mla_q_norm_g[l][None, :]
        kvn = mla_kv_norm_g[l][None, :]
        dl = diff_lambda[l].astype(F32)
        subln = diff_subln_g[l][None, :]
        sink = jnp.broadcast_to(win_sink[l].astype(F32)[:, None], (WIN_Q_HEADS, LANES))

        x = _ffn(x, mx[0], mx[1], mx[2], g[0], g[1], wts["ffn_in"][0], wts["ffn_out"][0])
        h = _ffn(h, mc[0], mc[1], mc[2], g[0], g[1], wts["ffn_in"][0], wts["ffn_out"][0])

        dqt_x, dk_x, dvt_x = _proj_diff(x, mx[3], mx[4], g[2], cos_x, sin_x, wts["w_diff"])
        dqt_c, dk_c, dvt_c = _proj_diff(h, mc[3], mc[4], g[2], cos_c, sin_c, wts["w_diff"])
        wqt_x, wk_x, wvt_x, mqt_x, mk_x, mvt_x = _proj_wm(
            x, mx[3], mx[4], g[2], cos_x, sin_x, wts["w_wm"], wts["wuq"], wts["wukv"], qn, kvn)
        wqt_c, wk_c, wvt_c, mqt_c, mk_c, mvt_c = _proj_wm(
            h, mc[3], mc[4], g[2], cos_c, sin_c, wts["w_wm"], wts["wuq"], wts["wukv"], qn, kvn)

        ya = _flash("diff", dqt_x, [(dk_x, dvt_x), (dk_c, dvt_c)], (dl, subln), lambda_init)
        yb = _win_attn(wqt_x, wk_x, wvt_x, wk_c, wvt_c, sink, band=True)
        yc = _flash("mla", mqt_x, [(mk_x, mvt_x), (mk_c, mvt_c)])
        x_new = _merge(x, ya, yb, yc, mx[3], mx[4], mx[5], g[2], g[3],
                       wts["w_gate"], wts["bw"], wts["wo"])
        if not last:
            ca = _flash("diff", dqt_c, [(dk_c, dvt_c)], (dl, subln), lambda_init)
            cb = _win_attn(wqt_c, None, None, wk_c, wvt_c, sink, band=False)
            cc = _flash("mla", mqt_c, [(mk_c, mvt_c)])
            h = _merge(h, ca, cb, cc, mc[3], mc[4], mc[5], g[2], g[3],
                       wts["w_gate"], wts["bw"], wts["wo"])
            h = _ffn(h, mc[6], mc[7], mc[8], g[4], g[5], wts["ffn_in"][1], wts["ffn_out"][1])
        x = _ffn(x_new, mx[6], mx[7], mx[8], g[4], g[5], wts["ffn_in"][1], wts["ffn_out"][1])
    return x
```

```python
import functools
import math

import jax
import jax.numpy as jnp
from jax import lax
from jax.experimental import pallas as pl
from jax.experimental.pallas import tpu as pltpu

F32 = jnp.float32
BF16 = jnp.bfloat16

GRID_W = 64
QBLK = 128
EPS = 1e-6
NEG_INF = -1e30
ROPE_DIM = 64
ROPE_BASE = 10000.0
N_MOD = 9
FFN_RES = 0.5
DIFF_HEADS = 8
DIFF_D = 64
WIN_Q_HEADS = 16
WIN_KV_HEADS = 4
WIN_GROUP = 4
WIN_HEAD_DIM = 64
WIN_SCALE = WIN_HEAD_DIM ** -0.5
MLA_HEADS = 8
MLA_Q_RANK = 384
MLA_KV_RANK = 256
MLA_NOPE = 128
MLA_ROPE = 64
MLA_V = 128
MLA_SCALE = (MLA_NOPE + MLA_ROPE) ** -0.5
LOG2E = math.log2(math.e)

LANES = 128
VMEM_LIMIT = 56 * 1024 * 1024

FLASH_TQ = 512
FLASH_TILES_PER_STEP = 2
FLASH_ITEM_ELEMS = 2048 * 1024
KV_CHUNK = 2048
V_ROWS = LANES + 16
SPEC_FINITE_LIMIT = 3.0e38
SPEC_PROBE_ROWS = 128


def _cparams(sem):
    return pltpu.CompilerParams(dimension_semantics=sem, vmem_limit_bytes=VMEM_LIMIT)


def _resident(shape):
    nd = len(shape)
    return pl.BlockSpec(shape, lambda *_: (0,) * nd, pipeline_mode=pl.Buffered(1))


def _rms(x, g):
    return x * lax.rsqrt(jnp.mean(x * x, axis=-1, keepdims=True) + EPS) * g


def _norm_mod(x, g, shift, scale):
    return _rms(x, g) * (1.0 + scale) + shift


def _dot(a, b):
    return jnp.dot(a, b, preferred_element_type=F32)


def _dot_nt(a, b):
    return lax.dot_general(a, b, (((1,), (1,)), ((), ())), preferred_element_type=F32)


def _mod_spec(arr):
    d = arr.shape[-1]
    if arr.shape[0] == 1:
        return pl.BlockSpec((1, 1, d), lambda b, *_: (0, 0, 0))
    return pl.BlockSpec((1, 1, d), lambda b, *_: (b, 0, 0))


def _ada_kernel(c_ref, w_ref, b_ref, o_ref):
    c = c_ref[...]
    a = c * jax.nn.sigmoid(c)
    o_ref[0] = jnp.dot(a, w_ref[0], preferred_element_type=F32,
                       precision=lax.Precision.HIGHEST) + b_ref[0]


def _ada_mods(cvecs, ada_w, ada_b):
    depth, d, nd = ada_w.shape
    rows = cvecs.shape[0]
    tn = 1152 if nd % 1152 == 0 else nd
    return pl.pallas_call(
        _ada_kernel,
        grid=(depth, nd // tn),
        in_specs=[pl.BlockSpec((rows, d), lambda l, j: (0, 0)),
                  pl.BlockSpec((1, d, tn), lambda l, j: (l, 0, j)),
                  pl.BlockSpec((1, 1, tn), lambda l, j: (l, 0, j))],
        out_specs=pl.BlockSpec((1, rows, tn), lambda l, j: (l, 0, j)),
        out_shape=jax.ShapeDtypeStruct((depth, rows, nd), F32),
        compiler_params=_cparams(("parallel", "parallel")),
        name="ada_mods",
    )(cvecs, ada_w, ada_b.reshape(depth, 1, nd))


def _ffn_kernel(x_ref, sh_ref, sc_ref, gt_ref, gpre_ref, gpost_ref, win_ref, wout_ref, o_ref,
                *, ffn_dim, chunk):
    x = x_ref[0]
    xm = _norm_mod(x, gpre_ref[...], sh_ref[0], sc_ref[0]).astype(BF16)
    acc = jnp.zeros(x.shape, F32)
    for c in range(ffn_dim // chunk):
        a = _dot(xm, win_ref[:, c * chunk:(c + 1) * chunk])
        b = _dot(xm, win_ref[:, ffn_dim + c * chunk:ffn_dim + (c + 1) * chunk])
        h = (a * jax.nn.sigmoid(a) * b).astype(BF16)
        acc = acc + _dot(h, wout_ref[c * chunk:(c + 1) * chunk, :])
    o_ref[0] = x + FFN_RES * gt_ref[0] * _rms(acc, gpost_ref[...])


def _ffn(x, shift, scale, gate, g_pre, g_post, w_in, w_out):
    bsz, n, d = x.shape
    ffn_dim = w_out.shape[0]
    tm = min(512, n)
    tok = pl.BlockSpec((1, tm, d), lambda b, i: (b, i, 0))
    gain = pl.BlockSpec((1, d), lambda b, i: (0, 0))
    return pl.pallas_call(
        functools.partial(_ffn_kernel, ffn_dim=ffn_dim, chunk=256),
        grid=(bsz, n // tm),
        in_specs=[tok, _mod_spec(shift), _mod_spec(scale), _mod_spec(gate), gain, gain,
                  _resident(w_in.shape), _resident(w_out.shape)],
        out_specs=tok,
        out_shape=jax.ShapeDtypeStruct(x.shape, F32),
        compiler_params=_cparams(("parallel", "parallel")),
        name="ffn",
    )(x, shift, scale, gate, g_pre, g_post, w_in, w_out)


def _rope_masks(tm):
    lane = lax.broadcasted_iota(jnp.int32, (tm, LANES), 1)
    even = ((lane // (ROPE_DIM // 4)) & 1) == 0
    lo = lane < (LANES // 2)
    return even, lo


def _rope(blk, cos, sin, even):
    q = ROPE_DIM // 4
    partner = jnp.where(even, pltpu.roll(blk, LANES - q, 1), pltpu.roll(blk, q, 1))
    return blk * cos + partner * sin


def _store_vt(vt_ref, h, v_blk):
    tm = v_blk.shape[0]
    vt_ref[0, h, 0, 0:LANES, :] = v_blk.T.astype(BF16)
    row = lax.broadcasted_iota(jnp.int32, (V_ROWS - LANES, tm), 0)
    vt_ref[0, h, 0, LANES:V_ROWS, :] = jnp.where(row == 0, 1.0, 0.0).astype(BF16)


def _kv_specs(bsz, n, tm, heads, dk):
    chunk = min(KV_CHUNK, n)
    per = chunk // tm
    specs = [pl.BlockSpec((1, heads, tm, dk), lambda b, i: (b, 0, i, 0)),
             pl.BlockSpec((1, heads, 1, V_ROWS, tm), lambda b, i: (b, 0, i // per, 0, i % per))]
    shapes = [jax.ShapeDtypeStruct((bsz, heads, n, dk), BF16),
              jax.ShapeDtypeStruct((bsz, heads, n // chunk, V_ROWS, chunk), BF16)]
    return specs, shapes


def _proj_diff_kernel(x_ref, sh_ref, sc_ref, g_ref, cos_ref, sin_ref, w_ref,
                      qt_ref, k_ref, vt_ref):
    hm = _norm_mod(x_ref[0], g_ref[...], sh_ref[0], sc_ref[0]).astype(BF16)
    tm = hm.shape[0]
    cos = cos_ref[...]
    sin = sin_ref[...]
    even, _ = _rope_masks(tm)
    hw = DIFF_HEADS * 2 * DIFF_D
    q = _dot(hm, w_ref[:, 0:hw])
    k = _dot(hm, w_ref[:, hw:2 * hw])
    v = _dot(hm, w_ref[:, 2 * hw:3 * hw])
    scale = LOG2E / math.sqrt(DIFF_D)
    top = lax.broadcasted_iota(jnp.int32, (LANES, tm), 0) < DIFF_D
    for h in range(DIFF_HEADS):
        hs = slice(h * LANES, (h + 1) * LANES)
        qbt = (_rope(q[:, hs], cos, sin, even) * scale).T
        qt_ref[0, h, 0, :, 0:tm] = jnp.where(top, qbt, 0.0).astype(BF16)
        qt_ref[0, h, 0, :, tm:2 * tm] = jnp.where(top, 0.0, qbt).astype(BF16)
        k_ref[0, h] = _rope(k[:, hs], cos, sin, even).astype(BF16)
        _store_vt(vt_ref, h, v[:, hs])


def _proj_diff(x, shift, scale, g, cos, sin, w):
    bsz, n, d = x.shape
    tm = min(FLASH_TQ, n)
    tok = pl.BlockSpec((1, tm, d), lambda b, i: (b, i, 0))
    tab = pl.BlockSpec((tm, LANES), lambda b, i: (i, 0))
    kv_specs, kv_shapes = _kv_specs(bsz, n, tm, DIFF_HEADS, LANES)
    return pl.pallas_call(
        _proj_diff_kernel,
        grid=(bsz, n // tm),
        in_specs=[tok, _mod_spec(shift), _mod_spec(scale),
                  pl.BlockSpec((1, d), lambda b, i: (0, 0)), tab, tab, _resident(w.shape)],
        out_specs=[pl.BlockSpec((1, DIFF_HEADS, 1, LANES, 2 * tm), lambda b, i: (b, 0, i, 0, 0))]
        + kv_specs,
        out_shape=[jax.ShapeDtypeStruct((bsz, DIFF_HEADS, n // tm, LANES, 2 * tm), BF16)] + kv_shapes,
        compiler_params=_cparams(("parallel", "parallel")),
        name="proj_diff",
    )(x, shift, scale, g, cos, sin, w)


_WQ0, _WK0, _WV0 = 0, 1024, 1280
_CQ0, _CKV0, _KR0, _PB_COLS = 1536, 1920, 2176, 2304


def _proj_wm_kernel(x_ref, sh_ref, sc_ref, g_ref, cos_ref, sin_ref, w_ref, wuq_ref, wukv_ref,
                    qn_ref, kvn_ref, wqt_ref, wk_ref, wvt_ref, mqt_ref, mk_ref, mvt_ref):
    hm = _norm_mod(x_ref[0], g_ref[...], sh_ref[0], sc_ref[0]).astype(BF16)
    tm = hm.shape[0]
    cos = cos_ref[...]
    sin = sin_ref[...]
    even, lo = _rope_masks(tm)
    half = LANES // 2
    p = _dot(hm, w_ref[...])

    def two_heads(blk):
        return jnp.where(lo, blk, 0.0), jnp.where(lo, pltpu.roll(blk, half, 1), 0.0)

    for j in range(WIN_Q_HEADS // 2):
        blk = _rope(p[:, _WQ0 + j * LANES:_WQ0 + (j + 1) * LANES], cos, sin, even) * (WIN_SCALE * LOG2E)
        for sub, padded in enumerate(two_heads(blk)):
            kvh, grp = divmod(2 * j + sub, WIN_GROUP)
            qt = padded.T.astype(BF16)
            for qb in range(tm // QBLK):
                wqt_ref[0, kvh, qb, :, grp * QBLK:(grp + 1) * QBLK] = qt[:, qb * QBLK:(qb + 1) * QBLK]
    for j in range(WIN_KV_HEADS // 2):
        kblk = two_heads(_rope(p[:, _WK0 + j * LANES:_WK0 + (j + 1) * LANES], cos, sin, even))
        vblk = two_heads(p[:, _WV0 + j * LANES:_WV0 + (j + 1) * LANES])
        for sub in range(2):
            kvh = 2 * j + sub
            wk_ref[0, :, kvh * LANES:(kvh + 1) * LANES] = kblk[sub].astype(BF16)
            wvt_ref[0, kvh, 0:LANES, :] = vblk[sub].T.astype(BF16)
            row = lax.broadcasted_iota(jnp.int32, (V_ROWS - LANES, tm), 0)
            wvt_ref[0, kvh, LANES:V_ROWS, :] = jnp.where(row == 0, 1.0, 0.0).astype(BF16)

    cq = _rms(p[:, _CQ0:_CQ0 + MLA_Q_RANK], qn_ref[...]).astype(BF16)
    q2 = _dot(cq, wuq_ref[...])
    ckv = _rms(p[:, _CKV0:_CKV0 + MLA_KV_RANK], kvn_ref[...]).astype(BF16)
    kv = _dot(ckv, wukv_ref[...])
    kr = _rope(p[:, _KR0:_KR0 + LANES], cos, sin, even).astype(BF16)
    for h in range(MLA_HEADS):
        c0 = 2 * h * LANES
        qn = q2[:, c0:c0 + LANES] * (MLA_SCALE * LOG2E)
        qr = _rope(q2[:, c0 + LANES:c0 + 2 * LANES], cos, sin, even) * (MLA_SCALE * LOG2E)
        mqt_ref[0, h, 0, 0:LANES, :] = qn.T.astype(BF16)
        mqt_ref[0, h, 0, LANES:2 * LANES, :] = qr.T.astype(BF16)
        mk_ref[0, h, :, 0:LANES] = kv[:, c0:c0 + LANES].astype(BF16)
        mk_ref[0, h, :, LANES:2 * LANES] = kr
        _store_vt(mvt_ref, h, kv[:, c0 + LANES:c0 + 2 * LANES])


def _proj_wm(x, shift, scale, g, cos, sin, w, wuq, wukv, qn, kvn):
    bsz, n, d = x.shape
    tm = min(FLASH_TQ, n)
    tok = lambda c: pl.BlockSpec((1, tm, c), lambda b, i: (b, i, 0))
    tab = pl.BlockSpec((tm, LANES), lambda b, i: (i, 0))
    row = lambda c: pl.BlockSpec((1, c), lambda b, i: (0, 0))
    kv_specs, kv_shapes = _kv_specs(bsz, n, tm, MLA_HEADS, 2 * LANES)
    qrow = WIN_GROUP * QBLK
    win_specs = [pl.BlockSpec((1, WIN_KV_HEADS, tm // QBLK, LANES, qrow), lambda b, i: (b, 0, i, 0, 0)),
                 tok(WIN_KV_HEADS * LANES),
                 pl.BlockSpec((1, WIN_KV_HEADS, V_ROWS, tm), lambda b, i: (b, 0, 0, i))]
    win_shapes = [jax.ShapeDtypeStruct((bsz, WIN_KV_HEADS, n // QBLK, LANES, qrow), BF16),
                  jax.ShapeDtypeStruct((bsz, n, WIN_KV_HEADS * LANES), BF16),
                  jax.ShapeDtypeStruct((bsz, WIN_KV_HEADS, V_ROWS, n), BF16)]
    return pl.pallas_call(
        _proj_wm_kernel,
        grid=(bsz, n // tm),
        in_specs=[tok(d), _mod_spec(shift), _mod_spec(scale), row(d), tab, tab,
                  _resident(w.shape), _resident(wuq.shape), _resident(wukv.shape),
                  row(MLA_Q_RANK), row(MLA_KV_RANK)],
        out_specs=win_specs
        + [pl.BlockSpec((1, MLA_HEADS, 1, 2 * LANES, tm), lambda b, i: (b, 0, i, 0, 0))] + kv_specs,
        out_shape=win_shapes
        + [jax.ShapeDtypeStruct((bsz, MLA_HEADS, n // tm, 2 * LANES, tm), BF16)] + kv_shapes,
        compiler_params=_cparams(("parallel", "parallel")),
        name="proj_wm",
    )(x, shift, scale, g, cos, sin, w, wuq, wukv, qn, kvn)


def _segments(seg_refs):
    segs = []
    for k_ref, vt_ref in zip(seg_refs[0::2], seg_refs[1::2]):
        segs.append((k_ref, vt_ref, vt_ref.shape[2], vt_ref.shape[4]))
    return segs


def _item_rows(tk, w):
    return min(tk, max(FLASH_ITEM_ELEMS // w, 2 * LANES))


def _flash_loop_exact(qt, segs, m_sc, acc_sc):
    m_sc[...] = jnp.full(m_sc.shape, NEG_INF, F32)
    acc_sc[...] = jnp.zeros(acc_sc.shape, F32)
    for k_ref, vt_ref, nc, tk in segs:
        rows = _item_rows(tk, qt.shape[1])

        def body(j, carry, k_ref=k_ref, vt_ref=vt_ref, tk=tk, rows=rows):
            for r0 in range(0, tk, rows):
                start = pl.multiple_of(j * tk + r0, rows)
                s = _dot(k_ref[0, 0, pl.ds(start, rows), :], qt)
                m_prev = m_sc[...]
                m_new = jnp.maximum(m_prev, jnp.max(s, axis=0, keepdims=True))
                p = jnp.exp2(s - m_new).astype(BF16)
                acc_sc[...] = (jnp.exp2(m_prev - m_new) * acc_sc[...]
                               + _dot(vt_ref[0, 0, j][:, r0:r0 + rows], p))
                m_sc[...] = m_new
            return carry
        lax.fori_loop(0, nc, body, 0)


def _flash_loop_spec(qt, segs, acc_sc):
    probe = _dot(segs[0][0][0, 0, 0:SPEC_PROBE_ROWS, :], qt)
    m0 = jnp.max(probe, axis=0, keepdims=True)
    acc_sc[...] = jnp.zeros(acc_sc.shape, F32)
    items = []
    for k_ref, vt_ref, nc, tk in segs:
        rows = _item_rows(tk, qt.shape[1])
        items += [(k_ref, vt_ref, j, j * tk + r0, r0, rows) for j in range(nc) for r0 in range(0, tk, rows)]

    def scores(item):
        k_ref, _, _, row0, _, rows = item
        return _dot(k_ref[0, 0, row0:row0 + rows, :], qt)

    s = scores(items[0])
    for idx, (_, vt_ref, j, _, r0, rows) in enumerate(items):
        pf = jnp.exp2(s - m0)
        lsum = jnp.sum(pf, axis=0, keepdims=True)
        p = pf.astype(BF16)
        if idx + 1 < len(items):
            s = scores(items[idx + 1])
        acc_sc[0:LANES, :] += _dot(vt_ref[0, 0, j][0:LANES, r0:r0 + rows], p)
        acc_sc[LANES:LANES + 1, :] += lsum


def _flash_tile(qt, segs, m_sc, acc_sc):
    _flash_loop_spec(qt, segs, acc_sc)
    finite = jnp.where(jnp.abs(acc_sc[0:LANES + 1, :]) <= SPEC_FINITE_LIMIT, 1.0, 0.0)

    @pl.when(jnp.min(finite) < 0.5)
    def _redo():
        _flash_loop_exact(qt, segs, m_sc, acc_sc)


def _diff_flash_kernel(*refs, nseg, lambda_init):
    qt_ref, seg_refs = refs[0], refs[1:1 + 2 * nseg]
    dl_ref, g_ref, o_ref, m_sc, acc_sc = refs[1 + 2 * nseg:]
    tiles = qt_ref.shape[2]
    qt = jnp.concatenate([qt_ref[0, 0, t] for t in range(tiles)], axis=1)
    _flash_tile(qt, _segments(seg_refs), m_sc, acc_sc)
    tq = o_ref.shape[1] // tiles
    dl = dl_ref[...]
    lam = (jnp.exp(jnp.sum(dl[0:1] * dl[1:2], axis=1, keepdims=True))
           - jnp.exp(jnp.sum(dl[2:3] * dl[3:4], axis=1, keepdims=True)) + lambda_init)
    for t in range(tiles):
        lanes = slice(2 * t * tq, 2 * (t + 1) * tq)
        o = acc_sc[0:LANES, lanes] / acc_sc[LANES:LANES + 1, lanes]
        yt = o[:, :tq] - lam * o[:, tq:]
        yt = yt * lax.rsqrt(jnp.mean(yt * yt, axis=0, keepdims=True) + EPS)
        o_ref[0, t * tq:(t + 1) * tq, :] = (yt.T * g_ref[...] * (1.0 - lambda_init)).astype(BF16)


def _mla_flash_kernel(*refs, nseg):
    qt_ref, seg_refs = refs[0], refs[1:1 + 2 * nseg]
    o_ref, m_sc, acc_sc = refs[1 + 2 * nseg:]
    qt = jnp.concatenate([qt_ref[0, 0, t] for t in range(qt_ref.shape[2])], axis=1)
    _flash_tile(qt, _segments(seg_refs), m_sc, acc_sc)
    o_ref[0] = (acc_sc[0:LANES, :] / acc_sc[LANES:LANES + 1, :]).T.astype(BF16)


def _flash(mode, qt, kv_segs, extra=(), lambda_init=0.0):
    bsz, heads, nq, dk, w = qt.shape
    maps = 2 if mode == "diff" else 1
    tps = FLASH_TILES_PER_STEP if nq % FLASH_TILES_PER_STEP == 0 else 1
    tq = tps * w // maps
    nq //= tps
    in_specs = [pl.BlockSpec((1, 1, tps, dk, w), lambda b, h, i: (b, h, i, 0, 0))]
    w *= tps
    args = [qt]
    for k, vt in kv_segs:
        in_specs += [pl.BlockSpec((1, 1) + k.shape[2:], lambda b, h, i: (b, h, 0, 0)),
                     pl.BlockSpec((1, 1) + vt.shape[2:], lambda b, h, i: (b, h, 0, 0, 0))]
        args += [k, vt]
    if mode == "diff":
        kern = functools.partial(_diff_flash_kernel, nseg=len(kv_segs), lambda_init=lambda_init)
        in_specs += [pl.BlockSpec(e.shape, lambda b, h, i: (0, 0)) for e in extra]
    else:
        kern = functools.partial(_mla_flash_kernel, nseg=len(kv_segs))
    return pl.pallas_call(
        kern,
        grid=(bsz, heads, nq),
        in_specs=in_specs,
        out_specs=pl.BlockSpec((1, tq, LANES), lambda b, h, i: (b, i, h)),
        out_shape=jax.ShapeDtypeStruct((bsz, nq * tq, heads * LANES), BF16),
        scratch_shapes=[pltpu.VMEM((1, w), F32), pltpu.VMEM((V_ROWS, w), F32)],
        compiler_params=_cparams(("parallel", "parallel", "arbitrary")),
        name=f"{mode}_flash",
    )(*args, *extra)


def _win_kernel(*refs, band, nblk):
    if band:
        (q_ref, kp_ref, ko_ref, kn_ref, vp_ref, vo_ref, vn_ref, kc_ref, vc_ref,
         sink_ref, o_ref) = refs
    else:
        q_ref, kc_ref, vc_ref, sink_ref, o_ref = refs
    t = QBLK
    w = WIN_GROUP * t
    if band:
        i = pl.program_id(1)
        kk = lax.broadcasted_iota(jnp.int32, (t, w), 0)
        qq = lax.broadcasted_iota(jnp.int32, (t, w), 1) % t
        ok_prev = kk >= qq + jnp.where(i > 0, 0, t)
        ok_next = kk <= qq - jnp.where(i < nblk - 1, 0, t)
    parts, sinks, maxes = [], [], []
    for kvh in range(WIN_KV_HEADS):
        ksl = slice(kvh * LANES, (kvh + 1) * LANES)
        h0 = kvh * WIN_GROUP
        qt = q_ref[0, kvh, 0]
        sink = jnp.concatenate([sink_ref[h0 + g:h0 + g + 1, :] for g in range(WIN_GROUP)],
                               axis=1) * LOG2E
        s_all = [_dot(kc_ref[0, :, ksl], qt)]
        if band:
            s_all = [jnp.where(ok_prev, _dot(kp_ref[0, :, ksl], qt), NEG_INF),
                     _dot(ko_ref[0, :, ksl], qt),
                     jnp.where(ok_next, _dot(kn_ref[0, :, ksl], qt), NEG_INF)] + s_all
        m = sink
        for part in s_all:
            m = jnp.maximum(m, jnp.max(part, axis=0, keepdims=True))
        parts.append(s_all)
        sinks.append(sink)
        maxes.append(m)
    outs = []
    for kvh in range(WIN_KV_HEADS):
        m, sink = maxes[kvh], sinks[kvh]
        p = [jnp.exp2(part - m).astype(BF16) for part in parts[kvh]]
        if band:
            p = jnp.concatenate(p, axis=0)
            vt = jnp.concatenate([vp_ref[0, kvh], vo_ref[0, kvh], vn_ref[0, kvh], vc_ref[0, kvh]],
                                 axis=1)
        else:
            p, vt = p[0], vc_ref[0, kvh]
        pv = _dot(vt, p)
        outs.append(pv[0:LANES, :] / (pv[LANES:LANES + 1, :] + jnp.exp2(sink - m)))
    for kvh in range(WIN_KV_HEADS):
        o = outs[kvh]
        for pair in range(WIN_GROUP // 2):
            a = o[0:WIN_HEAD_DIM, (2 * pair) * t:(2 * pair + 1) * t]
            b = o[0:WIN_HEAD_DIM, (2 * pair + 1) * t:(2 * pair + 2) * t]
            blk = kvh * (WIN_GROUP // 2) + pair
            o_ref[0, :, blk * LANES:(blk + 1) * LANES] = jnp.concatenate([a, b], axis=0).T.astype(BF16)


def _win_attn(qt, k, vt, kc, vtc, sink, band):
    bsz, _, nblk, _, qrow = qt.shape
    kw = kc.shape[2]
    nctx = kc.shape[1]
    t = QBLK
    out_w = WIN_Q_HEADS * WIN_HEAD_DIM
    own = lambda b, i: (b, i, 0)
    in_specs = [pl.BlockSpec((1, WIN_KV_HEADS, 1, LANES, qrow), lambda b, i: (b, 0, i, 0, 0))]
    args = [qt]
    if band:
        prev = lambda b, i: (b, jnp.maximum(i - 1, 0), 0)
        nxt = lambda b, i: (b, jnp.minimum(i + 1, nblk - 1), 0)
        kspec = lambda f: pl.BlockSpec((1, t, kw), f)
        vspec = lambda f: pl.BlockSpec((1, WIN_KV_HEADS, V_ROWS, t),
                                       lambda b, i, f=f: (b, 0, 0, f(b, i)[1]))
        in_specs += [kspec(prev), kspec(own), kspec(nxt), vspec(prev), vspec(own), vspec(nxt)]
        args += [k, k, k, vt, vt, vt]
    in_specs += [pl.BlockSpec((1, nctx, kw), lambda b, i: (b, 0, 0)),
                 pl.BlockSpec((1, WIN_KV_HEADS, V_ROWS, nctx), lambda b, i: (b, 0, 0, 0)),
                 pl.BlockSpec(sink.shape, lambda b, i: (0, 0))]
    args += [kc, vtc, sink]
    return pl.pallas_call(
        functools.partial(_win_kernel, band=band, nblk=nblk),
        grid=(bsz, nblk),
        in_specs=in_specs,
        out_specs=pl.BlockSpec((1, t, out_w), own),
        out_shape=jax.ShapeDtypeStruct((bsz, nblk * t, out_w), BF16),
        compiler_params=_cparams(("parallel", "parallel")),
        name="win_attn" if band else "win_attn_ctx",
    )(*args)


def _merge_kernel(x_ref, ya_ref, yb_ref, yc_ref, sh_ref, sc_ref, gt_ref, g2_ref, g3_ref,
                  wg_ref, bw_ref, wo_ref, o_ref):
    x = x_ref[0]
    d = x.shape[1]
    hm = _norm_mod(x, g2_ref[...], sh_ref[0], sc_ref[0]).astype(BF16)
    merged = jnp.zeros(x.shape, F32)
    for i, y_ref in enumerate((ya_ref, yb_ref, yc_ref)):
        gate = jax.nn.sigmoid(_dot(hm, wg_ref[:, i * d:(i + 1) * d]))
        merged = merged + gate * _dot(y_ref[0], bw_ref[i])
    y = _dot(merged.astype(BF16), wo_ref[...])
    o_ref[0] = x + gt_ref[0] * _rms(y, g3_ref[...])


def _merge(x, ya, yb, yc, shift, scale, gate, g2, g3, wg, bw, wo):
    bsz, n, d = x.shape
    tm = min(512, n)
    tok = pl.BlockSpec((1, tm, d), lambda b, i: (b, i, 0))
    gain = pl.BlockSpec((1, d), lambda b, i: (0, 0))
    return pl.pallas_call(
        _merge_kernel,
        grid=(bsz, n // tm),
        in_specs=[tok, tok, tok, tok, _mod_spec(shift), _mod_spec(scale), _mod_spec(gate),
                  gain, gain, _resident(wg.shape), _resident(bw.shape), _resident(wo.shape)],
        out_specs=tok,
        out_shape=jax.ShapeDtypeStruct(x.shape, F32),
        compiler_params=_cparams(("parallel", "parallel")),
        name="merge",
    )(x, ya, yb, yc, shift, scale, gate, g2, g3, wg, bw, wo)


def _rope_tables(n):
    quarter = ROPE_DIM // 4
    pos = jnp.arange(n)
    row = (pos // GRID_W).astype(F32)
    col = (pos % GRID_W).astype(F32)
    inv_freq = 1.0 / (ROPE_BASE ** (jnp.arange(quarter, dtype=F32) / quarter))
    ang_r = row[:, None] * inv_freq
    ang_c = col[:, None] * inv_freq
    cos = jnp.concatenate([jnp.cos(ang_r)] * 2 + [jnp.cos(ang_c)] * 2, axis=1)
    sin = jnp.concatenate([-jnp.sin(ang_r), jnp.sin(ang_r), -jnp.sin(ang_c), jnp.sin(ang_c)], axis=1)
    return jnp.tile(cos, (1, 2)), jnp.tile(sin, (1, 2))


def _layer_weights(l, ffn_w_in, ffn_w_out, mix_w_in, mla_w_uq, mla_w_ukv, branch_w, mix_w_out):
    d = mix_w_in.shape[1]
    w = mix_w_in[l]
    hw = DIFF_HEADS * 2 * DIFF_D
    wq_n = WIN_Q_HEADS * WIN_HEAD_DIM
    wk_n = WIN_KV_HEADS * WIN_HEAD_DIM
    c = 3 * hw
    w_diff = w[:, :c]
    rest_n = wq_n + 2 * wk_n + MLA_Q_RANK + MLA_KV_RANK + MLA_ROPE
    w_wm = jnp.concatenate([w[:, c:c + rest_n], jnp.zeros((d, _PB_COLS - rest_n), w.dtype)], axis=1)
    w_gate = w[:, c + rest_n:]
    wuq = mla_w_uq[l].reshape(MLA_Q_RANK, MLA_HEADS, MLA_NOPE + MLA_ROPE)
    wuq = jnp.pad(wuq, ((0, 0), (0, 0), (0, 2 * LANES - MLA_NOPE - MLA_ROPE)))
    wuq = wuq.reshape(MLA_Q_RANK, MLA_HEADS * 2 * LANES)
    return dict(
        ffn_in=[ffn_w_in[l, i].astype(BF16) for i in range(2)],
        ffn_out=[ffn_w_out[l, i].astype(BF16) for i in range(2)],
        w_diff=w_diff.astype(BF16), w_wm=w_wm.astype(BF16), w_gate=w_gate.astype(BF16),
        wuq=wuq.astype(BF16), wukv=mla_w_ukv[l].astype(BF16),
        bw=branch_w[l].astype(BF16), wo=mix_w_out[l].astype(BF16))


def kernel(x, c, ctx, c_ctx, ada_w, ada_b, norm_g, ffn_w_in, ffn_w_out, mix_w_in, diff_lambda,
           diff_subln_g, win_sink, mla_q_norm_g, mla_kv_norm_g, mla_w_uq, mla_w_ukv, branch_w,
           mix_w_out):
    bsz, s, d = x.shape
    nctx = ctx.shape[1]
    depth = ada_w.shape[0]

    cvecs = jnp.zeros((8, d), F32).at[:bsz].set(c).at[bsz].set(c_ctx)
    mods = _ada_mods(cvecs, ada_w, ada_b).reshape(depth, 8, N_MOD, d)

    cos_x, sin_x = _rope_tables(s)
    cos_c = jnp.ones((nctx, LANES), F32)
    sin_c = jnp.zeros((nctx, LANES), F32)

    h = ctx
    for l in range(depth):
        last = l == depth - 1
        lambda_init = 0.8 - 0.6 * math.exp(-0.3 * l)
        wts = _layer_weights(l, ffn_w_in, ffn_w_out, mix_w_in, mla_w_uq, mla_w_ukv, branch_w,
                             mix_w_out)
        mx = [mods[l, :bsz, k][:, None, :] for k in range(N_MOD)]
        mc = [mods[l, bsz:bsz + 1, k][:, None, :] for k in range(N_MOD)]
        g = [norm_g[l, k][None, :] for k in range(6)]
        qn = mla_q_norm_g[l][None, :]
        kvn = mla_kv_norm_g[l][None, :]
        dl = diff_lambda[l].astype(F32)
        subln = diff_subln_g[l][None, :]
        sink = jnp.broadcast_to(win_sink[l].astype(F32)[:, None], (WIN_Q_HEADS, LANES))

        x = _ffn(x, mx[0], mx[1], mx[2], g[0], g[1], wts["ffn_in"][0], wts["ffn_out"][0])
        h = _ffn(h, mc[0], mc[1], mc[2], g[0], g[1], wts["ffn_in"][0], wts["ffn_out"][0])

        dqt_x, dk_x, dvt_x = _proj_diff(x, mx[3], mx[4], g[2], cos_x, sin_x, wts["w_diff"])
        dqt_c, dk_c, dvt_c = _proj_diff(h, mc[3], mc[4], g[2], cos_c, sin_c, wts["w_diff"])
        wqt_x, wk_x, wvt_x, mqt_x, mk_x, mvt_x = _proj_wm(
            x, mx[3], mx[4], g[2], cos_x, sin_x, wts["w_wm"], wts["wuq"], wts["wukv"], qn, kvn)
        wqt_c, wk_c, wvt_c, mqt_c, mk_c, mvt_c = _proj_wm(
            h, mc[3], mc[4], g[2], cos_c, sin_c, wts["w_wm"], wts["wuq"], wts["wukv"], qn, kvn)

        ya = _flash("diff", dqt_x, [(dk_x, dvt_x), (dk_c, dvt_c)], (dl, subln), lambda_init)
        yb = _win_attn(wqt_x, wk_x, wvt_x, wk_c, wvt_c, sink, band=True)
        yc = _flash("mla", mqt_x, [(mk_x, mvt_x), (mk_c, mvt_c)])
        x_new = _merge(x, ya, yb, yc, mx[3], mx[4], mx[5], g[2], g[3],
                       wts["w_gate"], wts["bw"], wts["wo"])
        if not last:
            ca = _flash("diff", dqt_c, [(dk_c, dvt_c)], (dl, subln), lambda_init)
            cb = _win_attn(wqt_c, None, None, wk_c, wvt_c, sink, band=False)
            cc = _flash("mla", mqt_c, [(mk_c, mvt_c)])
            h = _merge(h, ca, cb, cc, mc[3], mc[4], mc[5], g[2], g[3],
                       wts["w_gate"], wts["bw"], wts["wo"])
            h = _ffn(h, mc[6], mc[7], mc[8], g[4], g[5], wts["ffn_in"][1], wts["ffn_out"][1])
        x = _ffn(x_new, mx[6], mx[7], mx[8], g[4], g[5], wts["ffn_in"][1], wts["ffn_out"][1])
    return x
```

```python
import functools
import math

import jax
import jax.numpy as jnp
from jax import lax
from jax.experimental import pallas as pl
from jax.experimental.pallas import tpu as pltpu

F32 = jnp.float32
BF16 = jnp.bfloat16

GRID_W = 64
QBLK = 128
EPS = 1e-6
NEG_INF = -1e30
ROPE_DIM = 64
ROPE_BASE = 10000.0
N_MOD = 9
FFN_RES = 0.5
DIFF_HEADS = 8
DIFF_D = 64
WIN_Q_HEADS = 16
WIN_KV_HEADS = 4
WIN_GROUP = 4
WIN_HEAD_DIM = 64
WIN_SCALE = WIN_HEAD_DIM ** -0.5
MLA_HEADS = 8
MLA_Q_RANK = 384
MLA_KV_RANK = 256
MLA_NOPE = 128
MLA_ROPE = 64
MLA_V = 128
MLA_SCALE = (MLA_NOPE + MLA_ROPE) ** -0.5
LOG2E = math.log2(math.e)

LANES = 128
VMEM_LIMIT = 56 * 1024 * 1024

FLASH_TQ = 512
FLASH_TILES_PER_STEP = 2
WIN_BLOCKS_PER_STEP = 4
FLASH_ITEM_ELEMS = 2048 * 1024
KV_CHUNK = 2048
V_ROWS = LANES + 16
SPEC_FINITE_LIMIT = 3.0e38
SPEC_PROBE_ROWS = 128


def _cparams(sem):
    return pltpu.CompilerParams(dimension_semantics=sem, vmem_limit_bytes=VMEM_LIMIT)


def _resident(shape):
    nd = len(shape)
    return pl.BlockSpec(shape, lambda *_: (0,) * nd, pipeline_mode=pl.Buffered(1))


def _rms(x, g):
    return x * lax.rsqrt(jnp.mean(x * x, axis=-1, keepdims=True) + EPS) * g


def _norm_mod(x, g, shift, scale):
    return _rms(x, g) * (1.0 + scale) + shift


def _dot(a, b):
    return jnp.dot(a, b, preferred_element_type=F32)


def _dot_nt(a, b):
    return lax.dot_general(a, b, (((1,), (1,)), ((), ())), preferred_element_type=F32)


def _mod_spec(arr):
    d = arr.shape[-1]
    if arr.shape[0] == 1:
        return pl.BlockSpec((1, 1, d), lambda b, *_: (0, 0, 0))
    return pl.BlockSpec((1, 1, d), lambda b, *_: (b, 0, 0))


def _ada_kernel(c_ref, w_ref, b_ref, o_ref):
    c = c_ref[...]
    a = c * jax.nn.sigmoid(c)
    o_ref[0] = jnp.dot(a, w_ref[0], preferred_element_type=F32,
                       precision=lax.Precision.HIGHEST) + b_ref[0]


def _ada_mods(cvecs, ada_w, ada_b):
    depth, d, nd = ada_w.shape
    rows = cvecs.shape[0]
    tn = 1152 if nd % 1152 == 0 else nd
    return pl.pallas_call(
        _ada_kernel,
        grid=(depth, nd // tn),
        in_specs=[pl.BlockSpec((rows, d), lambda l, j: (0, 0)),
                  pl.BlockSpec((1, d, tn), lambda l, j: (l, 0, j)),
                  pl.BlockSpec((1, 1, tn), lambda l, j: (l, 0, j))],
        out_specs=pl.BlockSpec((1, rows, tn), lambda l, j: (l, 0, j)),
        out_shape=jax.ShapeDtypeStruct((depth, rows, nd), F32),
        compiler_params=_cparams(("parallel", "parallel")),
        name="ada_mods",
    )(cvecs, ada_w, ada_b.reshape(depth, 1, nd))


def _ffn_kernel(x_ref, sh_ref, sc_ref, gt_ref, gpre_ref, gpost_ref, win_ref, wout_ref, o_ref,
                *, ffn_dim, chunk):
    x = x_ref[0]
    xm = _norm_mod(x, gpre_ref[...], sh_ref[0], sc_ref[0]).astype(BF16)
    acc = jnp.zeros(x.shape, F32)
    for c in range(ffn_dim // chunk):
        a = _dot(xm, win_ref[:, c * chunk:(c + 1) * chunk])
        b = _dot(xm, win_ref[:, ffn_dim + c * chunk:ffn_dim + (c + 1) * chunk])
        h = (a * jax.nn.sigmoid(a) * b).astype(BF16)
        acc = acc + _dot(h, wout_ref[c * chunk:(c + 1) * chunk, :])
    o_ref[0] = x + FFN_RES * gt_ref[0] * _rms(acc, gpost_ref[...])


def _ffn(x, shift, scale, gate, g_pre, g_post, w_in, w_out):
    bsz, n, d = x.shape
    ffn_dim = w_out.shape[0]
    tm = min(512, n)
    tok = pl.BlockSpec((1, tm, d), lambda b, i: (b, i, 0))
    gain = pl.BlockSpec((1, d), lambda b, i: (0, 0))
    return pl.pallas_call(
        functools.partial(_ffn_kernel, ffn_dim=ffn_dim, chunk=256),
        grid=(bsz, n // tm),
        in_specs=[tok, _mod_spec(shift), _mod_spec(scale), _mod_spec(gate), gain, gain,
                  _resident(w_in.shape), _resident(w_out.shape)],
        out_specs=tok,
        out_shape=jax.ShapeDtypeStruct(x.shape, F32),
        compiler_params=_cparams(("parallel", "parallel")),
        name="ffn",
    )(x, shift, scale, gate, g_pre, g_post, w_in, w_out)


def _rope_masks(tm):
    lane = lax.broadcasted_iota(jnp.int32, (tm, LANES), 1)
    even = ((lane // (ROPE_DIM // 4)) & 1) == 0
    lo = lane < (LANES // 2)
    return even, lo


def _rope(blk, cos, sin, even):
    q = ROPE_DIM // 4
    partner = jnp.where(even, pltpu.roll(blk, LANES - q, 1), pltpu.roll(blk, q, 1))
    return blk * cos + partner * sin


def _store_vt(vt_ref, h, v_blk):
    tm = v_blk.shape[0]
    vt_ref[0, h, 0, 0:LANES, :] = v_blk.T.astype(BF16)
    row = lax.broadcasted_iota(jnp.int32, (V_ROWS - LANES, tm), 0)
    vt_ref[0, h, 0, LANES:V_ROWS, :] = jnp.where(row == 0, 1.0, 0.0).astype(BF16)


def _kv_specs(bsz, n, tm, heads, dk):
    chunk = min(KV_CHUNK, n)
    per = chunk // tm
    specs = [pl.BlockSpec((1, heads, tm, dk), lambda b, i: (b, 0, i, 0)),
             pl.BlockSpec((1, heads, 1, V_ROWS, tm), lambda b, i: (b, 0, i // per, 0, i % per))]
    shapes = [jax.ShapeDtypeStruct((bsz, heads, n, dk), BF16),
              jax.ShapeDtypeStruct((bsz, heads, n // chunk, V_ROWS, chunk), BF16)]
    return specs, shapes


def _proj_diff_kernel(x_ref, sh_ref, sc_ref, g_ref, cos_ref, sin_ref, w_ref,
                      qt_ref, k_ref, vt_ref):
    hm = _norm_mod(x_ref[0], g_ref[...], sh_ref[0], sc_ref[0]).astype(BF16)
    tm = hm.shape[0]
    cos = cos_ref[...]
    sin = sin_ref[...]
    even, _ = _rope_masks(tm)
    hw = DIFF_HEADS * 2 * DIFF_D
    q = _dot(hm, w_ref[:, 0:hw])
    k = _dot(hm, w_ref[:, hw:2 * hw])
    v = _dot(hm, w_ref[:, 2 * hw:3 * hw])
    scale = LOG2E / math.sqrt(DIFF_D)
    top = lax.broadcasted_iota(jnp.int32, (LANES, tm), 0) < DIFF_D
    for h in range(DIFF_HEADS):
        hs = slice(h * LANES, (h + 1) * LANES)
        qbt = (_rope(q[:, hs], cos, sin, even) * scale).T
        qt_ref[0, h, 0, :, 0:tm] = jnp.where(top, qbt, 0.0).astype(BF16)
        qt_ref[0, h, 0, :, tm:2 * tm] = jnp.where(top, 0.0, qbt).astype(BF16)
        k_ref[0, h] = _rope(k[:, hs], cos, sin, even).astype(BF16)
        _store_vt(vt_ref, h, v[:, hs])


def _proj_diff(x, shift, scale, g, cos, sin, w):
    bsz, n, d = x.shape
    tm = min(FLASH_TQ, n)
    tok = pl.BlockSpec((1, tm, d), lambda b, i: (b, i, 0))
    tab = pl.BlockSpec((tm, LANES), lambda b, i: (i, 0))
    kv_specs, kv_shapes = _kv_specs(bsz, n, tm, DIFF_HEADS, LANES)
    return pl.pallas_call(
        _proj_diff_kernel,
        grid=(bsz, n // tm),
        in_specs=[tok, _mod_spec(shift), _mod_spec(scale),
                  pl.BlockSpec((1, d), lambda b, i: (0, 0)), tab, tab, _resident(w.shape)],
        out_specs=[pl.BlockSpec((1, DIFF_HEADS, 1, LANES, 2 * tm), lambda b, i: (b, 0, i, 0, 0))]
        + kv_specs,
        out_shape=[jax.ShapeDtypeStruct((bsz, DIFF_HEADS, n // tm, LANES, 2 * tm), BF16)] + kv_shapes,
        compiler_params=_cparams(("parallel", "parallel")),
        name="proj_diff",
    )(x, shift, scale, g, cos, sin, w)


_WQ0, _WK0, _WV0 = 0, 1024, 1280
_CQ0, _CKV0, _KR0, _PB_COLS = 1536, 1920, 2176, 2304


def _proj_wm_kernel(x_ref, sh_ref, sc_ref, g_ref, cos_ref, sin_ref, w_ref, wuq_ref, wukv_ref,
                    qn_ref, kvn_ref, wqt_ref, wk_ref, wvt_ref, mqt_ref, mk_ref, mvt_ref):
    hm = _norm_mod(x_ref[0], g_ref[...], sh_ref[0], sc_ref[0]).astype(BF16)
    tm = hm.shape[0]
    cos = cos_ref[...]
    sin = sin_ref[...]
    even, lo = _rope_masks(tm)
    half = LANES // 2
    p = _dot(hm, w_ref[...])

    def two_heads(blk):
        return jnp.where(lo, blk, 0.0), jnp.where(lo, pltpu.roll(blk, half, 1), 0.0)

    for j in range(WIN_Q_HEADS // 2):
        blk = _rope(p[:, _WQ0 + j * LANES:_WQ0 + (j + 1) * LANES], cos, sin, even) * (WIN_SCALE * LOG2E)
        for sub, padded in enumerate(two_heads(blk)):
            kvh, grp = divmod(2 * j + sub, WIN_GROUP)
            qt = padded.T.astype(BF16)
            for qb in range(tm // QBLK):
                wqt_ref[0, kvh, qb, :, grp * QBLK:(grp + 1) * QBLK] = qt[:, qb * QBLK:(qb + 1) * QBLK]
    for j in range(WIN_KV_HEADS // 2):
        kblk = two_heads(_rope(p[:, _WK0 + j * LANES:_WK0 + (j + 1) * LANES], cos, sin, even))
        vblk = two_heads(p[:, _WV0 + j * LANES:_WV0 + (j + 1) * LANES])
        for sub in range(2):
            kvh = 2 * j + sub
            wk_ref[0, :, kvh * LANES:(kvh + 1) * LANES] = kblk[sub].astype(BF16)
            wvt_ref[0, kvh, 0:LANES, :] = vblk[sub].T.astype(BF16)
            row = lax.broadcasted_iota(jnp.int32, (V_ROWS - LANES, tm), 0)
            wvt_ref[0, kvh, LANES:V_ROWS, :] = jnp.where(row == 0, 1.0, 0.0).astype(BF16)

    cq = _rms(p[:, _CQ0:_CQ0 + MLA_Q_RANK], qn_ref[...]).astype(BF16)
    q2 = _dot(cq, wuq_ref[...])
    ckv = _rms(p[:, _CKV0:_CKV0 + MLA_KV_RANK], kvn_ref[...]).astype(BF16)
    kv = _dot(ckv, wukv_ref[...])
    kr = _rope(p[:, _KR0:_KR0 + LANES], cos, sin, even).astype(BF16)
    for h in range(MLA_HEADS):
        c0 = 2 * h * LANES
        qn = q2[:, c0:c0 + LANES] * (MLA_SCALE * LOG2E)
        qr = _rope(q2[:, c0 + LANES:c0 + 2 * LANES], cos, sin, even) * (MLA_SCALE * LOG2E)
        mqt_ref[0, h, 0, 0:LANES, :] = qn.T.astype(BF16)
        mqt_ref[0, h, 0, LANES:2 * LANES, :] = qr.T.astype(BF16)
        mk_ref[0, h, :, 0:LANES] = kv[:, c0:c0 + LANES].astype(BF16)
        mk_ref[0, h, :, LANES:2 * LANES] = kr
        _store_vt(mvt_ref, h, kv[:, c0 + LANES:c0 + 2 * LANES])


def _proj_wm(x, shift, scale, g, cos, sin, w, wuq, wukv, qn, kvn):
    bsz, n, d = x.shape
    tm = min(FLASH_TQ, n)
    tok = lambda c: pl.BlockSpec((1, tm, c), lambda b, i: (b, i, 0))
    tab = pl.BlockSpec((tm, LANES), lambda b, i: (i, 0))
    row = lambda c: pl.BlockSpec((1, c), lambda b, i: (0, 0))
    kv_specs, kv_shapes = _kv_specs(bsz, n, tm, MLA_HEADS, 2 * LANES)
    qrow = WIN_GROUP * QBLK
    win_specs = [pl.BlockSpec((1, WIN_KV_HEADS, tm // QBLK, LANES, qrow), lambda b, i: (b, 0, i, 0, 0)),
                 tok(WIN_KV_HEADS * LANES),
                 pl.BlockSpec((1, WIN_KV_HEADS, V_ROWS, tm), lambda b, i: (b, 0, 0, i))]
    win_shapes = [jax.ShapeDtypeStruct((bsz, WIN_KV_HEADS, n // QBLK, LANES, qrow), BF16),
                  jax.ShapeDtypeStruct((bsz, n, WIN_KV_HEADS * LANES), BF16),
                  jax.ShapeDtypeStruct((bsz, WIN_KV_HEADS, V_ROWS, n), BF16)]
    return pl.pallas_call(
        _proj_wm_kernel,
        grid=(bsz, n // tm),
        in_specs=[tok(d), _mod_spec(shift), _mod_spec(scale), row(d), tab, tab,
                  _resident(w.shape), _resident(wuq.shape), _resident(wukv.shape),
                  row(MLA_Q_RANK), row(MLA_KV_RANK)],
        out_specs=win_specs
        + [pl.BlockSpec((1, MLA_HEADS, 1, 2 * LANES, tm), lambda b, i: (b, 0, i, 0, 0))] + kv_specs,
        out_shape=win_shapes
        + [jax.ShapeDtypeStruct((bsz, MLA_HEADS, n // tm, 2 * LANES, tm), BF16)] + kv_shapes,
        compiler_params=_cparams(("parallel", "parallel")),
        name="proj_wm",
    )(x, shift, scale, g, cos, sin, w, wuq, wukv, qn, kvn)


def _segments(seg_refs):
    segs = []
    for k_ref, vt_ref in zip(seg_refs[0::2], seg_refs[1::2]):
        segs.append((k_ref, vt_ref, vt_ref.shape[2], vt_ref.shape[4]))
    return segs


def _item_rows(tk, w):
    return min(tk, max(FLASH_ITEM_ELEMS // w, 2 * LANES))


def _flash_loop_exact(qt, segs, m_sc, acc_sc):
    m_sc[...] = jnp.full(m_sc.shape, NEG_INF, F32)
    acc_sc[...] = jnp.zeros(acc_sc.shape, F32)
    for k_ref, vt_ref, nc, tk in segs:
        rows = _item_rows(tk, qt.shape[1])

        def body(j, carry, k_ref=k_ref, vt_ref=vt_ref, tk=tk, rows=rows):
            for r0 in range(0, tk, rows):
                start = pl.multiple_of(j * tk + r0, rows)
                s = _dot(k_ref[0, 0, pl.ds(start, rows), :], qt)
                m_prev = m_sc[...]
                m_new = jnp.maximum(m_prev, jnp.max(s, axis=0, keepdims=True))
                p = jnp.exp2(s - m_new).astype(BF16)
                acc_sc[...] = (jnp.exp2(m_prev - m_new) * acc_sc[...]
                               + _dot(vt_ref[0, 0, j][:, r0:r0 + rows], p))
                m_sc[...] = m_new
            return carry
        lax.fori_loop(0, nc, body, 0)


def _flash_loop_spec(qt, segs, acc_sc):
    probe = _dot(segs[0][0][0, 0, 0:SPEC_PROBE_ROWS, :], qt)
    m0 = jnp.max(probe, axis=0, keepdims=True)
    acc_sc[...] = jnp.zeros(acc_sc.shape, F32)
    items = []
    for k_ref, vt_ref, nc, tk in segs:
        rows = _item_rows(tk, qt.shape[1])
        items += [(k_ref, vt_ref, j, j * tk + r0, r0, rows) for j in range(nc) for r0 in range(0, tk, rows)]

    def scores(item):
        k_ref, _, _, row0, _, rows = item
        return _dot(k_ref[0, 0, row0:row0 + rows, :], qt)

    s = scores(items[0])
    for idx, (_, vt_ref, j, _, r0, rows) in enumerate(items):
        pf = jnp.exp2(s - m0)
        lsum = jnp.sum(pf, axis=0, keepdims=True)
        p = pf.astype(BF16)
        if idx + 1 < len(items):
            s = scores(items[idx + 1])
        acc_sc[0:LANES, :] += _dot(vt_ref[0, 0, j][0:LANES, r0:r0 + rows], p)
        acc_sc[LANES:LANES + 1, :] += lsum


def _flash_tile(qt, segs, m_sc, acc_sc):
    _flash_loop_spec(qt, segs, acc_sc)
    finite = jnp.where(jnp.abs(acc_sc[0:LANES + 1, :]) <= SPEC_FINITE_LIMIT, 1.0, 0.0)

    @pl.when(jnp.min(finite) < 0.5)
    def _redo():
        _flash_loop_exact(qt, segs, m_sc, acc_sc)


def _diff_flash_kernel(*refs, nseg, lambda_init):
    qt_ref, seg_refs = refs[0], refs[1:1 + 2 * nseg]
    dl_ref, g_ref, o_ref, m_sc, acc_sc = refs[1 + 2 * nseg:]
    tiles = qt_ref.shape[2]
    qt = jnp.concatenate([qt_ref[0, 0, t] for t in range(tiles)], axis=1)
    _flash_tile(qt, _segments(seg_refs), m_sc, acc_sc)
    tq = o_ref.shape[1] // tiles
    dl = dl_ref[...]
    lam = (jnp.exp(jnp.sum(dl[0:1] * dl[1:2], axis=1, keepdims=True))
           - jnp.exp(jnp.sum(dl[2:3] * dl[3:4], axis=1, keepdims=True)) + lambda_init)
    for t in range(tiles):
        lanes = slice(2 * t * tq, 2 * (t + 1) * tq)
        o = acc_sc[0:LANES, lanes] / acc_sc[LANES:LANES + 1, lanes]
        yt = o[:, :tq] - lam * o[:, tq:]
        yt = yt * lax.rsqrt(jnp.mean(yt * yt, axis=0, keepdims=True) + EPS)
        o_ref[0, t * tq:(t + 1) * tq, :] = (yt.T * g_ref[...] * (1.0 - lambda_init)).astype(BF16)


def _mla_flash_kernel(*refs, nseg):
    qt_ref, seg_refs = refs[0], refs[1:1 + 2 * nseg]
    o_ref, m_sc, acc_sc = refs[1 + 2 * nseg:]
    qt = jnp.concatenate([qt_ref[0, 0, t] for t in range(qt_ref.shape[2])], axis=1)
    _flash_tile(qt, _segments(seg_refs), m_sc, acc_sc)
    o_ref[0] = (acc_sc[0:LANES, :] / acc_sc[LANES:LANES + 1, :]).T.astype(BF16)


def _flash(mode, qt, kv_segs, extra=(), lambda_init=0.0):
    bsz, heads, nq, dk, w = qt.shape
    maps = 2 if mode == "diff" else 1
    tps = FLASH_TILES_PER_STEP if nq % FLASH_TILES_PER_STEP == 0 else 1
    tq = tps * w // maps
    nq //= tps
    in_specs = [pl.BlockSpec((1, 1, tps, dk, w), lambda b, h, i: (b, h, i, 0, 0))]
    w *= tps
    args = [qt]
    for k, vt in kv_segs:
        in_specs += [pl.BlockSpec((1, 1) + k.shape[2:], lambda b, h, i: (b, h, 0, 0)),
                     pl.BlockSpec((1, 1) + vt.shape[2:], lambda b, h, i: (b, h, 0, 0, 0))]
        args += [k, vt]
    if mode == "diff":
        kern = functools.partial(_diff_flash_kernel, nseg=len(kv_segs), lambda_init=lambda_init)
        in_specs += [pl.BlockSpec(e.shape, lambda b, h, i: (0, 0)) for e in extra]
    else:
        kern = functools.partial(_mla_flash_kernel, nseg=len(kv_segs))
    return pl.pallas_call(
        kern,
        grid=(bsz, heads, nq),
        in_specs=in_specs,
        out_specs=pl.BlockSpec((1, tq, LANES), lambda b, h, i: (b, i, h)),
        out_shape=jax.ShapeDtypeStruct((bsz, nq * tq, heads * LANES), BF16),
        scratch_shapes=[pltpu.VMEM((1, w), F32), pltpu.VMEM((V_ROWS, w), F32)],
        compiler_params=_cparams(("parallel", "parallel", "arbitrary")),
        name=f"{mode}_flash",
    )(*args, *extra)


def _win_kernel(*refs, band, nblk):
    if band:
        (q_ref, kp_ref, ko_ref, kn_ref, vp_ref, vo_ref, vn_ref, kc_ref, vc_ref,
         sink_ref, o_ref) = refs
    else:
        q_ref, kc_ref, vc_ref, sink_ref, o_ref = refs
    t = QBLK
    w = WIN_GROUP * t
    bps = q_ref.shape[2]
    if band:
        i = pl.program_id(1)
        kk = lax.broadcasted_iota(jnp.int32, (t, w), 0)
        qq = lax.broadcasted_iota(jnp.int32, (t, w), 1) % t
    for r in range(bps):
        rows = slice(r * t, (r + 1) * t)
        if band:
            ok_prev = kk >= qq + (jnp.where(i > 0, 0, t) if r == 0 else 0)
            ok_next = kk <= qq - (jnp.where(i < nblk // bps - 1, 0, t) if r == bps - 1 else 0)
        parts, sinks, maxes = [], [], []
        for kvh in range(WIN_KV_HEADS):
            ksl = slice(kvh * LANES, (kvh + 1) * LANES)
            h0 = kvh * WIN_GROUP
            qt = q_ref[0, kvh, r]
            sink = jnp.concatenate([sink_ref[h0 + g:h0 + g + 1, :] for g in range(WIN_GROUP)],
                                   axis=1) * LOG2E
            s_all = [_dot(kc_ref[0, :, ksl], qt)]
            if band:
                k_prev = kp_ref[0, :, ksl] if r == 0 else ko_ref[0, (r - 1) * t:r * t, ksl]
                k_next = kn_ref[0, :, ksl] if r == bps - 1 else ko_ref[0, (r + 1) * t:(r + 2) * t, ksl]
                s_all = [jnp.where(ok_prev, _dot(k_prev, qt), NEG_INF),
                         _dot(ko_ref[0, rows, ksl], qt),
                         jnp.where(ok_next, _dot(k_next, qt), NEG_INF)] + s_all
            m = sink
            for part in s_all:
                m = jnp.maximum(m, jnp.max(part, axis=0, keepdims=True))
            parts.append(s_all)
            sinks.append(sink)
            maxes.append(m)
        outs = []
        for kvh in range(WIN_KV_HEADS):
            m, sink = maxes[kvh], sinks[kvh]
            p = [jnp.exp2(part - m).astype(BF16) for part in parts[kvh]]
            if band:
                p = jnp.concatenate(p, axis=0)
                v_prev = vp_ref[0, kvh] if r == 0 else vo_ref[0, kvh, :, (r - 1) * t:r * t]
                v_next = vn_ref[0, kvh] if r == bps - 1 else vo_ref[0, kvh, :, (r + 1) * t:(r + 2) * t]
                vt = jnp.concatenate([v_prev, vo_ref[0, kvh, :, rows], v_next, vc_ref[0, kvh]],
                                     axis=1)
            else:
                p, vt = p[0], vc_ref[0, kvh]
            pv = _dot(vt, p)
            outs.append(pv[0:LANES, :] / (pv[LANES:LANES + 1, :] + jnp.exp2(sink - m)))
        for kvh in range(WIN_KV_HEADS):
            o = outs[kvh]
            for pair in range(WIN_GROUP // 2):
                a = o[0:WIN_HEAD_DIM, (2 * pair) * t:(2 * pair + 1) * t]
                b = o[0:WIN_HEAD_DIM, (2 * pair + 1) * t:(2 * pair + 2) * t]
                blk = kvh * (WIN_GROUP // 2) + pair
                o_ref[0, rows, blk * LANES:(blk + 1) * LANES] = jnp.concatenate([a, b], axis=0).T.astype(BF16)


def _win_attn(qt, k, vt, kc, vtc, sink, band):
    bsz, _, nblk, _, qrow = qt.shape
    kw = kc.shape[2]
    nctx = kc.shape[1]
    t = QBLK
    out_w = WIN_Q_HEADS * WIN_HEAD_DIM
    own = lambda b, i: (b, i, 0)
    bps = WIN_BLOCKS_PER_STEP if (band and nblk % WIN_BLOCKS_PER_STEP == 0) else 1
    in_specs = [pl.BlockSpec((1, WIN_KV_HEADS, bps, LANES, qrow), lambda b, i: (b, 0, i, 0, 0))]
    args = [qt]
    if band:
        prev = lambda b, i: (b, jnp.maximum(bps * i - 1, 0), 0)
        nxt = lambda b, i: (b, jnp.minimum(bps * i + bps, nblk - 1), 0)
        kspec = lambda f, n: pl.BlockSpec((1, n, kw), f)
        vspec = lambda f, n: pl.BlockSpec((1, WIN_KV_HEADS, V_ROWS, n),
                                          lambda b, i, f=f: (b, 0, 0, f(b, i)[1]))
        in_specs += [kspec(prev, t), kspec(own, bps * t), kspec(nxt, t),
                     vspec(prev, t), vspec(own, bps * t), vspec(nxt, t)]
        args += [k, k, k, vt, vt, vt]
    in_specs += [pl.BlockSpec((1, nctx, kw), lambda b, i: (b, 0, 0)),
                 pl.BlockSpec((1, WIN_KV_HEADS, V_ROWS, nctx), lambda b, i: (b, 0, 0, 0)),
                 pl.BlockSpec(sink.shape, lambda b, i: (0, 0))]
    args += [kc, vtc, sink]
    return pl.pallas_call(
        functools.partial(_win_kernel, band=band, nblk=nblk),
        grid=(bsz, nblk // bps),
        in_specs=in_specs,
        out_specs=pl.BlockSpec((1, bps * t, out_w), own),
        out_shape=jax.ShapeDtypeStruct((bsz, nblk * t, out_w), BF16),
        compiler_params=_cparams(("parallel", "parallel")),
        name="win_attn" if band else "win_attn_ctx",
    )(*args)


def _merge_kernel(x_ref, ya_ref, yb_ref, yc_ref, sh_ref, sc_ref, gt_ref, g2_ref, g3_ref,
                  wg_ref, bw_ref, wo_ref, o_ref):
    x = x_ref[0]
    d = x.shape[1]
    hm = _norm_mod(x, g2_ref[...], sh_ref[0], sc_ref[0]).astype(BF16)
    merged = jnp.zeros(x.shape, F32)
    for i, y_ref in enumerate((ya_ref, yb_ref, yc_ref)):
        gate = jax.nn.sigmoid(_dot(hm, wg_ref[:, i * d:(i + 1) * d]))
        merged = merged + gate * _dot(y_ref[0], bw_ref[i])
    y = _dot(merged.astype(BF16), wo_ref[...])
    o_ref[0] = x + gt_ref[0] * _rms(y, g3_ref[...])


def _merge(x, ya, yb, yc, shift, scale, gate, g2, g3, wg, bw, wo):
    bsz, n, d = x.shape
    tm = min(512, n)
    tok = pl.BlockSpec((1, tm, d), lambda b, i: (b, i, 0))
    gain = pl.BlockSpec((1, d), lambda b, i: (0, 0))
    return pl.pallas_call(
        _merge_kernel,
        grid=(bsz, n // tm),
        in_specs=[tok, tok, tok, tok, _mod_spec(shift), _mod_spec(scale), _mod_spec(gate),
                  gain, gain, _resident(wg.shape), _resident(bw.shape), _resident(wo.shape)],
        out_specs=tok,
        out_shape=jax.ShapeDtypeStruct(x.shape, F32),
        compiler_params=_cparams(("parallel", "parallel")),
        name="merge",
    )(x, ya, yb, yc, shift, scale, gate, g2, g3, wg, bw, wo)


def _rope_tables(n):
    quarter = ROPE_DIM // 4
    pos = jnp.arange(n)
    row = (pos // GRID_W).astype(F32)
    col = (pos % GRID_W).astype(F32)
    inv_freq = 1.0 / (ROPE_BASE ** (jnp.arange(quarter, dtype=F32) / quarter))
    ang_r = row[:, None] * inv_freq
    ang_c = col[:, None] * inv_freq
    cos = jnp.concatenate([jnp.cos(ang_r)] * 2 + [jnp.cos(ang_c)] * 2, axis=1)
    sin = jnp.concatenate([-jnp.sin(ang_r), jnp.sin(ang_r), -jnp.sin(ang_c), jnp.sin(ang_c)], axis=1)
    return jnp.tile(cos, (1, 2)), jnp.tile(sin, (1, 2))


def _layer_weights(l, ffn_w_in, ffn_w_out, mix_w_in, mla_w_uq, mla_w_ukv, branch_w, mix_w_out):
    d = mix_w_in.shape[1]
    w = mix_w_in[l]
    hw = DIFF_HEADS * 2 * DIFF_D
    wq_n = WIN_Q_HEADS * WIN_HEAD_DIM
    wk_n = WIN_KV_HEADS * WIN_HEAD_DIM
    c = 3 * hw
    w_diff = w[:, :c]
    rest_n = wq_n + 2 * wk_n + MLA_Q_RANK + MLA_KV_RANK + MLA_ROPE
    w_wm = jnp.concatenate([w[:, c:c + rest_n], jnp.zeros((d, _PB_COLS - rest_n), w.dtype)], axis=1)
    w_gate = w[:, c + rest_n:]
    wuq = mla_w_uq[l].reshape(MLA_Q_RANK, MLA_HEADS, MLA_NOPE + MLA_ROPE)
    wuq = jnp.pad(wuq, ((0, 0), (0, 0), (0, 2 * LANES - MLA_NOPE - MLA_ROPE)))
    wuq = wuq.reshape(MLA_Q_RANK, MLA_HEADS * 2 * LANES)
    return dict(
        ffn_in=[ffn_w_in[l, i].astype(BF16) for i in range(2)],
        ffn_out=[ffn_w_out[l, i].astype(BF16) for i in range(2)],
        w_diff=w_diff.astype(BF16), w_wm=w_wm.astype(BF16), w_gate=w_gate.astype(BF16),
        wuq=wuq.astype(BF16), wukv=mla_w_ukv[l].astype(BF16),
        bw=branch_w[l].astype(BF16), wo=mix_w_out[l].astype(BF16))


def kernel(x, c, ctx, c_ctx, ada_w, ada_b, norm_g, ffn_w_in, ffn_w_out, mix_w_in, diff_lambda,
           diff_subln_g, win_sink, mla_q_norm_g, mla_kv_norm_g, mla_w_uq, mla_w_ukv, branch_w,
           mix_w_out):
    bsz, s, d = x.shape
    nctx = ctx.shape[1]
    depth = ada_w.shape[0]

    cvecs = jnp.zeros((8, d), F32).at[:bsz].set(c).at[bsz].set(c_ctx)
    mods = _ada_mods(cvecs, ada_w, ada_b).reshape(depth, 8, N_MOD, d)

    cos_x, sin_x = _rope_tables(s)
    cos_c = jnp.ones((nctx, LANES), F32)
    sin_c = jnp.zeros((nctx, LANES), F32)

    h = ctx
    for l in range(depth):
        last = l == depth - 1
        lambda_init = 0.8 - 0.6 * math.exp(-0.3 * l)
        wts = _layer_weights(l, ffn_w_in, ffn_w_out, mix_w_in, mla_w_uq, mla_w_ukv, branch_w,
                             mix_w_out)
        mx = [mods[l, :bsz, k][:, None, :] for k in range(N_MOD)]
        mc = [mods[l, bsz:bsz + 1, k][:, None, :] for k in range(N_MOD)]
        g = [norm_g[l, k][None, :] for k in range(6)]
        qn = mla_q_norm_g[l][None, :]
        kvn = mla_kv_norm_g[l][None, :]
        dl = diff_lambda[l].astype(F32)
        subln = diff_subln_g[l][None, :]
        sink = jnp.broadcast_to(win_sink[l].astype(F32)[:, None], (WIN_Q_HEADS, LANES))

        x = _ffn(x, mx[0], mx[1], mx[2], g[0], g[1], wts["ffn_in"][0], wts["ffn_out"][0])
        h = _ffn(h, mc[0], mc[1], mc[2], g[0], g[1], wts["ffn_in"][0], wts["ffn_out"][0])

        dqt_x, dk_x, dvt_x = _proj_diff(x, mx[3], mx[4], g[2], cos_x, sin_x, wts["w_diff"])
        dqt_c, dk_c, dvt_c = _proj_diff(h, mc[3], mc[4], g[2], cos_c, sin_c, wts["w_diff"])
        wqt_x, wk_x, wvt_x, mqt_x, mk_x, mvt_x = _proj_wm(
            x, mx[3], mx[4], g[2], cos_x, sin_x, wts["w_wm"], wts["wuq"], wts["wukv"], qn, kvn)
        wqt_c, wk_c, wvt_c, mqt_c, mk_c, mvt_c = _proj_wm(
            h, mc[3], mc[4], g[2], cos_c, sin_c, wts["w_wm"], wts["wuq"], wts["wukv"], qn, kvn)

        ya = _flash("diff", dqt_x, [(dk_x, dvt_x), (dk_c, dvt_c)], (dl, subln), lambda_init)
        yb = _win_attn(wqt_x, wk_x, wvt_x, wk_c, wvt_c, sink, band=True)
        yc = _flash("mla", mqt_x, [(mk_x, mvt_x), (mk_c, mvt_c)])
        x_new = _merge(x, ya, yb, yc, mx[3], mx[4], mx[5], g[2], g[3],
                       wts["w_gate"], wts["bw"], wts["wo"])
        if not last:
            ca = _flash("diff", dqt_c, [(dk_c, dvt_c)], (dl, subln), lambda_init)
            cb = _win_attn(wqt_c, None, None, wk_c, wvt_c, sink, band=False)
            cc = _flash("mla", mqt_c, [(mk_c, mvt_c)])
            h = _merge(h, ca, cb, cc, mc[3], mc[4], mc[5], g[2], g[3],
                       wts["w_gate"], wts["bw"], wts["wo"])
            h = _ffn(h, mc[6], mc[7], mc[8], g[4], g[5], wts["ffn_in"][1], wts["ffn_out"][1])
        x = _ffn(x_new, mx[6], mx[7], mx[8], g[4], g[5], wts["ffn_in"][1], wts["ffn_out"][1])
    return x
```

```python
import functools
import math

import jax
import jax.numpy as jnp
from jax import lax
from jax.experimental import pallas as pl
from jax.experimental.pallas import tpu as pltpu

F32 = jnp.float32
BF16 = jnp.bfloat16

GRID_W = 64
QBLK = 128
EPS = 1e-6
NEG_INF = -1e30
ROPE_DIM = 64
ROPE_BASE = 10000.0
N_MOD = 9
FFN_RES = 0.5
DIFF_HEADS = 8
DIFF_D = 64
WIN_Q_HEADS = 16
WIN_KV_HEADS = 4
WIN_GROUP = 4
WIN_HEAD_DIM = 64
WIN_SCALE = WIN_HEAD_DIM ** -0.5
MLA_HEADS = 8
MLA_Q_RANK = 384
MLA_KV_RANK = 256
MLA_NOPE = 128
MLA_ROPE = 64
MLA_V = 128
MLA_SCALE = (MLA_NOPE + MLA_ROPE) ** -0.5
LOG2E = math.log2(math.e)

LANES = 128
VMEM_LIMIT = 56 * 1024 * 1024

FFN_SUB_ROWS = 512
FLASH_TQ = 512
FLASH_TILES_PER_STEP = 2
WIN_BLOCKS_PER_STEP = 4
FLASH_ITEM_ELEMS = 2048 * 1024
KV_CHUNK = 2048
V_ROWS = LANES + 16
SPEC_FINITE_LIMIT = 3.0e38
SPEC_PROBE_ROWS = 128


def _cparams(sem):
    return pltpu.CompilerParams(dimension_semantics=sem, vmem_limit_bytes=VMEM_LIMIT)


def _resident(shape):
    nd = len(shape)
    return pl.BlockSpec(shape, lambda *_: (0,) * nd, pipeline_mode=pl.Buffered(1))


def _rms(x, g):
    return x * lax.rsqrt(jnp.mean(x * x, axis=-1, keepdims=True) + EPS) * g


def _norm_mod(x, g, shift, scale):
    return _rms(x, g) * (1.0 + scale) + shift


def _dot(a, b):
    return jnp.dot(a, b, preferred_element_type=F32)


def _dot_nt(a, b):
    return lax.dot_general(a, b, (((1,), (1,)), ((), ())), preferred_element_type=F32)


def _mod_spec(arr):
    d = arr.shape[-1]
    if arr.shape[0] == 1:
        return pl.BlockSpec((1, 1, d), lambda b, *_: (0, 0, 0))
    return pl.BlockSpec((1, 1, d), lambda b, *_: (b, 0, 0))


def _ada_kernel(c_ref, w_ref, b_ref, o_ref):
    c = c_ref[...]
    a = c * jax.nn.sigmoid(c)
    o_ref[0] = jnp.dot(a, w_ref[0], preferred_element_type=F32,
                       precision=lax.Precision.HIGHEST) + b_ref[0]


def _ada_mods(cvecs, ada_w, ada_b):
    depth, d, nd = ada_w.shape
    rows = cvecs.shape[0]
    tn = 1152 if nd % 1152 == 0 else nd
    return pl.pallas_call(
        _ada_kernel,
        grid=(depth, nd // tn),
        in_specs=[pl.BlockSpec((rows, d), lambda l, j: (0, 0)),
                  pl.BlockSpec((1, d, tn), lambda l, j: (l, 0, j)),
                  pl.BlockSpec((1, 1, tn), lambda l, j: (l, 0, j))],
        out_specs=pl.BlockSpec((1, rows, tn), lambda l, j: (l, 0, j)),
        out_shape=jax.ShapeDtypeStruct((depth, rows, nd), F32),
        compiler_params=_cparams(("parallel", "parallel")),
        name="ada_mods",
    )(cvecs, ada_w, ada_b.reshape(depth, 1, nd))


def _ffn_kernel(x_ref, sh_ref, sc_ref, gt_ref, gpre_ref, gpost_ref, win_ref, wout_ref, o_ref,
                *, ffn_dim, chunk):
    tm = x_ref.shape[1]
    sub = min(tm, FFN_SUB_ROWS)
    rows = [slice(r, r + sub) for r in range(0, tm, sub)]
    xs = [x_ref[0, r, :] for r in rows]
    xms = [_norm_mod(x, gpre_ref[...], sh_ref[0], sc_ref[0]).astype(BF16) for x in xs]
    accs = [jnp.zeros(x.shape, F32) for x in xs]
    for c in range(ffn_dim // chunk):
        for i, xm in enumerate(xms):
            a = _dot(xm, win_ref[:, c * chunk:(c + 1) * chunk])
            b = _dot(xm, win_ref[:, ffn_dim + c * chunk:ffn_dim + (c + 1) * chunk])
            h = (a * jax.nn.sigmoid(a) * b).astype(BF16)
            accs[i] = accs[i] + _dot(h, wout_ref[c * chunk:(c + 1) * chunk, :])
    for r, x, acc in zip(rows, xs, accs):
        o_ref[0, r, :] = x + FFN_RES * gt_ref[0] * _rms(acc, gpost_ref[...])


def _ffn(x, shift, scale, gate, g_pre, g_post, w_in, w_out):
    bsz, n, d = x.shape
    ffn_dim = w_out.shape[0]
    tm = min(2 * FFN_SUB_ROWS, n)
    tok = pl.BlockSpec((1, tm, d), lambda b, i: (b, i, 0))
    gain = pl.BlockSpec((1, d), lambda b, i: (0, 0))
    return pl.pallas_call(
        functools.partial(_ffn_kernel, ffn_dim=ffn_dim, chunk=256),
        grid=(bsz, n // tm),
        in_specs=[tok, _mod_spec(shift), _mod_spec(scale), _mod_spec(gate), gain, gain,
                  _resident(w_in.shape), _resident(w_out.shape)],
        out_specs=tok,
        out_shape=jax.ShapeDtypeStruct(x.shape, F32),
        compiler_params=_cparams(("parallel", "parallel")),
        name="ffn",
    )(x, shift, scale, gate, g_pre, g_post, w_in, w_out)


def _rope_masks(tm):
    lane = lax.broadcasted_iota(jnp.int32, (tm, LANES), 1)
    even = ((lane // (ROPE_DIM // 4)) & 1) == 0
    lo = lane < (LANES // 2)
    return even, lo


def _rope(blk, cos, sin, even):
    q = ROPE_DIM // 4
    partner = jnp.where(even, pltpu.roll(blk, LANES - q, 1), pltpu.roll(blk, q, 1))
    return blk * cos + partner * sin


def _store_vt(vt_ref, h, v_blk):
    tm = v_blk.shape[0]
    vt_ref[0, h, 0, 0:LANES, :] = v_blk.T.astype(BF16)
    row = lax.broadcasted_iota(jnp.int32, (V_ROWS - LANES, tm), 0)
    vt_ref[0, h, 0, LANES:V_ROWS, :] = jnp.where(row == 0, 1.0, 0.0).astype(BF16)


def _kv_specs(bsz, n, tm, heads, dk):
    chunk = min(KV_CHUNK, n)
    per = chunk // tm
    specs = [pl.BlockSpec((1, heads, tm, dk), lambda b, i: (b, 0, i, 0)),
             pl.BlockSpec((1, heads, 1, V_ROWS, tm), lambda b, i: (b, 0, i // per, 0, i % per))]
    shapes = [jax.ShapeDtypeStruct((bsz, heads, n, dk), BF16),
              jax.ShapeDtypeStruct((bsz, heads, n // chunk, V_ROWS, chunk), BF16)]
    return specs, shapes


def _proj_diff_kernel(x_ref, sh_ref, sc_ref, g_ref, cos_ref, sin_ref, w_ref,
                      qt_ref, k_ref, vt_ref):
    hm = _norm_mod(x_ref[0], g_ref[...], sh_ref[0], sc_ref[0]).astype(BF16)
    tm = hm.shape[0]
    cos = cos_ref[...]
    sin = sin_ref[...]
    even, _ = _rope_masks(tm)
    hw = DIFF_HEADS * 2 * DIFF_D
    q = _dot(hm, w_ref[:, 0:hw])
    k = _dot(hm, w_ref[:, hw:2 * hw])
    v = _dot(hm, w_ref[:, 2 * hw:3 * hw])
    scale = LOG2E / math.sqrt(DIFF_D)
    top = lax.broadcasted_iota(jnp.int32, (LANES, tm), 0) < DIFF_D
    for h in range(DIFF_HEADS):
        hs = slice(h * LANES, (h + 1) * LANES)
        qbt = (_rope(q[:, hs], cos, sin, even) * scale).T
        qt_ref[0, h, 0, :, 0:tm] = jnp.where(top, qbt, 0.0).astype(BF16)
        qt_ref[0, h, 0, :, tm:2 * tm] = jnp.where(top, 0.0, qbt).astype(BF16)
        k_ref[0, h] = _rope(k[:, hs], cos, sin, even).astype(BF16)
        _store_vt(vt_ref, h, v[:, hs])


def _proj_diff(x, shift, scale, g, cos, sin, w):
    bsz, n, d = x.shape
    tm = min(FLASH_TQ, n)
    tok = pl.BlockSpec((1, tm, d), lambda b, i: (b, i, 0))
    tab = pl.BlockSpec((tm, LANES), lambda b, i: (i, 0))
    kv_specs, kv_shapes = _kv_specs(bsz, n, tm, DIFF_HEADS, LANES)
    return pl.pallas_call(
        _proj_diff_kernel,
        grid=(bsz, n // tm),
        in_specs=[tok, _mod_spec(shift), _mod_spec(scale),
                  pl.BlockSpec((1, d), lambda b, i: (0, 0)), tab, tab, _resident(w.shape)],
        out_specs=[pl.BlockSpec((1, DIFF_HEADS, 1, LANES, 2 * tm), lambda b, i: (b, 0, i, 0, 0))]
        + kv_specs,
        out_shape=[jax.ShapeDtypeStruct((bsz, DIFF_HEADS, n // tm, LANES, 2 * tm), BF16)] + kv_shapes,
        compiler_params=_cparams(("parallel", "parallel")),
        name="proj_diff",
    )(x, shift, scale, g, cos, sin, w)


_WQ0, _WK0, _WV0 = 0, 1024, 1280
_CQ0, _CKV0, _KR0, _PB_COLS = 1536, 1920, 2176, 2304


def _proj_wm_kernel(x_ref, sh_ref, sc_ref, g_ref, cos_ref, sin_ref, w_ref, wuq_ref, wukv_ref,
                    qn_ref, kvn_ref, wqt_ref, wk_ref, wvt_ref, mqt_ref, mk_ref, mvt_ref):
    hm = _norm_mod(x_ref[0], g_ref[...], sh_ref[0], sc_ref[0]).astype(BF16)
    tm = hm.shape[0]
    cos = cos_ref[...]
    sin = sin_ref[...]
    even, lo = _rope_masks(tm)
    half = LANES // 2
    p = _dot(hm, w_ref[...])

    def two_heads(blk):
        return jnp.where(lo, blk, 0.0), jnp.where(lo, pltpu.roll(blk, half, 1), 0.0)

    for j in range(WIN_Q_HEADS // 2):
        blk = _rope(p[:, _WQ0 + j * LANES:_WQ0 + (j + 1) * LANES], cos, sin, even) * (WIN_SCALE * LOG2E)
        for sub, padded in enumerate(two_heads(blk)):
            kvh, grp = divmod(2 * j + sub, WIN_GROUP)
            qt = padded.T.astype(BF16)
            for qb in range(tm // QBLK):
                wqt_ref[0, kvh, qb, :, grp * QBLK:(grp + 1) * QBLK] = qt[:, qb * QBLK:(qb + 1) * QBLK]
    for j in range(WIN_KV_HEADS // 2):
        kblk = two_heads(_rope(p[:, _WK0 + j * LANES:_WK0 + (j + 1) * LANES], cos, sin, even))
        vblk = two_heads(p[:, _WV0 + j * LANES:_WV0 + (j + 1) * LANES])
        for sub in range(2):
            kvh = 2 * j + sub
            wk_ref[0, :, kvh * LANES:(kvh + 1) * LANES] = kblk[sub].astype(BF16)
            wvt_ref[0, kvh, 0:LANES, :] = vblk[sub].T.astype(BF16)
            row = lax.broadcasted_iota(jnp.int32, (V_ROWS - LANES, tm), 0)
            wvt_ref[0, kvh, LANES:V_ROWS, :] = jnp.where(row == 0, 1.0, 0.0).astype(BF16)

    cq = _rms(p[:, _CQ0:_CQ0 + MLA_Q_RANK], qn_ref[...]).astype(BF16)
    q2 = _dot(cq, wuq_ref[...])
    ckv = _rms(p[:, _CKV0:_CKV0 + MLA_KV_RANK], kvn_ref[...]).astype(BF16)
    kv = _dot(ckv, wukv_ref[...])
    kr = _rope(p[:, _KR0:_KR0 + LANES], cos, sin, even).astype(BF16)
    for h in range(MLA_HEADS):
        c0 = 2 * h * LANES
        qn = q2[:, c0:c0 + LANES] * (MLA_SCALE * LOG2E)
        qr = _rope(q2[:, c0 + LANES:c0 + 2 * LANES], cos, sin, even) * (MLA_SCALE * LOG2E)
        mqt_ref[0, h, 0, 0:LANES, :] = qn.T.astype(BF16)
        mqt_ref[0, h, 0, LANES:2 * LANES, :] = qr.T.astype(BF16)
        mk_ref[0, h, :, 0:LANES] = kv[:, c0:c0 + LANES].astype(BF16)
        mk_ref[0, h, :, LANES:2 * LANES] = kr
        _store_vt(mvt_ref, h, kv[:, c0 + LANES:c0 + 2 * LANES])


def _proj_wm(x, shift, scale, g, cos, sin, w, wuq, wukv, qn, kvn):
    bsz, n, d = x.shape
    tm = min(FLASH_TQ, n)
    tok = lambda c: pl.BlockSpec((1, tm, c), lambda b, i: (b, i, 0))
    tab = pl.BlockSpec((tm, LANES), lambda b, i: (i, 0))
    row = lambda c: pl.BlockSpec((1, c), lambda b, i: (0, 0))
    kv_specs, kv_shapes = _kv_specs(bsz, n, tm, MLA_HEADS, 2 * LANES)
    qrow = WIN_GROUP * QBLK
    win_specs = [pl.BlockSpec((1, WIN_KV_HEADS, tm // QBLK, LANES, qrow), lambda b, i: (b, 0, i, 0, 0)),
                 tok(WIN_KV_HEADS * LANES),
                 pl.BlockSpec((1, WIN_KV_HEADS, V_ROWS, tm), lambda b, i: (b, 0, 0, i))]
    win_shapes = [jax.ShapeDtypeStruct((bsz, WIN_KV_HEADS, n // QBLK, LANES, qrow), BF16),
                  jax.ShapeDtypeStruct((bsz, n, WIN_KV_HEADS * LANES), BF16),
                  jax.ShapeDtypeStruct((bsz, WIN_KV_HEADS, V_ROWS, n), BF16)]
    return pl.pallas_call(
        _proj_wm_kernel,
        grid=(bsz, n // tm),
        in_specs=[tok(d), _mod_spec(shift), _mod_spec(scale), row(d), tab, tab,
                  _resident(w.shape), _resident(wuq.shape), _resident(wukv.shape),
                  row(MLA_Q_RANK), row(MLA_KV_RANK)],
        out_specs=win_specs
        + [pl.BlockSpec((1, MLA_HEADS, 1, 2 * LANES, tm), lambda b, i: (b, 0, i, 0, 0))] + kv_specs,
        out_shape=win_shapes
        + [jax.ShapeDtypeStruct((bsz, MLA_HEADS, n // tm, 2 * LANES, tm), BF16)] + kv_shapes,
        compiler_params=_cparams(("parallel", "parallel")),
        name="proj_wm",
    )(x, shift, scale, g, cos, sin, w, wuq, wukv, qn, kvn)


def _segments(seg_refs):
    segs = []
    for k_ref, vt_ref in zip(seg_refs[0::2], seg_refs[1::2]):
        segs.append((k_ref, vt_ref, vt_ref.shape[2], vt_ref.shape[4]))
    return segs


def _item_rows(tk, w):
    return min(tk, max(FLASH_ITEM_ELEMS // w, 2 * LANES))


def _flash_loop_exact(qt, segs, m_sc, acc_sc):
    m_sc[...] = jnp.full(m_sc.shape, NEG_INF, F32)
    acc_sc[...] = jnp.zeros(acc_sc.shape, F32)
    for k_ref, vt_ref, nc, tk in segs:
        rows = _item_rows(tk, qt.shape[1])

        def body(j, carry, k_ref=k_ref, vt_ref=vt_ref, tk=tk, rows=rows):
            for r0 in range(0, tk, rows):
                start = pl.multiple_of(j * tk + r0, rows)
                s = _dot(k_ref[0, 0, pl.ds(start, rows), :], qt)
                m_prev = m_sc[...]
                m_new = jnp.maximum(m_prev, jnp.max(s, axis=0, keepdims=True))
                p = jnp.exp2(s - m_new).astype(BF16)
                acc_sc[...] = (jnp.exp2(m_prev - m_new) * acc_sc[...]
                               + _dot(vt_ref[0, 0, j][:, r0:r0 + rows], p))
                m_sc[...] = m_new
            return carry
        lax.fori_loop(0, nc, body, 0)


def _flash_loop_spec(qt, segs, acc_sc):
    probe = _dot(segs[0][0][0, 0, 0:SPEC_PROBE_ROWS, :], qt)
    m0 = jnp.max(probe, axis=0, keepdims=True)
    acc_sc[...] = jnp.zeros(acc_sc.shape, F32)
    items = []
    for k_ref, vt_ref, nc, tk in segs:
        rows = _item_rows(tk, qt.shape[1])
        items += [(k_ref, vt_ref, j, j * tk + r0, r0, rows) for j in range(nc) for r0 in range(0, tk, rows)]

    def scores(item):
        k_ref, _, _, row0, _, rows = item
        return _dot(k_ref[0, 0, row0:row0 + rows, :], qt)

    s = scores(items[0])
    for idx, (_, vt_ref, j, _, r0, rows) in enumerate(items):
        pf = jnp.exp2(s - m0)
        lsum = jnp.sum(pf, axis=0, keepdims=True)
        p = pf.astype(BF16)
        if idx + 1 < len(items):
            s = scores(items[idx + 1])
        acc_sc[0:LANES, :] += _dot(vt_ref[0, 0, j][0:LANES, r0:r0 + rows], p)
        acc_sc[LANES:LANES + 1, :] += lsum


def _flash_tile(qt, segs, m_sc, acc_sc):
    _flash_loop_spec(qt, segs, acc_sc)
    finite = jnp.where(jnp.abs(acc_sc[0:LANES + 1, :]) <= SPEC_FINITE_LIMIT, 1.0, 0.0)

    @pl.when(jnp.min(finite) < 0.5)
    def _redo():
        _flash_loop_exact(qt, segs, m_sc, acc_sc)


def _diff_flash_kernel(*refs, nseg, lambda_init):
    qt_ref, seg_refs = refs[0], refs[1:1 + 2 * nseg]
    dl_ref, g_ref, o_ref, m_sc, acc_sc = refs[1 + 2 * nseg:]
    tiles = qt_ref.shape[2]
    qt = jnp.concatenate([qt_ref[0, 0, t] for t in range(tiles)], axis=1)
    _flash_tile(qt, _segments(seg_refs), m_sc, acc_sc)
    tq = o_ref.shape[1] // tiles
    dl = dl_ref[...]
    lam = (jnp.exp(jnp.sum(dl[0:1] * dl[1:2], axis=1, keepdims=True))
           - jnp.exp(jnp.sum(dl[2:3] * dl[3:4], axis=1, keepdims=True)) + lambda_init)
    for t in range(tiles):
        lanes = slice(2 * t * tq, 2 * (t + 1) * tq)
        o = acc_sc[0:LANES, lanes] / acc_sc[LANES:LANES + 1, lanes]
        yt = o[:, :tq] - lam * o[:, tq:]
        yt = yt * lax.rsqrt(jnp.mean(yt * yt, axis=0, keepdims=True) + EPS)
        o_ref[0, t * tq:(t + 1) * tq, :] = (yt.T * g_ref[...] * (1.0 - lambda_init)).astype(BF16)


def _mla_flash_kernel(*refs, nseg):
    qt_ref, seg_refs = refs[0], refs[1:1 + 2 * nseg]
    o_ref, m_sc, acc_sc = refs[1 + 2 * nseg:]
    qt = jnp.concatenate([qt_ref[0, 0, t] for t in range(qt_ref.shape[2])], axis=1)
    _flash_tile(qt, _segments(seg_refs), m_sc, acc_sc)
    o_ref[0] = (acc_sc[0:LANES, :] / acc_sc[LANES:LANES + 1, :]).T.astype(BF16)


def _flash(mode, qt, kv_segs, extra=(), lambda_init=0.0):
    bsz, heads, nq, dk, w = qt.shape
    maps = 2 if mode == "diff" else 1
    tps = FLASH_TILES_PER_STEP if nq % FLASH_TILES_PER_STEP == 0 else 1
    tq = tps * w // maps
    nq //= tps
    in_specs = [pl.BlockSpec((1, 1, tps, dk, w), lambda b, h, i: (b, h, i, 0, 0))]
    w *= tps
    args = [qt]
    for k, vt in kv_segs:
        in_specs += [pl.BlockSpec((1, 1) + k.shape[2:], lambda b, h, i: (b, h, 0, 0)),
                     pl.BlockSpec((1, 1) + vt.shape[2:], lambda b, h, i: (b, h, 0, 0, 0))]
        args += [k, vt]
    if mode == "diff":
        kern = functools.partial(_diff_flash_kernel, nseg=len(kv_segs), lambda_init=lambda_init)
        in_specs += [pl.BlockSpec(e.shape, lambda b, h, i: (0, 0)) for e in extra]
    else:
        kern = functools.partial(_mla_flash_kernel, nseg=len(kv_segs))
    return pl.pallas_call(
        kern,
        grid=(bsz, heads, nq),
        in_specs=in_specs,
        out_specs=pl.BlockSpec((1, tq, LANES), lambda b, h, i: (b, i, h)),
        out_shape=jax.ShapeDtypeStruct((bsz, nq * tq, heads * LANES), BF16),
        scratch_shapes=[pltpu.VMEM((1, w), F32), pltpu.VMEM((V_ROWS, w), F32)],
        compiler_params=_cparams(("parallel", "parallel", "arbitrary")),
        name=f"{mode}_flash",
    )(*args, *extra)


def _win_kernel(*refs, band, nblk):
    if band:
        (q_ref, kp_ref, ko_ref, kn_ref, vp_ref, vo_ref, vn_ref, kc_ref, vc_ref,
         sink_ref, o_ref) = refs
    else:
        q_ref, kc_ref, vc_ref, sink_ref, o_ref = refs
    t = QBLK
    w = WIN_GROUP * t
    bps = q_ref.shape[2]
    if band:
        i = pl.program_id(1)
        kk = lax.broadcasted_iota(jnp.int32, (t, w), 0)
        qq = lax.broadcasted_iota(jnp.int32, (t, w), 1) % t
    for r in range(bps):
        rows = slice(r * t, (r + 1) * t)
        if band:
            ok_prev = kk >= qq + (jnp.where(i > 0, 0, t) if r == 0 else 0)
            ok_next = kk <= qq - (jnp.where(i < nblk // bps - 1, 0, t) if r == bps - 1 else 0)
        parts, sinks, maxes = [], [], []
        for kvh in range(WIN_KV_HEADS):
            ksl = slice(kvh * LANES, (kvh + 1) * LANES)
            h0 = kvh * WIN_GROUP
            qt = q_ref[0, kvh, r]
            sink = jnp.concatenate([sink_ref[h0 + g:h0 + g + 1, :] for g in range(WIN_GROUP)],
                                   axis=1) * LOG2E
            s_all = [_dot(kc_ref[0, :, ksl], qt)]
            if band:
                k_prev = kp_ref[0, :, ksl] if r == 0 else ko_ref[0, (r - 1) * t:r * t, ksl]
                k_next = kn_ref[0, :, ksl] if r == bps - 1 else ko_ref[0, (r + 1) * t:(r + 2) * t, ksl]
                s_all = [jnp.where(ok_prev, _dot(k_prev, qt), NEG_INF),
                         _dot(ko_ref[0, rows, ksl], qt),
                         jnp.where(ok_next, _dot(k_next, qt), NEG_INF)] + s_all
            m = sink
            for part in s_all:
                m = jnp.maximum(m, jnp.max(part, axis=0, keepdims=True))
            parts.append(s_all)
            sinks.append(sink)
            maxes.append(m)
        outs = []
        for kvh in range(WIN_KV_HEADS):
            m, sink = maxes[kvh], sinks[kvh]
            p = [jnp.exp2(part - m).astype(BF16) for part in parts[kvh]]
            if band:
                p = jnp.concatenate(p, axis=0)
                v_prev = vp_ref[0, kvh] if r == 0 else vo_ref[0, kvh, :, (r - 1) * t:r * t]
                v_next = vn_ref[0, kvh] if r == bps - 1 else vo_ref[0, kvh, :, (r + 1) * t:(r + 2) * t]
                vt = jnp.concatenate([v_prev, vo_ref[0, kvh, :, rows], v_next, vc_ref[0, kvh]],
                                     axis=1)
            else:
                p, vt = p[0], vc_ref[0, kvh]
            pv = _dot(vt, p)
            outs.append(pv[0:LANES, :] / (pv[LANES:LANES + 1, :] + jnp.exp2(sink - m)))
        for kvh in range(WIN_KV_HEADS):
            o = outs[kvh]
            for pair in range(WIN_GROUP // 2):
                a = o[0:WIN_HEAD_DIM, (2 * pair) * t:(2 * pair + 1) * t]
                b = o[0:WIN_HEAD_DIM, (2 * pair + 1) * t:(2 * pair + 2) * t]
                blk = kvh * (WIN_GROUP // 2) + pair
                o_ref[0, rows, blk * LANES:(blk + 1) * LANES] = jnp.concatenate([a, b], axis=0).T.astype(BF16)


def _win_attn(qt, k, vt, kc, vtc, sink, band):
    bsz, _, nblk, _, qrow = qt.shape
    kw = kc.shape[2]
    nctx = kc.shape[1]
    t = QBLK
    out_w = WIN_Q_HEADS * WIN_HEAD_DIM
    own = lambda b, i: (b, i, 0)
    bps = WIN_BLOCKS_PER_STEP if (band and nblk % WIN_BLOCKS_PER_STEP == 0) else 1
    in_specs = [pl.BlockSpec((1, WIN_KV_HEADS, bps, LANES, qrow), lambda b, i: (b, 0, i, 0, 0))]
    args = [qt]
    if band:
        prev = lambda b, i: (b, jnp.maximum(bps * i - 1, 0), 0)
        nxt = lambda b, i: (b, jnp.minimum(bps * i + bps, nblk - 1), 0)
        kspec = lambda f, n: pl.BlockSpec((1, n, kw), f)
        vspec = lambda f, n: pl.BlockSpec((1, WIN_KV_HEADS, V_ROWS, n),
                                          lambda b, i, f=f: (b, 0, 0, f(b, i)[1]))
        in_specs += [kspec(prev, t), kspec(own, bps * t), kspec(nxt, t),
                     vspec(prev, t), vspec(own, bps * t), vspec(nxt, t)]
        args += [k, k, k, vt, vt, vt]
    in_specs += [pl.BlockSpec((1, nctx, kw), lambda b, i: (b, 0, 0)),
                 pl.BlockSpec((1, WIN_KV_HEADS, V_ROWS, nctx), lambda b, i: (b, 0, 0, 0)),
                 pl.BlockSpec(sink.shape, lambda b, i: (0, 0))]
    args += [kc, vtc, sink]
    return pl.pallas_call(
        functools.partial(_win_kernel, band=band, nblk=nblk),
        grid=(bsz, nblk // bps),
        in_specs=in_specs,
        out_specs=pl.BlockSpec((1, bps * t, out_w), own),
        out_shape=jax.ShapeDtypeStruct((bsz, nblk * t, out_w), BF16),
        compiler_params=_cparams(("parallel", "parallel")),
        name="win_attn" if band else "win_attn_ctx",
    )(*args)


def _merge_kernel(x_ref, ya_ref, yb_ref, yc_ref, sh_ref, sc_ref, gt_ref, g2_ref, g3_ref,
                  wg_ref, bw_ref, wo_ref, o_ref):
    x = x_ref[0]
    d = x.shape[1]
    hm = _norm_mod(x, g2_ref[...], sh_ref[0], sc_ref[0]).astype(BF16)
    merged = jnp.zeros(x.shape, F32)
    for i, y_ref in enumerate((ya_ref, yb_ref, yc_ref)):
        gate = jax.nn.sigmoid(_dot(hm, wg_ref[:, i * d:(i + 1) * d]))
        merged = merged + gate * _dot(y_ref[0], bw_ref[i])
    y = _dot(merged.astype(BF16), wo_ref[...])
    o_ref[0] = x + gt_ref[0] * _rms(y, g3_ref[...])


def _merge(x, ya, yb, yc, shift, scale, gate, g2, g3, wg, bw, wo):
    bsz, n, d = x.shape
    tm = min(512, n)
    tok = pl.BlockSpec((1, tm, d), lambda b, i: (b, i, 0))
    gain = pl.BlockSpec((1, d), lambda b, i: (0, 0))
    return pl.pallas_call(
        _merge_kernel,
        grid=(bsz, n // tm),
        in_specs=[tok, tok, tok, tok, _mod_spec(shift), _mod_spec(scale), _mod_spec(gate),
                  gain, gain, _resident(wg.shape), _resident(bw.shape), _resident(wo.shape)],
        out_specs=tok,
        out_shape=jax.ShapeDtypeStruct(x.shape, F32),
        compiler_params=_cparams(("parallel", "parallel")),
        name="merge",
    )(x, ya, yb, yc, shift, scale, gate, g2, g3, wg, bw, wo)


def _rope_tables(n):
    quarter = ROPE_DIM // 4
    pos = jnp.arange(n)
    row = (pos // GRID_W).astype(F32)
    col = (pos % GRID_W).astype(F32)
    inv_freq = 1.0 / (ROPE_BASE ** (jnp.arange(quarter, dtype=F32) / quarter))
    ang_r = row[:, None] * inv_freq
    ang_c = col[:, None] * inv_freq
    cos = jnp.concatenate([jnp.cos(ang_r)] * 2 + [jnp.cos(ang_c)] * 2, axis=1)
    sin = jnp.concatenate([-jnp.sin(ang_r), jnp.sin(ang_r), -jnp.sin(ang_c), jnp.sin(ang_c)], axis=1)
    return jnp.tile(cos, (1, 2)), jnp.tile(sin, (1, 2))


def _layer_weights(l, ffn_w_in, ffn_w_out, mix_w_in, mla_w_uq, mla_w_ukv, branch_w, mix_w_out):
    d = mix_w_in.shape[1]
    w = mix_w_in[l]
    hw = DIFF_HEADS * 2 * DIFF_D
    wq_n = WIN_Q_HEADS * WIN_HEAD_DIM
    wk_n = WIN_KV_HEADS * WIN_HEAD_DIM
    c = 3 * hw
    w_diff = w[:, :c]
    rest_n = wq_n + 2 * wk_n + MLA_Q_RANK + MLA_KV_RANK + MLA_ROPE
    w_wm = jnp.concatenate([w[:, c:c + rest_n], jnp.zeros((d, _PB_COLS - rest_n), w.dtype)], axis=1)
    w_gate = w[:, c + rest_n:]
    wuq = mla_w_uq[l].reshape(MLA_Q_RANK, MLA_HEADS, MLA_NOPE + MLA_ROPE)
    wuq = jnp.pad(wuq, ((0, 0), (0, 0), (0, 2 * LANES - MLA_NOPE - MLA_ROPE)))
    wuq = wuq.reshape(MLA_Q_RANK, MLA_HEADS * 2 * LANES)
    return dict(
        ffn_in=[ffn_w_in[l, i].astype(BF16) for i in range(2)],
        ffn_out=[ffn_w_out[l, i].astype(BF16) for i in range(2)],
        w_diff=w_diff.astype(BF16), w_wm=w_wm.astype(BF16), w_gate=w_gate.astype(BF16),
        wuq=wuq.astype(BF16), wukv=mla_w_ukv[l].astype(BF16),
        bw=branch_w[l].astype(BF16), wo=mix_w_out[l].astype(BF16))


def kernel(x, c, ctx, c_ctx, ada_w, ada_b, norm_g, ffn_w_in, ffn_w_out, mix_w_in, diff_lambda,
           diff_subln_g, win_sink, mla_q_norm_g, mla_kv_norm_g, mla_w_uq, mla_w_ukv, branch_w,
           mix_w_out):
    bsz, s, d = x.shape
    nctx = ctx.shape[1]
    depth = ada_w.shape[0]

    cvecs = jnp.zeros((8, d), F32).at[:bsz].set(c).at[bsz].set(c_ctx)
    mods = _ada_mods(cvecs, ada_w, ada_b).reshape(depth, 8, N_MOD, d)

    cos_x, sin_x = _rope_tables(s)
    cos_c = jnp.ones((nctx, LANES), F32)
    sin_c = jnp.zeros((nctx, LANES), F32)

    h = ctx
    for l in range(depth):
        last = l == depth - 1
        lambda_init = 0.8 - 0.6 * math.exp(-0.3 * l)
        wts = _layer_weights(l, ffn_w_in, ffn_w_out, mix_w_in, mla_w_uq, mla_w_ukv, branch_w,
                             mix_w_out)
        mx = [mods[l, :bsz, k][:, None, :] for k in range(N_MOD)]
        mc = [mods[l, bsz:bsz + 1, k][:, None, :] for k in range(N_MOD)]
        g = [norm_g[l, k][None, :] for k in range(6)]
        qn = mla_q_norm_g[l][None, :]
        kvn = mla_kv_norm_g[l][None, :]
        dl = diff_lambda[l].astype(F32)
        subln = diff_subln_g[l][None, :]
        sink = jnp.broadcast_to(win_sink[l].astype(F32)[:, None], (WIN_Q_HEADS, LANES))

        x = _ffn(x, mx[0], mx[1], mx[2], g[0], g[1], wts["ffn_in"][0], wts["ffn_out"][0])
        h = _ffn(h, mc[0], mc[1], mc[2], g[0], g[1], wts["ffn_in"][0], wts["ffn_out"][0])

        dqt_x, dk_x, dvt_x = _proj_diff(x, mx[3], mx[4], g[2], cos_x, sin_x, wts["w_diff"])
        dqt_c, dk_c, dvt_c = _proj_diff(h, mc[3], mc[4], g[2], cos_c, sin_c, wts["w_diff"])
        wqt_x, wk_x, wvt_x, mqt_x, mk_x, mvt_x = _proj_wm(
            x, mx[3], mx[4], g[2], cos_x, sin_x, wts["w_wm"], wts["wuq"], wts["wukv"], qn, kvn)
        wqt_c, wk_c, wvt_c, mqt_c, mk_c, mvt_c = _proj_wm(
            h, mc[3], mc[4], g[2], cos_c, sin_c, wts["w_wm"], wts["wuq"], wts["wukv"], qn, kvn)

        ya = _flash("diff", dqt_x, [(dk_x, dvt_x), (dk_c, dvt_c)], (dl, subln), lambda_init)
        yb = _win_attn(wqt_x, wk_x, wvt_x, wk_c, wvt_c, sink, band=True)
        yc = _flash("mla", mqt_x, [(mk_x, mvt_x), (mk_c, mvt_c)])
        x_new = _merge(x, ya, yb, yc, mx[3], mx[4], mx[5], g[2], g[3],
                       wts["w_gate"], wts["bw"], wts["wo"])
        if not last:
            ca = _flash("diff", dqt_c, [(dk_c, dvt_c)], (dl, subln), lambda_init)
            cb = _win_attn(wqt_c, None, None, wk_c, wvt_c, sink, band=False)
            cc = _flash("mla", mqt_c, [(mk_c, mvt_c)])
            h = _merge(h, ca, cb, cc, mc[3], mc[4], mc[5], g[2], g[3],
                       wts["w_gate"], wts["bw"], wts["wo"])
            h = _ffn(h, mc[6], mc[7], mc[8], g[4], g[5], wts["ffn_in"][1], wts["ffn_out"][1])
        x = _ffn(x_new, mx[6], mx[7], mx[8], g[4], g[5], wts["ffn_in"][1], wts["ffn_out"][1])
    return x
```
